```python
import jax, jax.numpy as jnp
from jax import lax
import numpy as np

D_MODEL = 4096
BATCH = 2
SEQ = 8192
DEPTH = 4

ROPE_THETA = 10000.0
EPS = 1e-6
Q_BLOCK = 128
MAX_POS_OFFSET = 1024
NEG = -1e30
NEG_HALF = -5e29
FORCE = 1e9

MLA_HEADS = 16
MLA_Q_RANK = 1536
MLA_KV_RANK = 512
MLA_NOPE = 128
MLA_ROPE = 64
MLA_V = 128
MLA_IN = MLA_Q_RANK + MLA_KV_RANK + MLA_ROPE

NSA_HEADS = 16
NSA_KV_GROUPS = 4
NSA_HEAD_DIM = 128
NSA_CMP_LEN = 32
NSA_CMP_STRIDE = 16
NSA_SLC_LEN = 64
NSA_N_SEL = 16
NSA_WINDOW = 512
NSA_Q_BLOCK = 64
NSA_IN = NSA_HEADS * NSA_HEAD_DIM + 6 * NSA_KV_GROUPS * NSA_HEAD_DIM + 3 * NSA_HEADS

EVEN_IN = MLA_IN + NSA_IN
EVEN_MIX = MLA_HEADS * MLA_V + NSA_HEADS * NSA_HEAD_DIM

DSA_HEADS = 32
DSA_KV_HEADS = 8
DSA_HEAD_DIM = 128
IDX_HEADS = 32
IDX_DIM = 64
DSA_TOPK_MAX = 256
ODD_IN = DSA_HEADS * DSA_HEAD_DIM + 2 * DSA_KV_HEADS * DSA_HEAD_DIM + IDX_HEADS * IDX_DIM + IDX_DIM + IDX_HEADS
ODD_MIX = DSA_HEADS * DSA_HEAD_DIM

D_FF = 8192
N_EXPERTS = 8
TOP_K = 2
D_EXPERT = 1408

kernel_name = 'hybrid_mla_nsa_dsa_moe_trunk'


def rms_norm(x, g):
    xf = x.astype(jnp.float32)
    y = xf * lax.rsqrt(jnp.mean(xf * xf, axis=-1, keepdims=True) + EPS)
    return (y * g.astype(jnp.float32)).astype(x.dtype)


def rope_angles(pos, dim):
    inv = ROPE_THETA ** (-jnp.arange(0, dim, 2, dtype=jnp.float32) / dim)
    ang = pos.astype(jnp.float32)[..., None] * inv
    return jnp.cos(ang), jnp.sin(ang)


def apply_rope(x, cos, sin):
    half = x.shape[-1] // 2
    c = cos[:, :, None, :].astype(x.dtype)
    s = sin[:, :, None, :].astype(x.dtype)
    x1, x2 = x[..., :half], x[..., half:]
    return jnp.concatenate([x1 * c - x2 * s, x1 * s + x2 * c], axis=-1)


def masked_softmax(s, mask):
    p = jax.nn.softmax(jnp.where(mask, s.astype(jnp.float32), NEG), axis=-1)
    return p * jnp.any(mask, axis=-1, keepdims=True)


def sweep(fn, n_blocks):
    out = jnp.moveaxis(lax.map(fn, jnp.arange(n_blocks)), 0, 1)
    return out.reshape(out.shape[0], -1, out.shape[-1])


def mla_mixer(p, positions, q_a_norm, kv_a_norm, w_uq, w_ukv, q_norm, k_norm):
    B, S, _ = p.shape
    cq, ckv, kr = jnp.split(p, [MLA_Q_RANK, MLA_Q_RANK + MLA_KV_RANK], axis=-1)
    cos, sin = rope_angles(positions, MLA_ROPE)
    q = (rms_norm(cq, q_a_norm) @ w_uq).reshape(B, S, MLA_HEADS, MLA_NOPE + MLA_ROPE)
    kv = (rms_norm(ckv, kv_a_norm) @ w_ukv).reshape(B, S, MLA_HEADS, MLA_NOPE + MLA_V)
    q_nope = rms_norm(q[..., :MLA_NOPE], q_norm[:MLA_NOPE])
    q_rope = apply_rope(rms_norm(q[..., MLA_NOPE:], q_norm[MLA_NOPE:]), cos, sin)
    k_nope = rms_norm(kv[..., :MLA_NOPE], k_norm[:MLA_NOPE])
    k_rope = apply_rope(rms_norm(kr, k_norm[MLA_NOPE:])[:, :, None, :], cos, sin)[:, :, 0]
    v = kv[..., MLA_NOPE:]
    scale = (MLA_NOPE + MLA_ROPE) ** -0.5
    kpos = jnp.arange(S)

    def block(i):
        t0 = i * Q_BLOCK
        qn = lax.dynamic_slice_in_dim(q_nope, t0, Q_BLOCK, axis=1)
        qr = lax.dynamic_slice_in_dim(q_rope, t0, Q_BLOCK, axis=1)
        s = (jnp.einsum('bqhd,bkhd->bhqk', qn, k_nope) + jnp.einsum('bqhr,bkr->bhqk', qr, k_rope)) * scale
        mask = (t0 + jnp.arange(Q_BLOCK))[:, None] >= kpos[None, :]
        pr = masked_softmax(s, mask)
        o = jnp.einsum('bhqk,bkhd->bqhd', pr.astype(v.dtype), v)
        return o.reshape(B, Q_BLOCK, MLA_HEADS * MLA_V)

    return sweep(block, S // Q_BLOCK)


def nsa_mixer(p, positions, qk_norm, cmp_pos, cmp_w):
    B, S, _ = p.shape
    G, DH = NSA_KV_GROUPS, NSA_HEAD_DIM
    HPG = NSA_HEADS // G
    nq, nkv = NSA_HEADS * DH, G * DH
    cos, sin = rope_angles(positions, DH)
    q = p[..., :nq].reshape(B, S, NSA_HEADS, DH)
    kv = p[..., nq:nq + 6 * nkv].reshape(B, S, 6, G, DH)
    gates = jax.nn.sigmoid(p[..., nq + 6 * nkv:].astype(jnp.float32)).reshape(B, S, G, HPG, 3)
    k_c, v_c, k_s, v_s, k_w, v_w = [kv[:, :, j] for j in range(6)]
    q = apply_rope(rms_norm(q, qk_norm[0]), cos, sin)
    k_s = apply_rope(rms_norm(k_s, qk_norm[2]), cos, sin)
    k_w = apply_rope(rms_norm(k_w, qk_norm[3]), cos, sin)

    n_cmp = (S - NSA_CMP_LEN) // NSA_CMP_STRIDE + 1
    cidx = jnp.arange(n_cmp)[:, None] * NSA_CMP_STRIDE + jnp.arange(NSA_CMP_LEN)[None, :]

    def compress(t, pe, w):
        blocks = t[:, cidx] + pe[None, None, :, None, :].astype(t.dtype)
        return jnp.einsum('bnlgd,lde->bnge', blocks, w)

    cmp_end = cidx[:, -1]
    ccos, csin = rope_angles(positions[:, cmp_end], DH)
    kc = apply_rope(rms_norm(compress(k_c, cmp_pos[0], cmp_w[0]), qk_norm[1]), ccos, csin)
    vc = compress(v_c, cmp_pos[1], cmp_w[1])

    n_slc = S // NSA_SLC_LEN
    n_sel = min(NSA_N_SEL, n_slc)
    r = NSA_SLC_LEN // NSA_CMP_STRIDE
    cl = NSA_CMP_LEN // NSA_CMP_STRIDE
    ci = (jnp.arange(n_slc)[:, None, None] * r + jnp.arange(r)[None, :, None]
          + jnp.arange(cl)[None, None, :] - (cl - 1)).reshape(n_slc, -1)
    cj = jnp.broadcast_to(jnp.arange(n_slc)[:, None], ci.shape)
    ok = ((ci >= 0) & (ci < n_cmp)).astype(jnp.float32)
    cmp_to_slc = jnp.zeros((n_cmp, n_slc), jnp.float32).at[jnp.clip(ci, 0, n_cmp - 1), cj].add(ok)
    ks_blk = k_s.reshape(B, n_slc, NSA_SLC_LEN, G, DH).transpose(0, 3, 1, 2, 4)
    vs_blk = v_s.reshape(B, n_slc, NSA_SLC_LEN, G, DH).transpose(0, 3, 1, 2, 4)

    pad = ((0, 0), (NSA_WINDOW, 0), (0, 0), (0, 0))
    kw_pad = jnp.pad(k_w, pad)
    vw_pad = jnp.pad(v_w, pad)

    bi = jnp.arange(B)[:, None, None, None]
    gi = jnp.arange(G)[None, :, None, None]
    blk = jnp.arange(n_slc)
    scale = DH ** -0.5

    def block(i):
        t0 = i * NSA_Q_BLOCK
        tq = t0 + jnp.arange(NSA_Q_BLOCK)
        qb = lax.dynamic_slice_in_dim(q, t0, NSA_Q_BLOCK, axis=1).reshape(B, NSA_Q_BLOCK, G, HPG, DH)
        gb = lax.dynamic_slice_in_dim(gates, t0, NSA_Q_BLOCK, axis=1).astype(q.dtype)
        s_c = jnp.einsum('bqghd,bngd->bghqn', qb, kc) * scale
        p_c = masked_softmax(s_c, cmp_end[None, :] <= tq[:, None])
        o_c = jnp.einsum('bghqn,bngd->bqghd', p_c.astype(vc.dtype), vc)
        imp = jnp.einsum('bghqn,nj->bgqj', p_c, cmp_to_slc)
        cur = tq // NSA_SLC_LEN
        adm = blk[None, :] <= cur[:, None]
        forced = (blk[None, :] == 0) | (blk[None, :] == cur[:, None]) | (blk[None, :] == cur[:, None] - 1)
        imp = jnp.where(forced, FORCE, jnp.where(adm, imp, NEG))
        vals, sel = lax.top_k(imp, n_sel)
        k_sel = ks_blk[bi, gi, sel].reshape(B, G, NSA_Q_BLOCK, n_sel * NSA_SLC_LEN, DH)
        v_sel = vs_blk[bi, gi, sel].reshape(B, G, NSA_Q_BLOCK, n_sel * NSA_SLC_LEN, DH)
        kpos = sel[..., None] * NSA_SLC_LEN + jnp.arange(NSA_SLC_LEN)
        m_s = ((kpos <= tq[None, None, :, None, None]) & (vals > NEG_HALF)[..., None]).reshape(B, G, NSA_Q_BLOCK, -1)
        s_s = jnp.einsum('bqghd,bgqkd->bghqk', qb, k_sel) * scale
        p_s = masked_softmax(s_s, m_s[:, :, None])
        o_s = jnp.einsum('bghqk,bgqkd->bqghd', p_s.astype(v_sel.dtype), v_sel)
        kw = lax.dynamic_slice_in_dim(kw_pad, t0, NSA_Q_BLOCK + NSA_WINDOW, axis=1)
        vw = lax.dynamic_slice_in_dim(vw_pad, t0, NSA_Q_BLOCK + NSA_WINDOW, axis=1)
        wpos = t0 - NSA_WINDOW + jnp.arange(NSA_Q_BLOCK + NSA_WINDOW)
        dist = tq[:, None] - wpos[None, :]
        m_w = (dist >= 0) & (dist < NSA_WINDOW) & (wpos[None, :] >= 0)
        s_w = jnp.einsum('bqghd,bkgd->bghqk', qb, kw) * scale
        p_w = masked_softmax(s_w, m_w)
        o_w = jnp.einsum('bghqk,bkgd->bqghd', p_w.astype(vw.dtype), vw)
        o = o_c * gb[..., 0:1] + o_s * gb[..., 1:2] + o_w * gb[..., 2:3]
        return o.reshape(B, NSA_Q_BLOCK, NSA_HEADS * DH)

    return sweep(block, S // NSA_Q_BLOCK)


def dsa_mixer(p, positions, qk_norm, idx_k_norm):
    B, S, _ = p.shape
    G = DSA_HEADS // DSA_KV_HEADS
    DH = DSA_HEAD_DIM
    cuts = np.cumsum([DSA_HEADS * DH, DSA_KV_HEADS * DH, DSA_KV_HEADS * DH, IDX_HEADS * IDX_DIM, IDX_DIM]).tolist()
    q, k, v, qi, ki, wi = jnp.split(p, cuts, axis=-1)
    cos, sin = rope_angles(positions, DH)
    icos, isin = rope_angles(positions, IDX_DIM)
    q = apply_rope(rms_norm(q.reshape(B, S, DSA_HEADS, DH), qk_norm[0]), cos, sin)
    k = apply_rope(rms_norm(k.reshape(B, S, DSA_KV_HEADS, DH), qk_norm[1]), cos, sin)
    v = v.reshape(B, S, DSA_KV_HEADS, DH)
    qi = apply_rope(qi.reshape(B, S, IDX_HEADS, IDX_DIM), icos, isin)
    ki = apply_rope(rms_norm(ki, idx_k_norm)[:, :, None, :], icos, isin)[:, :, 0]
    wi = wi * (IDX_HEADS ** -0.5 * IDX_DIM ** -0.5)
    topk = min(DSA_TOPK_MAX, S // 4)
    bi = jnp.arange(B)[:, None, None]
    kpos = jnp.arange(S)
    scale = DH ** -0.5

    def block(i):
        t0 = i * Q_BLOCK
        tq = t0 + jnp.arange(Q_BLOCK)
        qib = lax.dynamic_slice_in_dim(qi, t0, Q_BLOCK, axis=1)
        wib = lax.dynamic_slice_in_dim(wi, t0, Q_BLOCK, axis=1)
        rel = jax.nn.relu(jnp.einsum('bqhd,bkd->bqhk', qib, ki))
        score = jnp.einsum('bqhk,bqh->bqk', rel, wib).astype(jnp.float32)
        score = jnp.where(kpos[None, None, :] <= tq[None, :, None], score, NEG)
        vals, sel = lax.top_k(score, topk)
        kg = k[bi, sel]
        vg = v[bi, sel]
        qb = lax.dynamic_slice_in_dim(q, t0, Q_BLOCK, axis=1).reshape(B, Q_BLOCK, DSA_KV_HEADS, G, DH)
        s = jnp.einsum('bqhgd,bqkhd->bhgqk', qb, kg) * scale
        pr = masked_softmax(s, (vals > NEG_HALF)[:, None, None])
        o = jnp.einsum('bhgqk,bqkhd->bqhgd', pr.astype(vg.dtype), vg)
        return o.reshape(B, Q_BLOCK, DSA_HEADS * DH)

    return sweep(block, S // Q_BLOCK)


def swiglu(h, w_gate, w_up, w_down):
    return (jax.nn.silu(h @ w_gate) * (h @ w_up)) @ w_down


def moe_swiglu(h, w_router, w_gate, w_up, w_down):
    B, S, D = h.shape
    t = h.reshape(-1, D)
    logits = (t @ w_router).astype(jnp.float32)
    vals, idx = lax.top_k(logits, TOP_K)
    probs = jax.nn.softmax(vals, axis=-1)
    combine = jnp.sum(jax.nn.one_hot(idx, N_EXPERTS, dtype=jnp.float32) * probs[..., None], axis=1)
    y = jnp.zeros_like(t)
    for e in range(N_EXPERTS):
        y = y + combine[:, e:e + 1].astype(t.dtype) * swiglu(t, w_gate[e], w_up[e], w_down[e])
    return y.reshape(B, S, D)


def setup_inputs(seed: int = 0) -> dict:
    key = jax.random.key(seed)
    keys = jax.random.split(key, 40)
    counter = [0]

    def nk():
        counter[0] += 1
        return keys[counter[0] - 1]

    def w(shape, fan_in, s=1.0):
        return jax.random.normal(nk(), shape, jnp.float32) * (s * fan_in ** -0.5)

    def gain(shape):
        return 1.0 + 0.02 * jax.random.normal(nk(), shape, jnp.float32)

    NE, NO = (DEPTH + 1) // 2, DEPTH // 2
    D = D_MODEL
    x = jax.random.normal(nk(), (BATCH, SEQ, D), jnp.float32)
    c = jax.random.normal(nk(), (BATCH, D), jnp.float32)
    start = jax.random.randint(nk(), (BATCH, 1), 0, MAX_POS_OFFSET, dtype=jnp.int32)
    positions = start + jnp.arange(SEQ, dtype=jnp.int32)[None, :]
    return {
        'x': x,
        'c': c,
        'positions': positions,
        'ada_w': w((D, 6 * D), D, 0.5),
        'ada_b': 0.01 * jax.random.normal(nk(), (6 * D,), jnp.float32),
        'ada_table': 0.1 * jax.random.normal(nk(), (DEPTH, 6, D), jnp.float32),
        'norm_g': gain((DEPTH, 2, D)),
        'ev_w_in': w((NE, D, EVEN_IN), D),
        'ev_w_out': w((NE, EVEN_MIX, D), EVEN_MIX),
        'mla_q_a_norm': gain((NE, MLA_Q_RANK)),
        'mla_kv_a_norm': gain((NE, MLA_KV_RANK)),
        'mla_w_uq': w((NE, MLA_Q_RANK, MLA_HEADS * (MLA_NOPE + MLA_ROPE)), MLA_Q_RANK),
        'mla_w_ukv': w((NE, MLA_KV_RANK, MLA_HEADS * (MLA_NOPE + MLA_V)), MLA_KV_RANK),
        'mla_q_norm': gain((NE, MLA_NOPE + MLA_ROPE)),
        'mla_k_norm': gain((NE, MLA_NOPE + MLA_ROPE)),
        'nsa_qk_norm': gain((NE, 4, NSA_HEAD_DIM)),
        'nsa_cmp_pos': 0.1 * jax.random.normal(nk(), (NE, 2, NSA_CMP_LEN, NSA_HEAD_DIM), jnp.float32),
        'nsa_cmp_w': w((NE, 2, NSA_CMP_LEN, NSA_HEAD_DIM, NSA_HEAD_DIM), NSA_CMP_LEN * NSA_HEAD_DIM),
        'ffn_w_gate': w((NE, D, D_FF), D),
        'ffn_w_up': w((NE, D, D_FF), D),
        'ffn_w_down': w((NE, D_FF, D), D_FF),
        'od_w_in': w((NO, D, ODD_IN), D),
        'od_w_out': w((NO, ODD_MIX, D), ODD_MIX),
        'dsa_qk_norm': gain((NO, 2, DSA_HEAD_DIM)),
        'idx_k_norm': gain((NO, IDX_DIM)),
        'moe_router': w((NO, D, N_EXPERTS), D),
        'moe_w_gate': w((NO, N_EXPERTS, D, D_EXPERT), D),
        'moe_w_up': w((NO, N_EXPERTS, D, D_EXPERT), D),
        'moe_w_down': w((NO, N_EXPERTS, D_EXPERT, D), D_EXPERT),
    }


def reference(x, c, positions, ada_w, ada_b, ada_table, norm_g,
              ev_w_in, ev_w_out, mla_q_a_norm, mla_kv_a_norm, mla_w_uq, mla_w_ukv,
              mla_q_norm, mla_k_norm, nsa_qk_norm, nsa_cmp_pos, nsa_cmp_w,
              ffn_w_gate, ffn_w_up, ffn_w_down,
              od_w_in, od_w_out, dsa_qk_norm, idx_k_norm,
              moe_router, moe_w_gate, moe_w_up, moe_w_down):
    B = x.shape[0]
    cond = (jax.nn.silu(c) @ ada_w + ada_b).reshape(B, 6, D_MODEL)
    for l in range(DEPTH):
        i = l // 2
        mod = cond + ada_table[l]
        sh_a, sc_a, g_a, sh_f, sc_f, g_f = [mod[:, j, None, :] for j in range(6)]
        h = rms_norm(x, norm_g[l, 0]) * (1 + sc_a) + sh_a
        if l % 2 == 0:
            p = h @ ev_w_in[i]
            a_out = mla_mixer(p[..., :MLA_IN], positions, mla_q_a_norm[i], mla_kv_a_norm[i],
                              mla_w_uq[i], mla_w_ukv[i], mla_q_norm[i], mla_k_norm[i])
            b_out = nsa_mixer(p[..., MLA_IN:], positions, nsa_qk_norm[i], nsa_cmp_pos[i], nsa_cmp_w[i])
            mix = jnp.concatenate([a_out, b_out], axis=-1) @ ev_w_out[i]
        else:
            mix = dsa_mixer(h @ od_w_in[i], positions, dsa_qk_norm[i], idx_k_norm[i]) @ od_w_out[i]
        x = x + g_a * mix
        h = rms_norm(x, norm_g[l, 1]) * (1 + sc_f) + sh_f
        if l % 2 == 0:
            ffn = swiglu(h, ffn_w_gate[i], ffn_w_up[i], ffn_w_down[i])
        else:
            ffn = moe_swiglu(h, moe_router[i], moe_w_gate[i], moe_w_up[i], moe_w_down[i])
        x = x + g_f * ffn
    return x
```

```python
import functools

import numpy as np
import jax
import jax.numpy as jnp
from jax import lax
from jax.experimental import pallas as pl
from jax.experimental.pallas import tpu as pltpu

F32 = jnp.float32
I32 = jnp.int32
CDT = jnp.bfloat16

ROPE_THETA = 10000.0
EPS = 1e-6
NEG = -1e30
NEG_HALF = -5e29
FORCE = 1e9
REMOVED = -3e38

MLA_HEADS, MLA_Q_RANK, MLA_KV_RANK, MLA_NOPE, MLA_ROPE, MLA_V = 16, 1536, 512, 128, 64, 128
NSA_HEADS, NSA_KV_GROUPS, NSA_HEAD_DIM = 16, 4, 128
NSA_CMP_LEN, NSA_CMP_STRIDE, NSA_SLC_LEN, NSA_N_SEL, NSA_WINDOW = 32, 16, 64, 16, 512
DSA_HEADS, DSA_KV_HEADS, DSA_HEAD_DIM, IDX_HEADS, IDX_DIM, DSA_TOPK_MAX = 32, 8, 128, 32, 64, 256
N_EXPERTS = 8

LANE = 128
VMEM_LIMIT = 56 * 2**20


def _tile(n, pref, mult=LANE):
    if n <= pref:
        return n
    t = (pref // mult) * mult
    while t >= mult:
        if n % t == 0:
            return t
        t -= mult
    return n


def _params(sem):
    return pltpu.CompilerParams(dimension_semantics=sem, vmem_limit_bytes=VMEM_LIMIT)


def _silu(x):
    return x / (1.0 + jnp.exp(-x))


def _sigmoid(x):
    return 1.0 / (1.0 + jnp.exp(-x))


def _rms(x, g):
    return x * lax.rsqrt(jnp.mean(x * x, axis=-1, keepdims=True) + EPS) * g


def _rope128(y, cosf, sinf):
    return y * cosf + pltpu.roll(y, 64, 1) * sinf


def _rope64pair(y, cos4, sin4, lane):
    rot = jnp.where((lane & 63) < 32, pltpu.roll(y, 96, 1), pltpu.roll(y, 32, 1))
    return y * cos4 + rot * sin4


def _mm_kernel(*refs, nk, mode, rows_per_batch, tm, d_expert, tn):
    k = pl.program_id(2)
    n_in = {"plain": 2, "res": 4, "swiglu": 3, "swiglu_scaled": 4}[mode]
    o_ref = refs[n_in]
    acc, acc2 = (tuple(refs[n_in + 1:]) + (None, None))[:2]
    if mode == "plain":
        a_ref, b_ref = refs[:n_in]
    elif mode == "res":
        a_ref, b_ref, x_ref, g_ref = refs[:n_in]
    elif mode == "swiglu":
        a_ref, b_ref, b2_ref = refs[:n_in]
    else:
        a_ref, b_ref, b2_ref, c_ref = refs[:n_in]

    a = a_ref[...]
    part = jnp.dot(a, b_ref[...].astype(a.dtype), preferred_element_type=F32)
    if mode in ("swiglu", "swiglu_scaled"):
        part2 = jnp.dot(a, b2_ref[...].astype(a.dtype), preferred_element_type=F32)

    if nk > 1:
        @pl.when(k == 0)
        def _():
            acc[...] = part
            if mode in ("swiglu", "swiglu_scaled"):
                acc2[...] = part2

        @pl.when(k > 0)
        def _():
            acc[...] += part
            if mode in ("swiglu", "swiglu_scaled"):
                acc2[...] += part2

    def finish():
        r = acc[...] if nk > 1 else part
        if mode == "plain":
            o_ref[...] = r.astype(o_ref.dtype)
        elif mode == "res":
            o_ref[...] = (x_ref[...] + g_ref[0] * r).astype(o_ref.dtype)
        else:
            r2 = acc2[...] if nk > 1 else part2
            h = _silu(r) * r2
            if mode == "swiglu_scaled":
                e = (pl.program_id(1) * tn) // d_expert
                lane = lax.broadcasted_iota(I32, c_ref.shape, 1)
                h = h * jnp.sum(jnp.where(lane == e, c_ref[...], 0.0), axis=1, keepdims=True)
            o_ref[...] = h.astype(o_ref.dtype)

    if nk > 1:
        pl.when(k == nk - 1)(finish)
    else:
        finish()


def _mm(a, b, *, mode="plain", b2=None, x=None, g=None, comb=None, out_dtype=F32,
        tm=1024, tn=512, tk=4096, rows_per_batch=None, d_expert=1):
    M, K = a.shape
    N = b.shape[1]
    tm, tn, tk = _tile(M, tm, 8), _tile(N, tn), _tile(K, tk)
    nk = K // tk
    grid = (M // tm, N // tn, nk)
    in_specs = [pl.BlockSpec((tm, tk), lambda i, j, k: (i, k)),
                pl.BlockSpec((tk, tn), lambda i, j, k: (k, j))]
    args = [a, b]
    scratch = [pltpu.VMEM((tm, tn), F32)] if nk > 1 else []
    if mode in ("swiglu", "swiglu_scaled"):
        in_specs.append(pl.BlockSpec((tk, tn), lambda i, j, k: (k, j)))
        args.append(b2)
        scratch = scratch * 2
    if mode == "swiglu_scaled":
        in_specs.append(pl.BlockSpec((tm, LANE), lambda i, j, k: (i, 0)))
        args.append(comb)
    if mode == "res":
        rpb = rows_per_batch
        assert rpb % tm == 0
        in_specs += [pl.BlockSpec((tm, tn), lambda i, j, k: (i, j)),
                     pl.BlockSpec((1, 1, tn), lambda i, j, k: ((i * tm) // rpb, 0, j))]
        args += [x, g]
    kern = functools.partial(_mm_kernel, nk=nk, mode=mode, rows_per_batch=rows_per_batch,
                             tm=tm, d_expert=d_expert, tn=tn)
    return pl.pallas_call(
        kern,
        out_shape=jax.ShapeDtypeStruct((M, N), out_dtype),
        grid=grid,
        in_specs=in_specs,
        out_specs=pl.BlockSpec((tm, tn), lambda i, j, k: (i, j)),
        scratch_shapes=scratch,
        compiler_params=_params(("parallel", "parallel", "arbitrary")),
    )(*args)


def _cond_kernel(c_ref, w_ref, b_ref, o_ref):
    a = _silu(c_ref[...]).astype(CDT)
    o_ref[...] = jnp.dot(a, w_ref[...].astype(CDT), preferred_element_type=F32) + b_ref[...]


def _cond(c, ada_w, ada_b):
    B, D = c.shape
    N = ada_w.shape[1]
    cp = jnp.zeros((8, D), F32).at[:B].set(c)
    tn = _tile(N, 512)
    out = pl.pallas_call(
        _cond_kernel,
        out_shape=jax.ShapeDtypeStruct((8, N), F32),
        grid=(N // tn,),
        in_specs=[pl.BlockSpec((8, D), lambda j: (0, 0)),
                  pl.BlockSpec((D, tn), lambda j: (0, j)),
                  pl.BlockSpec((1, tn), lambda j: (0, j))],
        out_specs=pl.BlockSpec((8, tn), lambda j: (0, j)),
        compiler_params=_params(("parallel",)),
    )(cp, ada_w, ada_b.reshape(1, N))
    return out[:B]


def _norm_kernel(*refs, modulate):
    if modulate:
        x_ref, g_ref, sc_ref, sh_ref, o_ref = refs
    else:
        x_ref, g_ref, o_ref = refs
    y = _rms(x_ref[0], g_ref[...])
    if modulate:
        y = y * (1.0 + sc_ref[0]) + sh_ref[0]
    o_ref[0] = y.astype(o_ref.dtype)


def _norm(x, g, sc=None, sh=None, *, width=None, col_block=0, ts=256):
    B, S, W = x.shape
    width = W if width is None else width
    ts = _tile(S, ts, 8)
    modulate = sc is not None
    in_specs = [pl.BlockSpec((1, ts, width), lambda b, i: (b, i, col_block)),
                pl.BlockSpec((1, width), lambda b, i: (0, 0))]
    args = [x, g.reshape(1, width)]
    if modulate:
        in_specs += [pl.BlockSpec((1, 1, width), lambda b, i: (b, 0, 0))] * 2
        args += [sc, sh]
    return pl.pallas_call(
        functools.partial(_norm_kernel, modulate=modulate),
        out_shape=jax.ShapeDtypeStruct((B, S, width), CDT),
        grid=(B, S // ts),
        in_specs=in_specs,
        out_specs=pl.BlockSpec((1, ts, width), lambda b, i: (b, i, 0)),
        compiler_params=_params(("parallel", "parallel")),
    )(*args)


def _rope_angles(pos, dim):
    inv = ROPE_THETA ** (-jnp.arange(0, dim, 2, dtype=F32) / dim)
    ang = pos.astype(F32)[..., None] * inv
    return jnp.cos(ang), jnp.sin(ang)


def _tables128(pos):
    c, s = _rope_angles(pos, 128)
    return jnp.concatenate([c, c], -1), jnp.concatenate([-s, s], -1)


def _tables64pair(pos):
    c, s = _rope_angles(pos, 64)
    return jnp.concatenate([c, c, c, c], -1), jnp.concatenate([-s, s, -s, s], -1)


def _pairs(S, tq, tk, window=None):
    qi, kj, fl = [], [], []
    for i in range(S // tq):
        lo = 0 if window is None else max(0, i * tq - window + 1)
        js = list(range(lo // tk, (i * tq + tq - 1) // tk + 1))
        for n, j in enumerate(js):
            diag = (j + 1) * tk - 1 > i * tq
            f = (1 if n == 0 else 0) | (2 if n == len(js) - 1 else 0)
            f |= 4 if (diag or window is not None) else 0
            qi.append(i), kj.append(j), fl.append(f)
    return (jnp.asarray(qi, I32), jnp.asarray(kj, I32), jnp.asarray(fl, I32))


def _flash_kernel(qi_ref, kj_ref, fl_ref, q_ref, k_ref, v_ref, *rest, G, tq, tk, mode, window):
    if mode == "bias":
        bias_ref, o_ref, m_sc, l_sc, acc_sc = rest
    elif mode == "sel":
        sel_ref, e_ref, o_ref, m_sc, l_sc, acc_sc = rest
    else:
        o_ref, m_sc, l_sc, acc_sc = rest
    p = pl.program_id(2)
    qi, kj, fl = qi_ref[p], kj_ref[p], fl_ref[p]
    dk = q_ref.shape[-1]
    dv = v_ref.shape[-1]

    @pl.when((fl & 1) != 0)
    def _():
        m_sc[...] = jnp.full(m_sc.shape, NEG, F32)
        l_sc[...] = jnp.zeros(l_sc.shape, F32)
        acc_sc[...] = jnp.zeros(acc_sc.shape, F32)

    def compute(position_mask):
        q = q_ref[0].reshape(G * tq, dk)
        s = lax.dot_general(q, k_ref[0, 0], (((1,), (1,)), ((), ())), preferred_element_type=F32)
        s = s.reshape(G, tq, tk)
        mask = None
        if position_mask:
            row = qi * tq + lax.broadcasted_iota(I32, (tq, tk), 0)
            col = kj * tk + lax.broadcasted_iota(I32, (tq, tk), 1)
            mask = col <= row
            if window is not None:
                mask = mask & (col > row - window)
        if mode == "sel":
            hit = jnp.dot(sel_ref[0, 0], e_ref[...], preferred_element_type=F32) > 0.5
            mask = hit if mask is None else (mask & hit)
        if mode == "bias":
            s = s + bias_ref[0, 0].astype(F32)[None]
        if mask is not None:
            s = jnp.where(mask[None], s, NEG)
        m_prev = m_sc[...]
        m_new = jnp.maximum(m_prev, jnp.max(s, axis=-1, keepdims=True))
        alpha = jnp.exp(m_prev - m_new)
        pr = jnp.exp(s - m_new)
        l_sc[...] = alpha * l_sc[...] + jnp.sum(pr, axis=-1, keepdims=True)
        pv = jnp.dot(pr.reshape(G * tq, tk).astype(v_ref.dtype), v_ref[0, 0],
                     preferred_element_type=F32)
        acc_sc[...] = alpha.reshape(G * tq, 1) * acc_sc[...] + pv
        m_sc[...] = m_new

    if mode == "bias":
        compute(False)
    else:
        pl.when((fl & 4) != 0)(functools.partial(compute, True))
        pl.when((fl & 4) == 0)(functools.partial(compute, False))

    @pl.when((fl & 2) != 0)
    def _():
        o = acc_sc[...] / l_sc[...].reshape(G * tq, 1)
        for g in range(G):
            o_ref[0, :, g * dv:(g + 1) * dv] = o[g * tq:(g + 1) * tq].astype(o_ref.dtype)


def _flash(q, k, v, *, mode="causal", window=None, bias=None, sel=None, expand=None,
           tq=512, tk=512, out_dtype=None):
    B, Hq, S, dk = q.shape
    Hkv, dv = k.shape[1], v.shape[-1]
    G = Hq // Hkv
    tq, tk = _tile(S, tq, 8), _tile(S, tk)
    qi, kj, fl = _pairs(S, tq, tk, window)
    P = int(qi.shape[0])
    in_specs = [pl.BlockSpec((1, G, tq, dk), lambda b, h, p, qi, kj, fl: (b, h, qi[p], 0)),
                pl.BlockSpec((1, 1, tk, dk), lambda b, h, p, qi, kj, fl: (b, h, kj[p], 0)),
                pl.BlockSpec((1, 1, tk, dv), lambda b, h, p, qi, kj, fl: (b, h, kj[p], 0))]
    args = [q, k, v]
    if mode == "bias":
        in_specs.append(pl.BlockSpec((1, 1, tq, tk), lambda b, h, p, qi, kj, fl: (b, kj[p], qi[p], 0)))
        args.append(bias)
    elif mode == "sel":
        in_specs += [pl.BlockSpec((1, 1, tq, LANE), lambda b, h, p, qi, kj, fl: (b, h, qi[p], 0)),
                     pl.BlockSpec((LANE, tk), lambda b, h, p, qi, kj, fl: (0, kj[p]))]
        args += [sel, expand]
    kern = functools.partial(_flash_kernel, G=G, tq=tq, tk=tk, mode=mode, window=window)
    return pl.pallas_call(
        kern,
        out_shape=jax.ShapeDtypeStruct((B, S, Hq * dv), CDT if out_dtype is None else out_dtype),
        grid_spec=pltpu.PrefetchScalarGridSpec(
            num_scalar_prefetch=3,
            grid=(B, Hkv, P),
            in_specs=in_specs,
            out_specs=pl.BlockSpec((1, tq, G * dv), lambda b, h, p, qi, kj, fl: (b, qi[p], h)),
            scratch_shapes=[pltpu.VMEM((G, tq, 1), F32), pltpu.VMEM((G, tq, 1), F32),
                            pltpu.VMEM((G * tq, dv), F32)]),
        compiler_params=_params(("parallel", "parallel", "arbitrary")),
    )(qi, kj, fl, *args)


def _mla_qprep_kernel(x_ref, cos_ref, sin_ref, gn_ref, gr_ref, o_ref, *, H, scale):
    ts = x_ref.shape[1]
    lane = lax.broadcasted_iota(I32, (ts, LANE), 1)
    lo = lane < 64
    cos4, sin4 = cos_ref[0], sin_ref[0]
    for h in range(H):
        xn = x_ref[0, :, h * LANE:(h + 1) * LANE]
        o_ref[0, h, :, 0:LANE] = (_rms(xn, gn_ref[...]) * scale).astype(o_ref.dtype)
    for j in range(H // 2):
        xr = x_ref[0, :, (H + j) * LANE:(H + j + 1) * LANE]
        ss = xr * xr
        s_lo = jnp.sum(jnp.where(lo, ss, 0.0), axis=-1, keepdims=True)
        s_hi = jnp.sum(jnp.where(lo, 0.0, ss), axis=-1, keepdims=True)
        inv = jnp.where(lo, lax.rsqrt(s_lo / 64.0 + EPS), lax.rsqrt(s_hi / 64.0 + EPS))
        r = _rope64pair(xr * inv * gr_ref[...], cos4, sin4, lane) * scale
        o_ref[0, 2 * j, :, LANE:2 * LANE] = jnp.where(lo, r, 0.0).astype(o_ref.dtype)
        o_ref[0, 2 * j + 1, :, LANE:2 * LANE] = jnp.where(lo, pltpu.roll(r, 64, 1), 0.0).astype(o_ref.dtype)


def _mla_kvprep_kernel(x_ref, kr_ref, cos_ref, sin_ref, gn_ref, gr_ref, k_ref, v_ref, *, H):
    ts = x_ref.shape[1]
    lane = lax.broadcasted_iota(I32, (ts, LANE), 1)
    kr = kr_ref[0]
    inv = lax.rsqrt(jnp.sum(kr * kr, axis=-1, keepdims=True) / 64.0 + EPS)
    r = _rope64pair(kr * inv * gr_ref[...], cos_ref[0], sin_ref[0], lane).astype(k_ref.dtype)
    for h in range(H):
        xn = x_ref[0, :, h * LANE:(h + 1) * LANE]
        k_ref[0, h, :, 0:LANE] = _rms(xn, gn_ref[...]).astype(k_ref.dtype)
        k_ref[0, h, :, LANE:2 * LANE] = r
        v_ref[0, h] = x_ref[0, :, (H + h) * LANE:(H + h + 1) * LANE].astype(v_ref.dtype)


def _mla(p, kr_block, pos, q_a_norm, kv_a_norm, w_uq, w_ukv, q_norm, k_norm):
    B, S, _ = p.shape
    H = MLA_HEADS
    N = B * S
    cqn = _norm(p, q_a_norm, width=MLA_Q_RANK, col_block=0)
    ckvn = _norm(p, kv_a_norm, width=MLA_KV_RANK, col_block=MLA_Q_RANK // MLA_KV_RANK)
    hh = np.arange(H)[:, None]
    q_perm = np.concatenate([(hh * 192 + np.arange(128)).ravel(), (hh * 192 + 128 + np.arange(64)).ravel()])
    kv_perm = np.concatenate([(hh * 256 + np.arange(128)).ravel(), (hh * 256 + 128 + np.arange(128)).ravel()])
    q_raw = _mm(cqn.reshape(N, -1), w_uq[:, q_perm].astype(CDT), tn=1024, tk=MLA_Q_RANK).reshape(B, S, -1)
    kv_raw = _mm(ckvn.reshape(N, -1), w_ukv[:, kv_perm].astype(CDT), tn=1024, tk=MLA_KV_RANK).reshape(B, S, -1)
    cos4, sin4 = _tables64pair(pos)
    ts = _tile(S, 256, 8)
    scale = (MLA_NOPE + MLA_ROPE) ** -0.5
    gr = q_norm[MLA_NOPE:]
    tab = pl.BlockSpec((1, ts, LANE), lambda b, i: (b, i, 0))
    vec = pl.BlockSpec((1, LANE), lambda b, i: (0, 0))
    q = pl.pallas_call(
        functools.partial(_mla_qprep_kernel, H=H, scale=scale),
        out_shape=jax.ShapeDtypeStruct((B, H, S, 2 * LANE), CDT),
        grid=(B, S // ts),
        in_specs=[pl.BlockSpec((1, ts, H * 192), lambda b, i: (b, i, 0)), tab, tab, vec, vec],
        out_specs=pl.BlockSpec((1, H, ts, 2 * LANE), lambda b, i: (b, 0, i, 0)),
        compiler_params=_params(("parallel", "parallel")),
    )(q_raw, cos4, sin4, q_norm[:MLA_NOPE].reshape(1, -1), jnp.concatenate([gr, gr]).reshape(1, -1))
    gkr = jnp.concatenate([k_norm[MLA_NOPE:], jnp.zeros((64,), F32)])
    k, v = pl.pallas_call(
        functools.partial(_mla_kvprep_kernel, H=H),
        out_shape=(jax.ShapeDtypeStruct((B, H, S, 2 * LANE), CDT),
                   jax.ShapeDtypeStruct((B, H, S, LANE), CDT)),
        grid=(B, S // ts),
        in_specs=[pl.BlockSpec((1, ts, H * 256), lambda b, i: (b, i, 0)),
                  pl.BlockSpec((1, ts, LANE), lambda b, i: (b, i, kr_block)), tab, tab, vec, vec],
        out_specs=(pl.BlockSpec((1, H, ts, 2 * LANE), lambda b, i: (b, 0, i, 0)),
                   pl.BlockSpec((1, H, ts, LANE), lambda b, i: (b, 0, i, 0))),
        compiler_params=_params(("parallel", "parallel")),
    )(kv_raw, p, cos4, sin4, k_norm[:MLA_NOPE].reshape(1, -1), gkr.reshape(1, -1))
    return _flash(q, k, v, mode="causal")


def _nsa_prep_kernel(q_ref, kc_ref, vc_ref, ks_ref, vs_ref, kw_ref, vw_ref, cos_ref, sin_ref, g_ref,
                     qo, kso, vso, kwo, vwo, kco, vco, *, H, G, scale):
    cosf, sinf = cos_ref[0], sin_ref[0]
    for h in range(H):
        y = _rms(q_ref[0, :, h * LANE:(h + 1) * LANE], g_ref[0:1, :])
        qo[0, h] = (_rope128(y, cosf, sinf) * scale).astype(qo.dtype)
    for g in range(G):
        sl = slice(g * LANE, (g + 1) * LANE)
        kso[0, g] = _rope128(_rms(ks_ref[0, :, sl], g_ref[2:3, :]), cosf, sinf).astype(kso.dtype)
        kwo[0, g] = _rope128(_rms(kw_ref[0, :, sl], g_ref[3:4, :]), cosf, sinf).astype(kwo.dtype)
        vso[0, g] = vs_ref[0, :, sl].astype(vso.dtype)
        vwo[0, g] = vw_ref[0, :, sl].astype(vwo.dtype)
        kco[0, g] = kc_ref[0, :, sl].astype(kco.dtype)
        vco[0, g] = vc_ref[0, :, sl].astype(vco.dtype)


def _compress_kernel(xk_ref, xv_ref, wk_ref, wv_ref, pek_ref, pev_ref, g_ref, cos_ref, sin_ref,
                     kc_ref, vc_ref):
    nc = xk_ref.shape[2]

    def comp(x_ref, w_ref, pe_ref):
        y = jnp.dot(x_ref[0, 0], w_ref[...], preferred_element_type=F32)
        c = jnp.dot(pe_ref[...], w_ref[...], preferred_element_type=F32)
        const = c[0:1, 0:LANE] + c[1:2, LANE:2 * LANE]
        return y[:, 0:LANE] + pltpu.roll(y[:, LANE:2 * LANE], nc - 1, 0) + const

    kc = comp(xk_ref, wk_ref, pek_ref)
    kc_ref[0, 0] = _rope128(_rms(kc, g_ref[...]), cos_ref[0], sin_ref[0]).astype(kc_ref.dtype)
    vc_ref[0, 0] = comp(xv_ref, wv_ref, pev_ref).astype(vc_ref.dtype)


def _cmp_attn_kernel(q_ref, kc_ref, vc_ref, m_ref, o_ref, sel_ref, *, G, tq, n_cmp, n_sel, dv):
    i = pl.program_id(2)
    nc = kc_ref.shape[2]
    q = q_ref[0].reshape(G * tq, q_ref.shape[-1])
    s = lax.dot_general(q, kc_ref[0, 0], (((1,), (1,)), ((), ())), preferred_element_type=F32)
    s = s.reshape(G, tq, nc)
    t = i * tq + lax.broadcasted_iota(I32, (tq, nc), 0)
    n = lax.broadcasted_iota(I32, (tq, nc), 1)
    mask = (n * NSA_CMP_STRIDE + (NSA_CMP_LEN - 1) <= t) & (n < n_cmp)
    s = jnp.where(mask[None], s, NEG)
    mx = jnp.max(s, axis=-1, keepdims=True)
    e = jnp.where(mask[None], jnp.exp(s - mx), 0.0)
    l = jnp.sum(e, axis=-1, keepdims=True)
    pc = e / jnp.where(l > 0.0, l, 1.0)
    o = jnp.dot(pc.reshape(G * tq, nc).astype(vc_ref.dtype), vc_ref[0, 0], preferred_element_type=F32)
    for g in range(G):
        o_ref[0, :, g * dv:(g + 1) * dv] = o[g * tq:(g + 1) * tq].astype(o_ref.dtype)

    ps = jnp.sum(pc, axis=0)
    hi = ps.astype(CDT)
    lo_part = (ps - hi.astype(F32)).astype(CDT)
    imp = (jnp.dot(hi, m_ref[...], preferred_element_type=F32)
           + jnp.dot(lo_part, m_ref[...], preferred_element_type=F32))
    blk = lax.broadcasted_iota(I32, (tq, LANE), 1)
    cur = (i * tq + lax.broadcasted_iota(I32, (tq, LANE), 0)) >> (NSA_SLC_LEN.bit_length() - 1)
    forced = (blk == 0) | (blk == cur) | (blk == cur - 1)
    imp = jnp.where(forced, FORCE, jnp.where(blk <= cur, imp, NEG))
    v = imp.T
    rowi = lax.broadcasted_iota(I32, (LANE, tq), 0)

    def take(_, carry):
        v, chosen = carry
        mval = jnp.max(v, axis=0, keepdims=True)
        first = jnp.min(jnp.where(v == mval, rowi, LANE), axis=0, keepdims=True)
        hit = rowi == first
        chosen = jnp.where(hit & (mval > NEG_HALF), 1.0, chosen)
        return jnp.where(hit, REMOVED, v), chosen

    _, chosen = lax.fori_loop(0, n_sel, take, (v, jnp.zeros((LANE, tq), F32)))
    sel_ref[0, 0] = chosen.T.astype(sel_ref.dtype)


def _nsa_combine_kernel(oc_ref, os_ref, ow_ref, g_ref, o_ref, *, H, dv):
    gate = _sigmoid(g_ref[0])
    for h in range(H):
        sl = slice(h * dv, (h + 1) * dv)
        o = (oc_ref[0, :, sl] * gate[:, h:h + 1] + os_ref[0, :, sl] * gate[:, H + h:H + h + 1]
             + ow_ref[0, :, sl] * gate[:, 2 * H + h:2 * H + h + 1])
        o_ref[0, :, sl] = o.astype(o_ref.dtype)


def _nsa(p, q_block, kv_block0, gate_block, pos, qk_norm, cmp_pos, cmp_w):
    B, S, _ = p.shape
    H, G, DH = NSA_HEADS, NSA_KV_GROUPS, NSA_HEAD_DIM
    HPG = H // G
    scale = DH ** -0.5
    ts = _tile(S, 256, 8)
    cosf, sinf = _tables128(pos)
    tab = pl.BlockSpec((1, ts, LANE), lambda b, i: (b, i, 0))
    kvspec = [pl.BlockSpec((1, ts, G * DH), functools.partial(lambda b, i, m: (b, i, kv_block0 + m), m=m))
              for m in range(6)]
    head_out = lambda n: pl.BlockSpec((1, n, ts, DH), lambda b, i: (b, 0, i, 0))
    kv_shape = jax.ShapeDtypeStruct((B, G, S, DH), CDT)
    q, ks, vs, kw, vw, kcr, vcr = pl.pallas_call(
        functools.partial(_nsa_prep_kernel, H=H, G=G, scale=scale),
        out_shape=(jax.ShapeDtypeStruct((B, H, S, DH), CDT),) + (kv_shape,) * 6,
        grid=(B, S // ts),
        in_specs=[pl.BlockSpec((1, ts, H * DH), lambda b, i: (b, i, q_block))] + kvspec
                 + [tab, tab, pl.BlockSpec((4, DH), lambda b, i: (0, 0))],
        out_specs=(head_out(H),) + (head_out(G),) * 6,
        compiler_params=_params(("parallel", "parallel")),
    )(p, p, p, p, p, p, p, cosf, sinf, qk_norm)

    half = NSA_CMP_LEN // 2
    nc = S // NSA_CMP_STRIDE
    n_cmp = (S - NSA_CMP_LEN) // NSA_CMP_STRIDE + 1
    cmp_end = jnp.minimum(jnp.arange(nc) * NSA_CMP_STRIDE + NSA_CMP_LEN - 1, S - 1)
    ccos, csin = _tables128(pos[:, cmp_end])
    wcat = lambda w: jnp.concatenate([w[:half].reshape(half * DH, DH), w[half:].reshape(half * DH, DH)], 1).astype(CDT)
    pecat = lambda pe: jnp.zeros((8, half * DH), F32).at[0].set(pe[:half].reshape(-1)).at[1].set(
        pe[half:].reshape(-1)).astype(CDT)
    xspec = pl.BlockSpec((1, 1, nc, half * DH), lambda b, g: (b, g, 0, 0))
    wspec = pl.BlockSpec((half * DH, 2 * DH), lambda b, g: (0, 0))
    pespec = pl.BlockSpec((8, half * DH), lambda b, g: (0, 0))
    cspec = pl.BlockSpec((1, 1, nc, DH), lambda b, g: (b, g, 0, 0))
    ctab = pl.BlockSpec((1, nc, DH), lambda b, g: (b, 0, 0))
    kc, vc = pl.pallas_call(
        _compress_kernel,
        out_shape=(jax.ShapeDtypeStruct((B, G, nc, DH), CDT),) * 2,
        grid=(B, G),
        in_specs=[xspec, xspec, wspec, wspec, pespec, pespec,
                  pl.BlockSpec((1, DH), lambda b, g: (0, 0)), ctab, ctab],
        out_specs=(cspec, cspec),
        compiler_params=_params(("parallel", "parallel")),
    )(kcr.reshape(B, G, nc, half * DH), vcr.reshape(B, G, nc, half * DH), wcat(cmp_w[0]), wcat(cmp_w[1]),
      pecat(cmp_pos[0]), pecat(cmp_pos[1]), qk_norm[1].reshape(1, DH), ccos, csin)

    n_slc = S // NSA_SLC_LEN
    assert n_slc <= LANE
    r, cl = NSA_SLC_LEN // NSA_CMP_STRIDE, NSA_CMP_LEN // NSA_CMP_STRIDE
    m_np = np.zeros((nc, LANE), np.float32)
    for j in range(n_slc):
        for a in range(r):
            for c in range(cl):
                ci = j * r + a + c - (cl - 1)
                if 0 <= ci < n_cmp:
                    m_np[ci, j] += 1.0
    tq = _tile(S, 256, 8)
    o_c, sel = pl.pallas_call(
        functools.partial(_cmp_attn_kernel, G=HPG, tq=tq, n_cmp=n_cmp, n_sel=min(NSA_N_SEL, n_slc), dv=DH),
        out_shape=(jax.ShapeDtypeStruct((B, S, H * DH), F32),
                   jax.ShapeDtypeStruct((B, G, S, LANE), CDT)),
        grid=(B, G, S // tq),
        in_specs=[pl.BlockSpec((1, HPG, tq, DH), lambda b, g, i: (b, g, i, 0)),
                  pl.BlockSpec((1, 1, nc, DH), lambda b, g, i: (b, g, 0, 0)),
                  pl.BlockSpec((1, 1, nc, DH), lambda b, g, i: (b, g, 0, 0)),
                  pl.BlockSpec((nc, LANE), lambda b, g, i: (0, 0))],
        out_specs=(pl.BlockSpec((1, tq, HPG * DH), lambda b, g, i: (b, i, g)),
                   pl.BlockSpec((1, 1, tq, LANE), lambda b, g, i: (b, g, i, 0))),
        compiler_params=_params(("parallel", "parallel", "parallel")),
    )(q, kc, vc, jnp.asarray(m_np, CDT))

    expand = jnp.asarray((np.arange(S)[None, :] // NSA_SLC_LEN) == np.arange(LANE)[:, None], CDT)
    o_s = _flash(q, ks, vs, mode="sel", sel=sel, expand=expand, tq=256, out_dtype=F32)
    o_w = _flash(q, kw, vw, mode="window", window=NSA_WINDOW, tq=256, out_dtype=F32)
    ospec = pl.BlockSpec((1, ts, H * DH), lambda b, i: (b, i, 0))
    return pl.pallas_call(
        functools.partial(_nsa_combine_kernel, H=H, dv=DH),
        out_shape=jax.ShapeDtypeStruct((B, S, H * DH), CDT),
        grid=(B, S // ts),
        in_specs=[ospec, ospec, ospec, pl.BlockSpec((1, ts, LANE), lambda b, i: (b, i, gate_block))],
        out_specs=ospec,
        compiler_params=_params(("parallel", "parallel")),
    )(o_c, o_s, o_w, p)


def _dsa_prep_kernel(q_ref, k_ref, v_ref, qi_ref, ki_ref, wi_ref, cos_ref, sin_ref, cos4_ref, sin4_ref,
                     g_ref, gi_ref, qo, ko, vo, qio, kilo, kihi, wio, *, H, HKV, scale, wscale):
    ts = q_ref.shape[1]
    lane = lax.broadcasted_iota(I32, (ts, LANE), 1)
    cosf, sinf, cos4, sin4 = cos_ref[0], sin_ref[0], cos4_ref[0], sin4_ref[0]
    for h in range(H):
        y = _rms(q_ref[0, :, h * LANE:(h + 1) * LANE], g_ref[0:1, :])
        qo[0, h] = (_rope128(y, cosf, sinf) * scale).astype(qo.dtype)
    for h in range(HKV):
        sl = slice(h * LANE, (h + 1) * LANE)
        ko[0, h] = _rope128(_rms(k_ref[0, :, sl], g_ref[1:2, :]), cosf, sinf).astype(ko.dtype)
        vo[0, h] = v_ref[0, :, sl].astype(vo.dtype)
    for j in range(qi_ref.shape[2] // LANE):
        qio[0, j] = _rope64pair(qi_ref[0, :, j * LANE:(j + 1) * LANE], cos4, sin4, lane).astype(qio.dtype)
    ki = ki_ref[0]
    inv = lax.rsqrt(jnp.sum(ki * ki, axis=-1, keepdims=True) / 64.0 + EPS)
    r = _rope64pair(ki * inv * gi_ref[...], cos4, sin4, lane)
    kilo[0] = r.astype(kilo.dtype)
    kihi[0] = pltpu.roll(r, 64, 1).astype(kihi.dtype)
    wio[0] = wi_ref[0] * wscale


def _indexer_kernel(qi_ref, kj_ref, fl_ref, q_ref, klo_ref, khi_ref, w_ref, o_ref, wb_sc, sc_sc,
                    *, tq, tk, topk, n_pairs, rg):
    p = pl.program_id(1)
    qi, kj, fl = qi_ref[p], kj_ref[p], fl_ref[p]
    n_tiles = o_ref.shape[1]

    @pl.when((fl & 1) != 0)
    def _():
        w = w_ref[0]
        for h in range(2 * n_pairs):
            wb_sc[h] = jnp.broadcast_to(w[:, h:h + 1], (tq, LANE))

    def pair(j, acc):
        q = q_ref[0, j]
        sa = lax.dot_general(q, klo_ref[0], (((1,), (1,)), ((), ())), preferred_element_type=F32)
        sb = lax.dot_general(q, khi_ref[0], (((1,), (1,)), ((), ())), preferred_element_type=F32)
        wa = jnp.tile(wb_sc[2 * j], (1, tk // LANE))
        wb = jnp.tile(wb_sc[2 * j + 1], (1, tk // LANE))
        return acc + wa * jnp.maximum(sa, 0.0) + wb * jnp.maximum(sb, 0.0)

    score = lax.fori_loop(0, n_pairs, pair, jnp.zeros((tq, tk), F32))
    row = qi * tq + lax.broadcasted_iota(I32, (tq, tk), 0)
    col = kj * tk + lax.broadcasted_iota(I32, (tq, tk), 1)
    score = jnp.where(col <= row, score, NEG)
    bits = pltpu.bitcast(score, I32)
    key = bits ^ ((bits >> 31) & 0x7FFFFFFF)
    sc_sc[kj] = key

    @pl.when((fl & 2) != 0)
    def _():
        n_chunks = kj + 1
        nh_bits = int(np.float32(NEG_HALF).view(np.int32))
        key_neg_half = nh_bits ^ 0x7FFFFFFF if nh_bits < 0 else nh_bits
        for g in range(tq // rg):
            rows = pl.ds(g * rg, rg)

            def bit_step(b, thr):
                cand = thr + jnp.left_shift(jnp.int32(1), 31 - b)

                def count(c, cnt):
                    blk = sc_sc[c, rows, :]
                    for u in range(tk // LANE):
                        cnt = cnt + (blk[:, u * LANE:(u + 1) * LANE] >= cand).astype(I32)
                    return cnt

                cnt = lax.fori_loop(0, n_chunks, count, jnp.zeros((rg, LANE), I32))
                tot = jnp.sum(cnt, axis=1, keepdims=True)
                return jnp.where(tot >= topk, cand, thr)

            thr = lax.fori_loop(0, 32, bit_step, jnp.full((rg, LANE), -2**31, I32))
            thr = jnp.maximum(thr, key_neg_half + 1)
            thr_t = jnp.tile(thr, (1, tk // LANE))

            def emit(c, _):
                o_ref[0, c, rows, :] = jnp.where(sc_sc[c, rows, :] >= thr_t, 0.0, NEG).astype(o_ref.dtype)
                return 0

            def emit_masked(c, _):
                o_ref[0, c, rows, :] = jnp.full((rg, tk), NEG, o_ref.dtype)
                return 0

            lax.fori_loop(0, n_chunks, emit, 0)
            lax.fori_loop(n_chunks, n_tiles, emit_masked, 0)


def _dsa(p, pos, qk_norm, idx_k_norm):
    B, S, _ = p.shape
    H, HKV, DH = DSA_HEADS, DSA_KV_HEADS, DSA_HEAD_DIM
    NP = IDX_HEADS // 2
    ts = _tile(S, 256, 8)
    cosf, sinf = _tables128(pos)
    cos4, sin4 = _tables64pair(pos)
    tab = pl.BlockSpec((1, ts, LANE), lambda b, i: (b, i, 0))
    kw = HKV * DH
    gi = jnp.concatenate([idx_k_norm, jnp.zeros((LANE - IDX_DIM,), F32)]).reshape(1, LANE)
    head_out = lambda n: pl.BlockSpec((1, n, ts, DH), lambda b, i: (b, 0, i, 0))
    q, k, v, qidx, kilo, kihi, wi = pl.pallas_call(
        functools.partial(_dsa_prep_kernel, H=H, HKV=HKV, scale=DH ** -0.5,
                          wscale=IDX_HEADS ** -0.5 * IDX_DIM ** -0.5),
        out_shape=(jax.ShapeDtypeStruct((B, H, S, DH), CDT), jax.ShapeDtypeStruct((B, HKV, S, DH), CDT),
                   jax.ShapeDtypeStruct((B, HKV, S, DH), CDT), jax.ShapeDtypeStruct((B, NP, S, LANE), CDT),
                   jax.ShapeDtypeStruct((B, S, LANE), CDT), jax.ShapeDtypeStruct((B, S, LANE), CDT),
                   jax.ShapeDtypeStruct((B, S, LANE), F32)),
        grid=(B, S // ts),
        in_specs=[pl.BlockSpec((1, ts, H * DH), lambda b, i: (b, i, 0)),
                  pl.BlockSpec((1, ts, kw), lambda b, i: (b, i, H * DH // kw)),
                  pl.BlockSpec((1, ts, kw), lambda b, i: (b, i, H * DH // kw + 1)),
                  pl.BlockSpec((1, ts, NP * LANE), lambda b, i: (b, i, (H * DH + 2 * kw) // (NP * LANE))),
                  pl.BlockSpec((1, ts, LANE), lambda b, i: (b, i, (H * DH + 2 * kw + NP * LANE) // LANE)),
                  pl.BlockSpec((1, ts, LANE), lambda b, i: (b, i, (H * DH + 2 * kw + NP * LANE) // LANE + 1)),
                  tab, tab, tab, tab,
                  pl.BlockSpec((2, DH), lambda b, i: (0, 0)), pl.BlockSpec((1, LANE), lambda b, i: (0, 0))],
        out_specs=(head_out(H), head_out(HKV), head_out(HKV), head_out(NP), tab, tab, tab),
        compiler_params=_params(("parallel", "parallel")),
    )(p, p, p, p, p, p, cosf, sinf, cos4, sin4, qk_norm, gi)

    topk = min(DSA_TOPK_MAX, S // 4)
    tq, tk = _tile(S, 256, 8), _tile(S, 512)
    qi_t, kj_t, fl_t = _pairs(S, tq, tk)
    bias = pl.pallas_call(
        functools.partial(_indexer_kernel, tq=tq, tk=tk, topk=topk, n_pairs=NP, rg=min(64, tq)),
        out_shape=jax.ShapeDtypeStruct((B, S // tk, S, tk), CDT),
        grid_spec=pltpu.PrefetchScalarGridSpec(
            num_scalar_prefetch=3,
            grid=(B, int(qi_t.shape[0])),
            in_specs=[pl.BlockSpec((1, NP, tq, LANE), lambda b, p, qi, kj, fl: (b, 0, qi[p], 0)),
                      pl.BlockSpec((1, tk, LANE), lambda b, p, qi, kj, fl: (b, kj[p], 0)),
                      pl.BlockSpec((1, tk, LANE), lambda b, p, qi, kj, fl: (b, kj[p], 0)),
                      pl.BlockSpec((1, tq, LANE), lambda b, p, qi, kj, fl: (b, qi[p], 0))],
            out_specs=pl.BlockSpec((1, S // tk, tq, tk), lambda b, p, qi, kj, fl: (b, 0, qi[p], 0)),
            scratch_shapes=[pltpu.VMEM((2 * NP, tq, LANE), F32), pltpu.VMEM((S // tk, tq, tk), I32)]),
        compiler_params=_params(("parallel", "arbitrary")),
    )(qi_t, kj_t, fl_t, qidx, kilo, kihi, wi)
    return _flash(q, k, v, mode="bias", bias=bias, tq=tq, tk=tk)


def _router_kernel(l_ref, o_ref, *, n_experts):
    x = l_ref[...]
    lane = lax.broadcasted_iota(I32, x.shape, 1)
    x = jnp.where(lane < n_experts, x, -jnp.inf)
    m1 = jnp.max(x, axis=1, keepdims=True)
    i1 = jnp.min(jnp.where(x == m1, lane, LANE), axis=1, keepdims=True)
    x2 = jnp.where(lane == i1, -jnp.inf, x)
    m2 = jnp.max(x2, axis=1, keepdims=True)
    i2 = jnp.min(jnp.where(x2 == m2, lane, LANE), axis=1, keepdims=True)
    e2 = jnp.exp(m2 - m1)
    p1 = 1.0 / (1.0 + e2)
    p2 = e2 / (1.0 + e2)
    o_ref[...] = jnp.where(lane == i1, p1, 0.0) + jnp.where(lane == i2, p2, 0.0)


def _moe(h, x, g_f, w_router, w_gate, w_up, w_down, S):
    N, D = h.shape
    E, _, DE = w_gate.shape
    wr = jnp.zeros((D, LANE), F32).at[:, :E].set(w_router).astype(CDT)
    logits = _mm(h, wr, tn=LANE, tk=D)
    tm = _tile(N, 1024, 8)
    comb = pl.pallas_call(
        functools.partial(_router_kernel, n_experts=E),
        out_shape=jax.ShapeDtypeStruct((N, LANE), F32),
        grid=(N // tm,),
        in_specs=[pl.BlockSpec((tm, LANE), lambda i: (i, 0))],
        out_specs=pl.BlockSpec((tm, LANE), lambda i: (i, 0)),
        compiler_params=_params(("parallel",)),
    )(logits)
    wg = jnp.transpose(w_gate, (1, 0, 2)).reshape(D, E * DE).astype(CDT)
    wu = jnp.transpose(w_up, (1, 0, 2)).reshape(D, E * DE).astype(CDT)
    hid = _mm(h, wg, mode="swiglu_scaled", b2=wu, comb=comb, out_dtype=CDT, tn=DE, tk=1024, d_expert=DE)
    return _mm(hid, w_down.reshape(E * DE, D).astype(CDT), mode="res", x=x, g=g_f,
               tn=1024, tk=1024, rows_per_batch=S)


def _pad_cols(blocks, total):
    cols = []
    for w, width in blocks:
        cols.append(w)
        if width > w.shape[1]:
            cols.append(jnp.zeros((w.shape[0], width - w.shape[1]), w.dtype))
    out = jnp.concatenate(cols, axis=1)
    if total > out.shape[1]:
        out = jnp.concatenate([out, jnp.zeros((out.shape[0], total - out.shape[1]), out.dtype)], axis=1)
    return out.astype(CDT)


def _round_up(n, m):
    return (n + m - 1) // m * m


def kernel(x, c, positions, ada_w, ada_b, ada_table, norm_g, ev_w_in, ev_w_out, mla_q_a_norm, mla_kv_a_norm, mla_w_uq, mla_w_ukv, mla_q_norm, mla_k_norm, nsa_qk_norm, nsa_cmp_pos, nsa_cmp_w, ffn_w_gate, ffn_w_up, ffn_w_down, od_w_in, od_w_out, dsa_qk_norm, idx_k_norm, moe_router, moe_w_gate, moe_w_up, moe_w_down):
    B, S, D = x.shape
    N = B * S
    depth = ada_table.shape[0]
    cond = _cond(c, ada_w, ada_b).reshape(B, 6, D)

    mla_in = MLA_Q_RANK + MLA_KV_RANK + MLA_ROPE
    nq = NSA_HEADS * NSA_HEAD_DIM
    nkv = 6 * NSA_KV_GROUPS * NSA_HEAD_DIM
    hn = np.arange(NSA_HEADS)
    gate_perm = np.concatenate([hn * 3 + r for r in range(3)])

    x2 = x.reshape(N, D)
    for l in range(depth):
        i = l // 2
        mod = cond + ada_table[l]
        sh_a, sc_a, g_a, sh_f, sc_f, g_f = [mod[:, j, None, :] for j in range(6)]
        h = _norm(x2.reshape(B, S, D), norm_g[l, 0], sc_a, sh_a).reshape(N, D)
        if l % 2 == 0:
            w = ev_w_in[i]
            nsa = w[:, mla_in:]
            blocks = [(w[:, :MLA_Q_RANK + MLA_KV_RANK], MLA_Q_RANK + MLA_KV_RANK),
                      (nsa[:, :nq + nkv], nq + nkv),
                      (w[:, MLA_Q_RANK + MLA_KV_RANK:mla_in], LANE),
                      (nsa[:, nq + nkv:][:, gate_perm], LANE)]
            width = MLA_Q_RANK + MLA_KV_RANK + nq + nkv + 2 * LANE
            w_in = _pad_cols(blocks, _round_up(width, 512))
            p = _mm(h, w_in).reshape(B, S, -1)
            off = MLA_Q_RANK + MLA_KV_RANK
            a_out = _mla(p, (off + nq + nkv) // LANE, positions, mla_q_a_norm[i], mla_kv_a_norm[i],
                         mla_w_uq[i], mla_w_ukv[i], mla_q_norm[i], mla_k_norm[i])
            b_out = _nsa(p, off // nq, (off + nq) // (NSA_KV_GROUPS * NSA_HEAD_DIM),
                         (off + nq + nkv) // LANE + 1, positions, nsa_qk_norm[i], nsa_cmp_pos[i], nsa_cmp_w[i])
            mix = jnp.concatenate([a_out, b_out], axis=-1).reshape(N, -1)
            w_out = ev_w_out[i]
        else:
            w = od_w_in[i]
            main = DSA_HEADS * DSA_HEAD_DIM + 2 * DSA_KV_HEADS * DSA_HEAD_DIM + IDX_HEADS * IDX_DIM
            blocks = [(w[:, :main], main), (w[:, main:main + IDX_DIM], LANE), (w[:, main + IDX_DIM:], LANE)]
            w_in = _pad_cols(blocks, _round_up(main + 2 * LANE, 512))
            p = _mm(h, w_in).reshape(B, S, -1)
            mix = _dsa(p, positions, dsa_qk_norm[i], idx_k_norm[i]).reshape(N, -1)
            w_out = od_w_out[i]
        x2 = _mm(mix, w_out.astype(CDT), mode="res", x=x2, g=g_a, rows_per_batch=S)
        h = _norm(x2.reshape(B, S, D), norm_g[l, 1], sc_f, sh_f).reshape(N, D)
        if l % 2 == 0:
            hid = _mm(h, ffn_w_gate[i].astype(CDT), mode="swiglu", b2=ffn_w_up[i].astype(CDT), out_dtype=CDT)
            x2 = _mm(hid, ffn_w_down[i].astype(CDT), mode="res", x=x2, g=g_f, tn=1024, tk=2048,
                     rows_per_batch=S)
        else:
            x2 = _moe(h, x2, g_f, moe_router[i], moe_w_gate[i], moe_w_up[i], moe_w_down[i], S)
    return x2.reshape(B, S, D)
```

```python
import functools

import numpy as np
import jax
import jax.numpy as jnp
from jax import lax
from jax.experimental import pallas as pl
from jax.experimental.pallas import tpu as pltpu

F32 = jnp.float32
I32 = jnp.int32
CDT = jnp.bfloat16

ROPE_THETA = 10000.0
EPS = 1e-6
NEG = -1e30
NEG_HALF = -5e29
FORCE = 1e9
REMOVED = -3e38
LOG2E = 1.4426950408889634

MLA_HEADS, MLA_Q_RANK, MLA_KV_RANK, MLA_NOPE, MLA_ROPE, MLA_V = 16, 1536, 512, 128, 64, 128
NSA_HEADS, NSA_KV_GROUPS, NSA_HEAD_DIM = 16, 4, 128
NSA_CMP_LEN, NSA_CMP_STRIDE, NSA_SLC_LEN, NSA_N_SEL, NSA_WINDOW = 32, 16, 64, 16, 512
DSA_HEADS, DSA_KV_HEADS, DSA_HEAD_DIM, IDX_HEADS, IDX_DIM, DSA_TOPK_MAX = 32, 8, 128, 32, 64, 256
N_EXPERTS = 8

LANE = 128
VMEM_LIMIT = 56 * 2**20


def _tile(n, pref, mult=LANE):
    if n <= pref:
        return n
    t = (pref // mult) * mult
    while t >= mult:
        if n % t == 0:
            return t
        t -= mult
    return n


def _params(sem):
    return pltpu.CompilerParams(dimension_semantics=sem, vmem_limit_bytes=VMEM_LIMIT)


def _silu(x):
    return x / (1.0 + jnp.exp(-x))


def _sigmoid(x):
    return 1.0 / (1.0 + jnp.exp(-x))


def _rms(x, g):
    return x * lax.rsqrt(jnp.mean(x * x, axis=-1, keepdims=True) + EPS) * g


def _rope128(y, cosf, sinf):
    return y * cosf + pltpu.roll(y, 64, 1) * sinf


def _rope64pair(y, cos4, sin4, lane):
    rot = jnp.where((lane & 63) < 32, pltpu.roll(y, 96, 1), pltpu.roll(y, 32, 1))
    return y * cos4 + rot * sin4


def _mm_kernel(*refs, nk, mode, rows_per_batch, tm, d_expert, tn):
    k = pl.program_id(2)
    n_in = {"plain": 2, "res": 4, "swiglu": 3, "swiglu_scaled": 4}[mode]
    o_ref = refs[n_in]
    acc, acc2 = (tuple(refs[n_in + 1:]) + (None, None))[:2]
    if mode == "plain":
        a_ref, b_ref = refs[:n_in]
    elif mode == "res":
        a_ref, b_ref, x_ref, g_ref = refs[:n_in]
    elif mode == "swiglu":
        a_ref, b_ref, b2_ref = refs[:n_in]
    else:
        a_ref, b_ref, b2_ref, c_ref = refs[:n_in]

    a = a_ref[...]
    part = jnp.dot(a, b_ref[...].astype(a.dtype), preferred_element_type=F32)
    if mode in ("swiglu", "swiglu_scaled"):
        part2 = jnp.dot(a, b2_ref[...].astype(a.dtype), preferred_element_type=F32)

    if nk > 1:
        @pl.when(k == 0)
        def _():
            acc[...] = part
            if mode in ("swiglu", "swiglu_scaled"):
                acc2[...] = part2

        @pl.when(k > 0)
        def _():
            acc[...] += part
            if mode in ("swiglu", "swiglu_scaled"):
                acc2[...] += part2

    def finish():
        r = acc[...] if nk > 1 else part
        if mode == "plain":
            o_ref[...] = r.astype(o_ref.dtype)
        elif mode == "res":
            o_ref[...] = (x_ref[...] + g_ref[0] * r).astype(o_ref.dtype)
        else:
            r2 = acc2[...] if nk > 1 else part2
            h = _silu(r) * r2
            if mode == "swiglu_scaled":
                e = (pl.program_id(1) * tn) // d_expert
                lane = lax.broadcasted_iota(I32, c_ref.shape, 1)
                h = h * jnp.sum(jnp.where(lane == e, c_ref[...], 0.0), axis=1, keepdims=True)
            o_ref[...] = h.astype(o_ref.dtype)

    if nk > 1:
        pl.when(k == nk - 1)(finish)
    else:
        finish()


def _mm(a, b, *, mode="plain", b2=None, x=None, g=None, comb=None, out_dtype=F32,
        tm=1024, tn=512, tk=4096, rows_per_batch=None, d_expert=1):
    M, K = a.shape
    N = b.shape[1]
    tm, tn, tk = _tile(M, tm, 8), _tile(N, tn), _tile(K, tk)
    nk = K // tk
    grid = (M // tm, N // tn, nk)
    in_specs = [pl.BlockSpec((tm, tk), lambda i, j, k: (i, k)),
                pl.BlockSpec((tk, tn), lambda i, j, k: (k, j))]
    args = [a, b]
    scratch = [pltpu.VMEM((tm, tn), F32)] if nk > 1 else []
    if mode in ("swiglu", "swiglu_scaled"):
        in_specs.append(pl.BlockSpec((tk, tn), lambda i, j, k: (k, j)))
        args.append(b2)
        scratch = scratch * 2
    if mode == "swiglu_scaled":
        in_specs.append(pl.BlockSpec((tm, LANE), lambda i, j, k: (i, 0)))
        args.append(comb)
    if mode == "res":
        rpb = rows_per_batch
        assert rpb % tm == 0
        in_specs += [pl.BlockSpec((tm, tn), lambda i, j, k: (i, j)),
                     pl.BlockSpec((1, 1, tn), lambda i, j, k: ((i * tm) // rpb, 0, j))]
        args += [x, g]
    kern = functools.partial(_mm_kernel, nk=nk, mode=mode, rows_per_batch=rows_per_batch,
                             tm=tm, d_expert=d_expert, tn=tn)
    return pl.pallas_call(
        kern,
        out_shape=jax.ShapeDtypeStruct((M, N), out_dtype),
        grid=grid,
        in_specs=in_specs,
        out_specs=pl.BlockSpec((tm, tn), lambda i, j, k: (i, j)),
        scratch_shapes=scratch,
        compiler_params=_params(("parallel", "parallel", "arbitrary")),
    )(*args)


def _cond_kernel(c_ref, w_ref, b_ref, o_ref):
    a = _silu(c_ref[...]).astype(CDT)
    o_ref[...] = jnp.dot(a, w_ref[...].astype(CDT), preferred_element_type=F32) + b_ref[...]


def _cond(c, ada_w, ada_b):
    B, D = c.shape
    N = ada_w.shape[1]
    cp = jnp.zeros((8, D), F32).at[:B].set(c)
    tn = _tile(N, 512)
    out = pl.pallas_call(
        _cond_kernel,
        out_shape=jax.ShapeDtypeStruct((8, N), F32),
        grid=(N // tn,),
        in_specs=[pl.BlockSpec((8, D), lambda j: (0, 0)),
                  pl.BlockSpec((D, tn), lambda j: (0, j)),
                  pl.BlockSpec((1, tn), lambda j: (0, j))],
        out_specs=pl.BlockSpec((8, tn), lambda j: (0, j)),
        compiler_params=_params(("parallel",)),
    )(cp, ada_w, ada_b.reshape(1, N))
    return out[:B]


def _norm_kernel(*refs, modulate):
    if modulate:
        x_ref, g_ref, sc_ref, sh_ref, o_ref = refs
    else:
        x_ref, g_ref, o_ref = refs
    y = _rms(x_ref[0], g_ref[...])
    if modulate:
        y = y * (1.0 + sc_ref[0]) + sh_ref[0]
    o_ref[0] = y.astype(o_ref.dtype)


def _norm(x, g, sc=None, sh=None, *, width=None, col_block=0, ts=256):
    B, S, W = x.shape
    width = W if width is None else width
    ts = _tile(S, ts, 8)
    modulate = sc is not None
    in_specs = [pl.BlockSpec((1, ts, width), lambda b, i: (b, i, col_block)),
                pl.BlockSpec((1, width), lambda b, i: (0, 0))]
    args = [x, g.reshape(1, width)]
    if modulate:
        in_specs += [pl.BlockSpec((1, 1, width), lambda b, i: (b, 0, 0))] * 2
        args += [sc, sh]
    return pl.pallas_call(
        functools.partial(_norm_kernel, modulate=modulate),
        out_shape=jax.ShapeDtypeStruct((B, S, width), CDT),
        grid=(B, S // ts),
        in_specs=in_specs,
        out_specs=pl.BlockSpec((1, ts, width), lambda b, i: (b, i, 0)),
        compiler_params=_params(("parallel", "parallel")),
    )(*args)


def _rope_angles(pos, dim):
    inv = ROPE_THETA ** (-jnp.arange(0, dim, 2, dtype=F32) / dim)
    ang = pos.astype(F32)[..., None] * inv
    return jnp.cos(ang), jnp.sin(ang)


def _tables128(pos):
    c, s = _rope_angles(pos, 128)
    return jnp.concatenate([c, c], -1), jnp.concatenate([-s, s], -1)


def _tables64pair(pos):
    c, s = _rope_angles(pos, 64)
    return jnp.concatenate([c, c, c, c], -1), jnp.concatenate([-s, s, -s, s], -1)


def _pairs(S, tq, tk, window=None):
    qi, kj, fl = [], [], []
    for i in range(S // tq):
        lo = 0 if window is None else max(0, i * tq - window + 1)
        js = list(range(lo // tk, (i * tq + tq - 1) // tk + 1))
        for n, j in enumerate(js):
            diag = (j + 1) * tk - 1 > i * tq
            f = (1 if n == 0 else 0) | (2 if n == len(js) - 1 else 0)
            f |= 4 if (diag or window is not None) else 0
            qi.append(i), kj.append(j), fl.append(f)
    return (jnp.asarray(qi, I32), jnp.asarray(kj, I32), jnp.asarray(fl, I32))


def _flash_kernel(qi_ref, kj_ref, fl_ref, q_ref, k_ref, v_ref, *rest, G, tq, tk, rb, mode, window):
    if mode == "bias":
        bias_ref, o_ref, m_sc, acc_sc = rest
    elif mode == "sel":
        sel_ref, e_ref, o_ref, m_sc, acc_sc = rest
    else:
        o_ref, m_sc, acc_sc = rest
    p = pl.program_id(2)
    qi, kj, fl = qi_ref[p], kj_ref[p], fl_ref[p]
    dk = q_ref.shape[-1]
    dv = v_ref.shape[-1]
    R = G * tq

    @pl.when((fl & 1) != 0)
    def _():
        m_sc[...] = jnp.full(m_sc.shape, NEG, F32)
        acc_sc[...] = jnp.zeros(acc_sc.shape, F32)

    def compute(position_mask):
        mask = None
        if position_mask:
            row = qi * tq + lax.broadcasted_iota(I32, (tq, tk), 0)
            col = kj * tk + lax.broadcasted_iota(I32, (tq, tk), 1)
            mask = col <= row
            if window is not None:
                mask = mask & (col > row - window)
        if mode == "sel":
            hit = jnp.dot(sel_ref[0, 0], e_ref[...], preferred_element_type=F32) > 0.5
            mask = hit if mask is None else (mask & hit)
        k = k_ref[0, 0]
        v_ext = jnp.concatenate([v_ref[0, 0], jnp.ones((tk, LANE), v_ref.dtype)], axis=1)
        for r in range(R // rb):
            g, t0 = divmod(r * rb, tq)
            rows = slice(r * rb, (r + 1) * rb)
            s = lax.dot_general(q_ref[0, g, t0:t0 + rb, :], k, (((1,), (1,)), ((), ())),
                                preferred_element_type=F32)
            if mode == "bias":
                s = s + bias_ref[0, 0, t0:t0 + rb, :].astype(F32)
            if mask is not None:
                s = jnp.where(mask[t0:t0 + rb], s, NEG)
            m_prev = m_sc[rows, :]
            m_new = jnp.maximum(m_prev, jnp.max(s, axis=-1, keepdims=True))
            alpha = jnp.exp2(m_prev - m_new)
            pr = jnp.concatenate([jnp.exp2(s[:, c * LANE:(c + 1) * LANE] - m_new).astype(v_ref.dtype)
                                  for c in range(tk // LANE)], axis=1)
            pv = jnp.dot(pr, v_ext, preferred_element_type=F32)
            acc_sc[rows, :] = jnp.concatenate([alpha] * (dv // LANE + 1), axis=1) * acc_sc[rows, :] + pv
            m_sc[rows, :] = m_new

    if mode == "bias":
        compute(False)
    else:
        pl.when((fl & 4) != 0)(functools.partial(compute, True))
        pl.when((fl & 4) == 0)(functools.partial(compute, False))

    @pl.when((fl & 2) != 0)
    def _():
        for g in range(G):
            rows = slice(g * tq, (g + 1) * tq)
            o = acc_sc[rows, 0:dv] / acc_sc[rows, dv:dv + LANE]
            o_ref[0, :, g * dv:(g + 1) * dv] = o.astype(o_ref.dtype)


def _flash(q, k, v, *, mode="causal", window=None, bias=None, sel=None, expand=None,
           tq=512, tk=512, rb=128, out_dtype=None):
    B, Hq, S, dk = q.shape
    Hkv, dv = k.shape[1], v.shape[-1]
    G = Hq // Hkv
    tq, tk = _tile(S, tq, 8), _tile(S, tk)
    qi, kj, fl = _pairs(S, tq, tk, window)
    P = int(qi.shape[0])
    in_specs = [pl.BlockSpec((1, G, tq, dk), lambda b, h, p, qi, kj, fl: (b, h, qi[p], 0)),
                pl.BlockSpec((1, 1, tk, dk), lambda b, h, p, qi, kj, fl: (b, h, kj[p], 0)),
                pl.BlockSpec((1, 1, tk, dv), lambda b, h, p, qi, kj, fl: (b, h, kj[p], 0))]
    args = [q, k, v]
    if mode == "bias":
        in_specs.append(pl.BlockSpec((1, 1, tq, tk), lambda b, h, p, qi, kj, fl: (b, kj[p], qi[p], 0)))
        args.append(bias)
    elif mode == "sel":
        in_specs += [pl.BlockSpec((1, 1, tq, LANE), lambda b, h, p, qi, kj, fl: (b, h, qi[p], 0)),
                     pl.BlockSpec((LANE, tk), lambda b, h, p, qi, kj, fl: (0, kj[p]))]
        args += [sel, expand]
    kern = functools.partial(_flash_kernel, G=G, tq=tq, tk=tk, rb=min(rb, tq), mode=mode, window=window)
    return pl.pallas_call(
        kern,
        out_shape=jax.ShapeDtypeStruct((B, S, Hq * dv), CDT if out_dtype is None else out_dtype),
        grid_spec=pltpu.PrefetchScalarGridSpec(
            num_scalar_prefetch=3,
            grid=(B, Hkv, P),
            in_specs=in_specs,
            out_specs=pl.BlockSpec((1, tq, G * dv), lambda b, h, p, qi, kj, fl: (b, qi[p], h)),
            scratch_shapes=[pltpu.VMEM((G * tq, LANE), F32), pltpu.VMEM((G * tq, dv + LANE), F32)]),
        compiler_params=_params(("parallel", "parallel", "arbitrary")),
    )(qi, kj, fl, *args)


def _mla_qprep_kernel(x_ref, cos_ref, sin_ref, gn_ref, gr_ref, o_ref, *, H, scale):
    ts = x_ref.shape[1]
    lane = lax.broadcasted_iota(I32, (ts, LANE), 1)
    lo = lane < 64
    cos4, sin4 = cos_ref[0], sin_ref[0]
    for h in range(H):
        xn = x_ref[0, :, h * LANE:(h + 1) * LANE]
        o_ref[0, h, :, 0:LANE] = (_rms(xn, gn_ref[...]) * scale).astype(o_ref.dtype)
    for j in range(H // 2):
        xr = x_ref[0, :, (H + j) * LANE:(H + j + 1) * LANE]
        ss = xr * xr
        s_lo = jnp.sum(jnp.where(lo, ss, 0.0), axis=-1, keepdims=True)
        s_hi = jnp.sum(jnp.where(lo, 0.0, ss), axis=-1, keepdims=True)
        inv = jnp.where(lo, lax.rsqrt(s_lo / 64.0 + EPS), lax.rsqrt(s_hi / 64.0 + EPS))
        r = _rope64pair(xr * inv * gr_ref[...], cos4, sin4, lane) * scale
        o_ref[0, 2 * j, :, LANE:2 * LANE] = jnp.where(lo, r, 0.0).astype(o_ref.dtype)
        o_ref[0, 2 * j + 1, :, LANE:2 * LANE] = jnp.where(lo, pltpu.roll(r, 64, 1), 0.0).astype(o_ref.dtype)


def _mla_kvprep_kernel(x_ref, kr_ref, cos_ref, sin_ref, gn_ref, gr_ref, k_ref, v_ref, *, H):
    ts = x_ref.shape[1]
    lane = lax.broadcasted_iota(I32, (ts, LANE), 1)
    kr = kr_ref[0]
    inv = lax.rsqrt(jnp.sum(kr * kr, axis=-1, keepdims=True) / 64.0 + EPS)
    r = _rope64pair(kr * inv * gr_ref[...], cos_ref[0], sin_ref[0], lane).astype(k_ref.dtype)
    for h in range(H):
        xn = x_ref[0, :, h * LANE:(h + 1) * LANE]
        k_ref[0, h, :, 0:LANE] = _rms(xn, gn_ref[...]).astype(k_ref.dtype)
        k_ref[0, h, :, LANE:2 * LANE] = r
        v_ref[0, h] = x_ref[0, :, (H + h) * LANE:(H + h + 1) * LANE].astype(v_ref.dtype)


def _mla(p, kr_block, pos, q_a_norm, kv_a_norm, w_uq, w_ukv, q_norm, k_norm):
    B, S, _ = p.shape
    H = MLA_HEADS
    N = B * S
    cqn = _norm(p, q_a_norm, width=MLA_Q_RANK, col_block=0)
    ckvn = _norm(p, kv_a_norm, width=MLA_KV_RANK, col_block=MLA_Q_RANK // MLA_KV_RANK)
    hh = np.arange(H)[:, None]
    q_perm = np.concatenate([(hh * 192 + np.arange(128)).ravel(), (hh * 192 + 128 + np.arange(64)).ravel()])
    kv_perm = np.concatenate([(hh * 256 + np.arange(128)).ravel(), (hh * 256 + 128 + np.arange(128)).ravel()])
    q_raw = _mm(cqn.reshape(N, -1), w_uq[:, q_perm].astype(CDT), tn=1024, tk=MLA_Q_RANK).reshape(B, S, -1)
    kv_raw = _mm(ckvn.reshape(N, -1), w_ukv[:, kv_perm].astype(CDT), tn=1024, tk=MLA_KV_RANK).reshape(B, S, -1)
    cos4, sin4 = _tables64pair(pos)
    ts = _tile(S, 256, 8)
    scale = (MLA_NOPE + MLA_ROPE) ** -0.5 * LOG2E
    gr = q_norm[MLA_NOPE:]
    tab = pl.BlockSpec((1, ts, LANE), lambda b, i: (b, i, 0))
    vec = pl.BlockSpec((1, LANE), lambda b, i: (0, 0))
    q = pl.pallas_call(
        functools.partial(_mla_qprep_kernel, H=H, scale=scale),
        out_shape=jax.ShapeDtypeStruct((B, H, S, 2 * LANE), CDT),
        grid=(B, S // ts),
        in_specs=[pl.BlockSpec((1, ts, H * 192), lambda b, i: (b, i, 0)), tab, tab, vec, vec],
        out_specs=pl.BlockSpec((1, H, ts, 2 * LANE), lambda b, i: (b, 0, i, 0)),
        compiler_params=_params(("parallel", "parallel")),
    )(q_raw, cos4, sin4, q_norm[:MLA_NOPE].reshape(1, -1), jnp.concatenate([gr, gr]).reshape(1, -1))
    gkr = jnp.concatenate([k_norm[MLA_NOPE:], jnp.zeros((64,), F32)])
    k, v = pl.pallas_call(
        functools.partial(_mla_kvprep_kernel, H=H),
        out_shape=(jax.ShapeDtypeStruct((B, H, S, 2 * LANE), CDT),
                   jax.ShapeDtypeStruct((B, H, S, LANE), CDT)),
        grid=(B, S // ts),
        in_specs=[pl.BlockSpec((1, ts, H * 256), lambda b, i: (b, i, 0)),
                  pl.BlockSpec((1, ts, LANE), lambda b, i: (b, i, kr_block)), tab, tab, vec, vec],
        out_specs=(pl.BlockSpec((1, H, ts, 2 * LANE), lambda b, i: (b, 0, i, 0)),
                   pl.BlockSpec((1, H, ts, LANE), lambda b, i: (b, 0, i, 0))),
        compiler_params=_params(("parallel", "parallel")),
    )(kv_raw, p, cos4, sin4, k_norm[:MLA_NOPE].reshape(1, -1), gkr.reshape(1, -1))
    return _flash(q, k, v, mode="causal", tq=1024, tk=1024, rb=256)


def _nsa_prep_kernel(q_ref, kc_ref, vc_ref, ks_ref, vs_ref, kw_ref, vw_ref, cos_ref, sin_ref, g_ref,
                     qo, kso, vso, kwo, vwo, kco, vco, *, H, G, scale):
    cosf, sinf = cos_ref[0], sin_ref[0]
    for h in range(H):
        y = _rms(q_ref[0, :, h * LANE:(h + 1) * LANE], g_ref[0:1, :])
        qo[0, h] = (_rope128(y, cosf, sinf) * scale).astype(qo.dtype)
    for g in range(G):
        sl = slice(g * LANE, (g + 1) * LANE)
        kso[0, g] = _rope128(_rms(ks_ref[0, :, sl], g_ref[2:3, :]), cosf, sinf).astype(kso.dtype)
        kwo[0, g] = _rope128(_rms(kw_ref[0, :, sl], g_ref[3:4, :]), cosf, sinf).astype(kwo.dtype)
        vso[0, g] = vs_ref[0, :, sl].astype(vso.dtype)
        vwo[0, g] = vw_ref[0, :, sl].astype(vwo.dtype)
        kco[0, g] = kc_ref[0, :, sl].astype(kco.dtype)
        vco[0, g] = vc_ref[0, :, sl].astype(vco.dtype)


def _compress_kernel(xk_ref, xv_ref, wk_ref, wv_ref, pek_ref, pev_ref, g_ref, cos_ref, sin_ref,
                     kc_ref, vc_ref):
    nc = xk_ref.shape[2]

    def comp(x_ref, w_ref, pe_ref):
        y = jnp.dot(x_ref[0, 0], w_ref[...], preferred_element_type=F32)
        c = jnp.dot(pe_ref[...], w_ref[...], preferred_element_type=F32)
        const = c[0:1, 0:LANE] + c[1:2, LANE:2 * LANE]
        return y[:, 0:LANE] + pltpu.roll(y[:, LANE:2 * LANE], nc - 1, 0) + const

    kc = comp(xk_ref, wk_ref, pek_ref)
    kc_ref[0, 0] = _rope128(_rms(kc, g_ref[...]), cos_ref[0], sin_ref[0]).astype(kc_ref.dtype)
    vc_ref[0, 0] = comp(xv_ref, wv_ref, pev_ref).astype(vc_ref.dtype)


def _cmp_attn_kernel(q_ref, kc_ref, vc_ref, m_ref, o_ref, sel_ref, *, G, tq, n_cmp, n_sel, dv):
    i = pl.program_id(2)
    nc = kc_ref.shape[2]
    q = q_ref[0].reshape(G * tq, q_ref.shape[-1])
    s = lax.dot_general(q, kc_ref[0, 0], (((1,), (1,)), ((), ())), preferred_element_type=F32)
    s = s.reshape(G, tq, nc)
    t = i * tq + lax.broadcasted_iota(I32, (tq, nc), 0)
    n = lax.broadcasted_iota(I32, (tq, nc), 1)
    mask = (n * NSA_CMP_STRIDE + (NSA_CMP_LEN - 1) <= t) & (n < n_cmp)
    s = jnp.where(mask[None], s, NEG)
    mx = jnp.max(s, axis=-1, keepdims=True)
    e = jnp.where(mask[None], jnp.exp2(s - mx), 0.0)
    l = jnp.sum(e, axis=-1, keepdims=True)
    pc = e / jnp.where(l > 0.0, l, 1.0)
    o = jnp.dot(pc.reshape(G * tq, nc).astype(vc_ref.dtype), vc_ref[0, 0], preferred_element_type=F32)
    for g in range(G):
        o_ref[0, :, g * dv:(g + 1) * dv] = o[g * tq:(g + 1) * tq].astype(o_ref.dtype)

    ps = jnp.sum(pc, axis=0)
    hi = ps.astype(CDT)
    lo_part = (ps - hi.astype(F32)).astype(CDT)
    imp = (jnp.dot(hi, m_ref[...], preferred_element_type=F32)
           + jnp.dot(lo_part, m_ref[...], preferred_element_type=F32))
    blk = lax.broadcasted_iota(I32, (tq, LANE), 1)
    cur = (i * tq + lax.broadcasted_iota(I32, (tq, LANE), 0)) >> (NSA_SLC_LEN.bit_length() - 1)
    forced = (blk == 0) | (blk == cur) | (blk == cur - 1)
    imp = jnp.where(forced, FORCE, jnp.where(blk <= cur, imp, NEG))
    v = imp.T
    rowi = lax.broadcasted_iota(I32, (LANE, tq), 0)

    def take(_, carry):
        v, chosen = carry
        mval = jnp.max(v, axis=0, keepdims=True)
        first = jnp.min(jnp.where(v == mval, rowi, LANE), axis=0, keepdims=True)
        hit = rowi == first
        chosen = jnp.where(hit & (mval > NEG_HALF), 1.0, chosen)
        return jnp.where(hit, REMOVED, v), chosen

    _, chosen = lax.fori_loop(0, n_sel, take, (v, jnp.zeros((LANE, tq), F32)))
    sel_ref[0, 0] = chosen.T.astype(sel_ref.dtype)


def _nsa_combine_kernel(oc_ref, os_ref, ow_ref, g_ref, o_ref, *, H, dv):
    gate = _sigmoid(g_ref[0])
    for h in range(H):
        sl = slice(h * dv, (h + 1) * dv)
        o = (oc_ref[0, :, sl] * gate[:, h:h + 1] + os_ref[0, :, sl] * gate[:, H + h:H + h + 1]
             + ow_ref[0, :, sl] * gate[:, 2 * H + h:2 * H + h + 1])
        o_ref[0, :, sl] = o.astype(o_ref.dtype)


def _nsa(p, q_block, kv_block0, gate_block, pos, qk_norm, cmp_pos, cmp_w):
    B, S, _ = p.shape
    H, G, DH = NSA_HEADS, NSA_KV_GROUPS, NSA_HEAD_DIM
    HPG = H // G
    scale = DH ** -0.5 * LOG2E
    ts = _tile(S, 256, 8)
    cosf, sinf = _tables128(pos)
    tab = pl.BlockSpec((1, ts, LANE), lambda b, i: (b, i, 0))
    kvspec = [pl.BlockSpec((1, ts, G * DH), functools.partial(lambda b, i, m: (b, i, kv_block0 + m), m=m))
              for m in range(6)]
    head_out = lambda n: pl.BlockSpec((1, n, ts, DH), lambda b, i: (b, 0, i, 0))
    kv_shape = jax.ShapeDtypeStruct((B, G, S, DH), CDT)
    q, ks, vs, kw, vw, kcr, vcr = pl.pallas_call(
        functools.partial(_nsa_prep_kernel, H=H, G=G, scale=scale),
        out_shape=(jax.ShapeDtypeStruct((B, H, S, DH), CDT),) + (kv_shape,) * 6,
        grid=(B, S // ts),
        in_specs=[pl.BlockSpec((1, ts, H * DH), lambda b, i: (b, i, q_block))] + kvspec
                 + [tab, tab, pl.BlockSpec((4, DH), lambda b, i: (0, 0))],
        out_specs=(head_out(H),) + (head_out(G),) * 6,
        compiler_params=_params(("parallel", "parallel")),
    )(p, p, p, p, p, p, p, cosf, sinf, qk_norm)

    half = NSA_CMP_LEN // 2
    nc = S // NSA_CMP_STRIDE
    n_cmp = (S - NSA_CMP_LEN) // NSA_CMP_STRIDE + 1
    cmp_end = jnp.minimum(jnp.arange(nc) * NSA_CMP_STRIDE + NSA_CMP_LEN - 1, S - 1)
    ccos, csin = _tables128(pos[:, cmp_end])
    wcat = lambda w: jnp.concatenate([w[:half].reshape(half * DH, DH), w[half:].reshape(half * DH, DH)], 1).astype(CDT)
    pecat = lambda pe: jnp.zeros((8, half * DH), F32).at[0].set(pe[:half].reshape(-1)).at[1].set(
        pe[half:].reshape(-1)).astype(CDT)
    xspec = pl.BlockSpec((1, 1, nc, half * DH), lambda b, g: (b, g, 0, 0))
    wspec = pl.BlockSpec((half * DH, 2 * DH), lambda b, g: (0, 0))
    pespec = pl.BlockSpec((8, half * DH), lambda b, g: (0, 0))
    cspec = pl.BlockSpec((1, 1, nc, DH), lambda b, g: (b, g, 0, 0))
    ctab = pl.BlockSpec((1, nc, DH), lambda b, g: (b, 0, 0))
    kc, vc = pl.pallas_call(
        _compress_kernel,
        out_shape=(jax.ShapeDtypeStruct((B, G, nc, DH), CDT),) * 2,
        grid=(B, G),
        in_specs=[xspec, xspec, wspec, wspec, pespec, pespec,
                  pl.BlockSpec((1, DH), lambda b, g: (0, 0)), ctab, ctab],
        out_specs=(cspec, cspec),
        compiler_params=_params(("parallel", "parallel")),
    )(kcr.reshape(B, G, nc, half * DH), vcr.reshape(B, G, nc, half * DH), wcat(cmp_w[0]), wcat(cmp_w[1]),
      pecat(cmp_pos[0]), pecat(cmp_pos[1]), qk_norm[1].reshape(1, DH), ccos, csin)

    n_slc = S // NSA_SLC_LEN
    assert n_slc <= LANE
    r, cl = NSA_SLC_LEN // NSA_CMP_STRIDE, NSA_CMP_LEN // NSA_CMP_STRIDE
    m_np = np.zeros((nc, LANE), np.float32)
    for j in range(n_slc):
        for a in range(r):
            for c in range(cl):
                ci = j * r + a + c - (cl - 1)
                if 0 <= ci < n_cmp:
                    m_np[ci, j] += 1.0
    tq = _tile(S, 256, 8)
    o_c, sel = pl.pallas_call(
        functools.partial(_cmp_attn_kernel, G=HPG, tq=tq, n_cmp=n_cmp, n_sel=min(NSA_N_SEL, n_slc), dv=DH),
        out_shape=(jax.ShapeDtypeStruct((B, S, H * DH), F32),
                   jax.ShapeDtypeStruct((B, G, S, LANE), CDT)),
        grid=(B, G, S // tq),
        in_specs=[pl.BlockSpec((1, HPG, tq, DH), lambda b, g, i: (b, g, i, 0)),
                  pl.BlockSpec((1, 1, nc, DH), lambda b, g, i: (b, g, 0, 0)),
                  pl.BlockSpec((1, 1, nc, DH), lambda b, g, i: (b, g, 0, 0)),
                  pl.BlockSpec((nc, LANE), lambda b, g, i: (0, 0))],
        out_specs=(pl.BlockSpec((1, tq, HPG * DH), lambda b, g, i: (b, i, g)),
                   pl.BlockSpec((1, 1, tq, LANE), lambda b, g, i: (b, g, i, 0))),
        compiler_params=_params(("parallel", "parallel", "parallel")),
    )(q, kc, vc, jnp.asarray(m_np, CDT))

    expand = jnp.asarray((np.arange(S)[None, :] // NSA_SLC_LEN) == np.arange(LANE)[:, None], CDT)
    o_s = _flash(q, ks, vs, mode="sel", sel=sel, expand=expand, tq=256, tk=1024, out_dtype=F32)
    o_w = _flash(q, kw, vw, mode="window", window=NSA_WINDOW, tq=512, tk=512, out_dtype=F32)
    ospec = pl.BlockSpec((1, ts, H * DH), lambda b, i: (b, i, 0))
    return pl.pallas_call(
        functools.partial(_nsa_combine_kernel, H=H, dv=DH),
        out_shape=jax.ShapeDtypeStruct((B, S, H * DH), CDT),
        grid=(B, S // ts),
        in_specs=[ospec, ospec, ospec, pl.BlockSpec((1, ts, LANE), lambda b, i: (b, i, gate_block))],
        out_specs=ospec,
        compiler_params=_params(("parallel", "parallel")),
    )(o_c, o_s, o_w, p)


def _dsa_prep_kernel(q_ref, k_ref, v_ref, qi_ref, ki_ref, wi_ref, cos_ref, sin_ref, cos4_ref, sin4_ref,
                     g_ref, gi_ref, qo, ko, vo, qio, kilo, kihi, wio, *, H, HKV, scale, wscale):
    ts = q_ref.shape[1]
    lane = lax.broadcasted_iota(I32, (ts, LANE), 1)
    cosf, sinf, cos4, sin4 = cos_ref[0], sin_ref[0], cos4_ref[0], sin4_ref[0]
    for h in range(H):
        y = _rms(q_ref[0, :, h * LANE:(h + 1) * LANE], g_ref[0:1, :])
        qo[0, h] = (_rope128(y, cosf, sinf) * scale).astype(qo.dtype)
    for h in range(HKV):
        sl = slice(h * LANE, (h + 1) * LANE)
        ko[0, h] = _rope128(_rms(k_ref[0, :, sl], g_ref[1:2, :]), cosf, sinf).astype(ko.dtype)
        vo[0, h] = v_ref[0, :, sl].astype(vo.dtype)
    for j in range(qi_ref.shape[2] // LANE):
        qio[0, j] = _rope64pair(qi_ref[0, :, j * LANE:(j + 1) * LANE], cos4, sin4, lane).astype(qio.dtype)
    ki = ki_ref[0]
    inv = lax.rsqrt(jnp.sum(ki * ki, axis=-1, keepdims=True) / 64.0 + EPS)
    r = _rope64pair(ki * inv * gi_ref[...], cos4, sin4, lane)
    kilo[0] = r.astype(kilo.dtype)
    kihi[0] = pltpu.roll(r, 64, 1).astype(kihi.dtype)
    wio[0] = wi_ref[0] * wscale


def _indexer_kernel(qi_ref, kj_ref, fl_ref, q_ref, klo_ref, khi_ref, w_ref, o_ref, wb_sc, sc_sc,
                    *, tq, tk, topk, n_pairs, rg):
    p = pl.program_id(1)
    qi, kj, fl = qi_ref[p], kj_ref[p], fl_ref[p]
    n_tiles = o_ref.shape[1]

    @pl.when((fl & 1) != 0)
    def _():
        w = w_ref[0]
        for h in range(2 * n_pairs):
            wb_sc[h] = jnp.broadcast_to(w[:, h:h + 1], (tq, LANE))

    def pair(j, acc):
        q = q_ref[0, j]
        sa = lax.dot_general(q, klo_ref[0], (((1,), (1,)), ((), ())), preferred_element_type=F32)
        sb = lax.dot_general(q, khi_ref[0], (((1,), (1,)), ((), ())), preferred_element_type=F32)
        wa = jnp.tile(wb_sc[2 * j], (1, tk // LANE))
        wb = jnp.tile(wb_sc[2 * j + 1], (1, tk // LANE))
        return acc + wa * jnp.maximum(sa, 0.0) + wb * jnp.maximum(sb, 0.0)

    score = lax.fori_loop(0, n_pairs, pair, jnp.zeros((tq, tk), F32))
    row = qi * tq + lax.broadcasted_iota(I32, (tq, tk), 0)
    col = kj * tk + lax.broadcasted_iota(I32, (tq, tk), 1)
    score = jnp.where(col <= row, score, NEG)
    bits = pltpu.bitcast(score, I32)
    key = bits ^ ((bits >> 31) & 0x7FFFFFFF)
    sc_sc[kj] = key

    @pl.when((fl & 2) != 0)
    def _():
        n_chunks = kj + 1
        nh_bits = int(np.float32(NEG_HALF).view(np.int32))
        key_neg_half = nh_bits ^ 0x7FFFFFFF if nh_bits < 0 else nh_bits
        for g in range(tq // rg):
            rows = pl.ds(g * rg, rg)

            def bit_step(b, thr):
                cand = thr + jnp.left_shift(jnp.int32(1), 31 - b)

                def count(c, cnt):
                    blk = sc_sc[c, rows, :]
                    for u in range(tk // LANE):
                        cnt = cnt + (blk[:, u * LANE:(u + 1) * LANE] >= cand).astype(I32)
                    return cnt

                cnt = lax.fori_loop(0, n_chunks, count, jnp.zeros((rg, LANE), I32))
                tot = jnp.sum(cnt, axis=1, keepdims=True)
                return jnp.where(tot >= topk, cand, thr)

            thr = lax.fori_loop(0, 32, bit_step, jnp.full((rg, LANE), -2**31, I32))
            thr = jnp.maximum(thr, key_neg_half + 1)
            thr_t = jnp.tile(thr, (1, tk // LANE))

            def emit(c, _):
                o_ref[0, c, rows, :] = jnp.where(sc_sc[c, rows, :] >= thr_t, 0.0, NEG).astype(o_ref.dtype)
                return 0

            def emit_masked(c, _):
                o_ref[0, c, rows, :] = jnp.full((rg, tk), NEG, o_ref.dtype)
                return 0

            lax.fori_loop(0, n_chunks, emit, 0)
            lax.fori_loop(n_chunks, n_tiles, emit_masked, 0)


def _dsa(p, pos, qk_norm, idx_k_norm):
    B, S, _ = p.shape
    H, HKV, DH = DSA_HEADS, DSA_KV_HEADS, DSA_HEAD_DIM
    NP = IDX_HEADS // 2
    ts = _tile(S, 256, 8)
    cosf, sinf = _tables128(pos)
    cos4, sin4 = _tables64pair(pos)
    tab = pl.BlockSpec((1, ts, LANE), lambda b, i: (b, i, 0))
    kw = HKV * DH
    gi = jnp.concatenate([idx_k_norm, jnp.zeros((LANE - IDX_DIM,), F32)]).reshape(1, LANE)
    head_out = lambda n: pl.BlockSpec((1, n, ts, DH), lambda b, i: (b, 0, i, 0))
    q, k, v, qidx, kilo, kihi, wi = pl.pallas_call(
        functools.partial(_dsa_prep_kernel, H=H, HKV=HKV, scale=DH ** -0.5 * LOG2E,
                          wscale=IDX_HEADS ** -0.5 * IDX_DIM ** -0.5),
        out_shape=(jax.ShapeDtypeStruct((B, H, S, DH), CDT), jax.ShapeDtypeStruct((B, HKV, S, DH), CDT),
                   jax.ShapeDtypeStruct((B, HKV, S, DH), CDT), jax.ShapeDtypeStruct((B, NP, S, LANE), CDT),
                   jax.ShapeDtypeStruct((B, S, LANE), CDT), jax.ShapeDtypeStruct((B, S, LANE), CDT),
                   jax.ShapeDtypeStruct((B, S, LANE), F32)),
        grid=(B, S // ts),
        in_specs=[pl.BlockSpec((1, ts, H * DH), lambda b, i: (b, i, 0)),
                  pl.BlockSpec((1, ts, kw), lambda b, i: (b, i, H * DH // kw)),
                  pl.BlockSpec((1, ts, kw), lambda b, i: (b, i, H * DH // kw + 1)),
                  pl.BlockSpec((1, ts, NP * LANE), lambda b, i: (b, i, (H * DH + 2 * kw) // (NP * LANE))),
                  pl.BlockSpec((1, ts, LANE), lambda b, i: (b, i, (H * DH + 2 * kw + NP * LANE) // LANE)),
                  pl.BlockSpec((1, ts, LANE), lambda b, i: (b, i, (H * DH + 2 * kw + NP * LANE) // LANE + 1)),
                  tab, tab, tab, tab,
                  pl.BlockSpec((2, DH), lambda b, i: (0, 0)), pl.BlockSpec((1, LANE), lambda b, i: (0, 0))],
        out_specs=(head_out(H), head_out(HKV), head_out(HKV), head_out(NP), tab, tab, tab),
        compiler_params=_params(("parallel", "parallel")),
    )(p, p, p, p, p, p, cosf, sinf, cos4, sin4, qk_norm, gi)

    topk = min(DSA_TOPK_MAX, S // 4)
    tq, tk = _tile(S, 256, 8), _tile(S, 1024)
    qi_t, kj_t, fl_t = _pairs(S, tq, tk)
    bias = pl.pallas_call(
        functools.partial(_indexer_kernel, tq=tq, tk=tk, topk=topk, n_pairs=NP, rg=min(64, tq)),
        out_shape=jax.ShapeDtypeStruct((B, S // tk, S, tk), CDT),
        grid_spec=pltpu.PrefetchScalarGridSpec(
            num_scalar_prefetch=3,
            grid=(B, int(qi_t.shape[0])),
            in_specs=[pl.BlockSpec((1, NP, tq, LANE), lambda b, p, qi, kj, fl: (b, 0, qi[p], 0)),
                      pl.BlockSpec((1, tk, LANE), lambda b, p, qi, kj, fl: (b, kj[p], 0)),
                      pl.BlockSpec((1, tk, LANE), lambda b, p, qi, kj, fl: (b, kj[p], 0)),
                      pl.BlockSpec((1, tq, LANE), lambda b, p, qi, kj, fl: (b, qi[p], 0))],
            out_specs=pl.BlockSpec((1, S // tk, tq, tk), lambda b, p, qi, kj, fl: (b, 0, qi[p], 0)),
            scratch_shapes=[pltpu.VMEM((2 * NP, tq, LANE), F32), pltpu.VMEM((S // tk, tq, tk), I32)]),
        compiler_params=_params(("parallel", "arbitrary")),
    )(qi_t, kj_t, fl_t, qidx, kilo, kihi, wi)
    return _flash(q, k, v, mode="bias", bias=bias, tq=tq, tk=tk)


def _router_kernel(l_ref, o_ref, *, n_experts):
    x = l_ref[...]
    lane = lax.broadcasted_iota(I32, x.shape, 1)
    x = jnp.where(lane < n_experts, x, -jnp.inf)
    m1 = jnp.max(x, axis=1, keepdims=True)
    i1 = jnp.min(jnp.where(x == m1, lane, LANE), axis=1, keepdims=True)
    x2 = jnp.where(lane == i1, -jnp.inf, x)
    m2 = jnp.max(x2, axis=1, keepdims=True)
    i2 = jnp.min(jnp.where(x2 == m2, lane, LANE), axis=1, keepdims=True)
    e2 = jnp.exp(m2 - m1)
    p1 = 1.0 / (1.0 + e2)
    p2 = e2 / (1.0 + e2)
    o_ref[...] = jnp.where(lane == i1, p1, 0.0) + jnp.where(lane == i2, p2, 0.0)


def _moe(h, x, g_f, w_router, w_gate, w_up, w_down, S):
    N, D = h.shape
    E, _, DE = w_gate.shape
    wr = jnp.zeros((D, LANE), F32).at[:, :E].set(w_router).astype(CDT)
    logits = _mm(h, wr, tn=LANE, tk=D)
    tm = _tile(N, 1024, 8)
    comb = pl.pallas_call(
        functools.partial(_router_kernel, n_experts=E),
        out_shape=jax.ShapeDtypeStruct((N, LANE), F32),
        grid=(N // tm,),
        in_specs=[pl.BlockSpec((tm, LANE), lambda i: (i, 0))],
        out_specs=pl.BlockSpec((tm, LANE), lambda i: (i, 0)),
        compiler_params=_params(("parallel",)),
    )(logits)
    wg = jnp.transpose(w_gate, (1, 0, 2)).reshape(D, E * DE).astype(CDT)
    wu = jnp.transpose(w_up, (1, 0, 2)).reshape(D, E * DE).astype(CDT)
    hid = _mm(h, wg, mode="swiglu_scaled", b2=wu, comb=comb, out_dtype=CDT, tn=DE, tk=1024, d_expert=DE)
    return _mm(hid, w_down.reshape(E * DE, D).astype(CDT), mode="res", x=x, g=g_f,
               tn=1024, tk=1024, rows_per_batch=S)


def _pad_cols(blocks, total):
    cols = []
    for w, width in blocks:
        cols.append(w)
        if width > w.shape[1]:
            cols.append(jnp.zeros((w.shape[0], width - w.shape[1]), w.dtype))
    out = jnp.concatenate(cols, axis=1)
    if total > out.shape[1]:
        out = jnp.concatenate([out, jnp.zeros((out.shape[0], total - out.shape[1]), out.dtype)], axis=1)
    return out.astype(CDT)


def _round_up(n, m):
    return (n + m - 1) // m * m


def kernel(x, c, positions, ada_w, ada_b, ada_table, norm_g, ev_w_in, ev_w_out, mla_q_a_norm, mla_kv_a_norm, mla_w_uq, mla_w_ukv, mla_q_norm, mla_k_norm, nsa_qk_norm, nsa_cmp_pos, nsa_cmp_w, ffn_w_gate, ffn_w_up, ffn_w_down, od_w_in, od_w_out, dsa_qk_norm, idx_k_norm, moe_router, moe_w_gate, moe_w_up, moe_w_down):
    B, S, D = x.shape
    N = B * S
    depth = ada_table.shape[0]
    cond = _cond(c, ada_w, ada_b).reshape(B, 6, D)

    mla_in = MLA_Q_RANK + MLA_KV_RANK + MLA_ROPE
    nq = NSA_HEADS * NSA_HEAD_DIM
    nkv = 6 * NSA_KV_GROUPS * NSA_HEAD_DIM
    hn = np.arange(NSA_HEADS)
    gate_perm = np.concatenate([hn * 3 + r for r in range(3)])

    x2 = x.reshape(N, D)
    for l in range(depth):
        i = l // 2
        mod = cond + ada_table[l]
        sh_a, sc_a, g_a, sh_f, sc_f, g_f = [mod[:, j, None, :] for j in range(6)]
        h = _norm(x2.reshape(B, S, D), norm_g[l, 0], sc_a, sh_a).reshape(N, D)
        if l % 2 == 0:
            w = ev_w_in[i]
            nsa = w[:, mla_in:]
            blocks = [(w[:, :MLA_Q_RANK + MLA_KV_RANK], MLA_Q_RANK + MLA_KV_RANK),
                      (nsa[:, :nq + nkv], nq + nkv),
                      (w[:, MLA_Q_RANK + MLA_KV_RANK:mla_in], LANE),
                      (nsa[:, nq + nkv:][:, gate_perm], LANE)]
            width = MLA_Q_RANK + MLA_KV_RANK + nq + nkv + 2 * LANE
            w_in = _pad_cols(blocks, _round_up(width, 512))
            p = _mm(h, w_in).reshape(B, S, -1)
            off = MLA_Q_RANK + MLA_KV_RANK
            a_out = _mla(p, (off + nq + nkv) // LANE, positions, mla_q_a_norm[i], mla_kv_a_norm[i],
                         mla_w_uq[i], mla_w_ukv[i], mla_q_norm[i], mla_k_norm[i])
            b_out = _nsa(p, off // nq, (off + nq) // (NSA_KV_GROUPS * NSA_HEAD_DIM),
                         (off + nq + nkv) // LANE + 1, positions, nsa_qk_norm[i], nsa_cmp_pos[i], nsa_cmp_w[i])
            mix = jnp.concatenate([a_out, b_out], axis=-1).reshape(N, -1)
            w_out = ev_w_out[i]
        else:
            w = od_w_in[i]
            main = DSA_HEADS * DSA_HEAD_DIM + 2 * DSA_KV_HEADS * DSA_HEAD_DIM + IDX_HEADS * IDX_DIM
            blocks = [(w[:, :main], main), (w[:, main:main + IDX_DIM], LANE), (w[:, main + IDX_DIM:], LANE)]
            w_in = _pad_cols(blocks, _round_up(main + 2 * LANE, 512))
            p = _mm(h, w_in).reshape(B, S, -1)
            mix = _dsa(p, positions, dsa_qk_norm[i], idx_k_norm[i]).reshape(N, -1)
            w_out = od_w_out[i]
        x2 = _mm(mix, w_out.astype(CDT), mode="res", x=x2, g=g_a, rows_per_batch=S)
        h = _norm(x2.reshape(B, S, D), norm_g[l, 1], sc_f, sh_f).reshape(N, D)
        if l % 2 == 0:
            hid = _mm(h, ffn_w_gate[i].astype(CDT), mode="swiglu", b2=ffn_w_up[i].astype(CDT), out_dtype=CDT)
            x2 = _mm(hid, ffn_w_down[i].astype(CDT), mode="res", x=x2, g=g_f, tn=1024, tk=2048,
                     rows_per_batch=S)
        else:
            x2 = _moe(h, x2, g_f, moe_router[i], moe_w_gate[i], moe_w_up[i], moe_w_down[i], S)
    return x2.reshape(B, S, D)
```

```python
import functools

import numpy as np
import jax
import jax.numpy as jnp
from jax import lax
from jax.experimental import pallas as pl
from jax.experimental.pallas import tpu as pltpu

F32 = jnp.float32
I32 = jnp.int32
CDT = jnp.bfloat16

ROPE_THETA = 10000.0
EPS = 1e-6
NEG = -1e30
NEG_HALF = -5e29
FORCE = 1e9
REMOVED = -3e38
LOG2E = 1.4426950408889634

MLA_HEADS, MLA_Q_RANK, MLA_KV_RANK, MLA_NOPE, MLA_ROPE, MLA_V = 16, 1536, 512, 128, 64, 128
NSA_HEADS, NSA_KV_GROUPS, NSA_HEAD_DIM = 16, 4, 128
NSA_CMP_LEN, NSA_CMP_STRIDE, NSA_SLC_LEN, NSA_N_SEL, NSA_WINDOW = 32, 16, 64, 16, 512
DSA_HEADS, DSA_KV_HEADS, DSA_HEAD_DIM, IDX_HEADS, IDX_DIM, DSA_TOPK_MAX = 32, 8, 128, 32, 64, 256
N_EXPERTS = 8

LANE = 128
VMEM_LIMIT = 56 * 2**20


def _tile(n, pref, mult=LANE):
    if n <= pref:
        return n
    t = (pref // mult) * mult
    while t >= mult:
        if n % t == 0:
            return t
        t -= mult
    return n


def _params(sem):
    return pltpu.CompilerParams(dimension_semantics=sem, vmem_limit_bytes=VMEM_LIMIT)


def _silu(x):
    return x / (1.0 + jnp.exp(-x))


def _sigmoid(x):
    return 1.0 / (1.0 + jnp.exp(-x))


def _rms(x, g):
    return x * lax.rsqrt(jnp.mean(x * x, axis=-1, keepdims=True) + EPS) * g


def _rope128(y, cosf, sinf):
    return y * cosf + pltpu.roll(y, 64, 1) * sinf


def _rope64pair(y, cos4, sin4, lane):
    rot = jnp.where((lane & 63) < 32, pltpu.roll(y, 96, 1), pltpu.roll(y, 32, 1))
    return y * cos4 + rot * sin4


def _mm_kernel(*refs, nk, mode, grouped):
    if grouped:
        refs = refs[1:]
    k = pl.program_id(2)
    n_in = {"plain": 2, "res": 4, "swiglu": 3, "rowscale": 3}[mode]
    o_ref = refs[n_in]
    acc, acc2 = (tuple(refs[n_in + 1:]) + (None, None))[:2]
    if mode == "plain":
        a_ref, b_ref = refs[:n_in]
    elif mode == "res":
        a_ref, b_ref, x_ref, g_ref = refs[:n_in]
    elif mode == "swiglu":
        a_ref, b_ref, b2_ref = refs[:n_in]
    else:
        a_ref, b_ref, c_ref = refs[:n_in]

    a = a_ref[...]
    part = jnp.dot(a, b_ref[...].astype(a.dtype), preferred_element_type=F32)
    if mode == "swiglu":
        part2 = jnp.dot(a, b2_ref[...].astype(a.dtype), preferred_element_type=F32)

    if nk > 1:
        @pl.when(k == 0)
        def _():
            acc[...] = part
            if mode == "swiglu":
                acc2[...] = part2

        @pl.when(k > 0)
        def _():
            acc[...] += part
            if mode == "swiglu":
                acc2[...] += part2

    def finish():
        r = acc[...] if nk > 1 else part
        if mode == "plain":
            o_ref[...] = r.astype(o_ref.dtype)
        elif mode == "res":
            o_ref[...] = (x_ref[...] + g_ref[0] * r).astype(o_ref.dtype)
        elif mode == "rowscale":
            o_ref[...] = (r * c_ref[...]).astype(o_ref.dtype)
        else:
            r2 = acc2[...] if nk > 1 else part2
            o_ref[...] = (_silu(r) * r2).astype(o_ref.dtype)

    if nk > 1:
        pl.when(k == nk - 1)(finish)
    else:
        finish()


def _mm(a, b, *, mode="plain", b2=None, x=None, g=None, rowscale=None, group=None, out_dtype=F32,
        tm=1024, tn=512, tk=4096, rows_per_batch=None):
    M, K = a.shape
    N = b.shape[-1]
    tm, tn, tk = _tile(M, tm, 8), _tile(N, tn), _tile(K, tk)
    nk = K // tk
    grid = (M // tm, N // tn, nk)
    grouped = group is not None
    if grouped:
        assert group.shape == (M // tm,)
        b_spec = pl.BlockSpec((None, tk, tn), lambda i, j, k, ge: (ge[i], k, j))
    else:
        b_spec = pl.BlockSpec((tk, tn), lambda i, j, k, *_: (k, j))
    in_specs = [pl.BlockSpec((tm, tk), lambda i, j, k, *_: (i, k)), b_spec]
    args = [a, b]
    scratch = [pltpu.VMEM((tm, tn), F32)] if nk > 1 else []
    if mode == "swiglu":
        in_specs.append(b_spec)
        args.append(b2)
        scratch = scratch * 2
    if mode == "rowscale":
        in_specs.append(pl.BlockSpec((tm, 1), lambda i, j, k, *_: (i, 0)))
        args.append(rowscale)
    if mode == "res":
        rpb = rows_per_batch
        assert rpb % tm == 0
        in_specs += [pl.BlockSpec((tm, tn), lambda i, j, k, *_: (i, j)),
                     pl.BlockSpec((1, 1, tn), lambda i, j, k, *_: ((i * tm) // rpb, 0, j))]
        args += [x, g]
    return pl.pallas_call(
        functools.partial(_mm_kernel, nk=nk, mode=mode, grouped=grouped),
        out_shape=jax.ShapeDtypeStruct((M, N), out_dtype),
        grid_spec=pltpu.PrefetchScalarGridSpec(
            num_scalar_prefetch=1 if grouped else 0,
            grid=grid,
            in_specs=in_specs,
            out_specs=pl.BlockSpec((tm, tn), lambda i, j, k, *_: (i, j)),
            scratch_shapes=scratch),
        compiler_params=_params(("parallel", "parallel", "arbitrary")),
    )(*(([group] if grouped else []) + args))


def _cond_kernel(c_ref, w_ref, b_ref, o_ref):
    a = _silu(c_ref[...]).astype(CDT)
    o_ref[...] = jnp.dot(a, w_ref[...].astype(CDT), preferred_element_type=F32) + b_ref[...]


def _cond(c, ada_w, ada_b):
    B, D = c.shape
    N = ada_w.shape[1]
    cp = jnp.zeros((8, D), F32).at[:B].set(c)
    tn = _tile(N, 512)
    out = pl.pallas_call(
        _cond_kernel,
        out_shape=jax.ShapeDtypeStruct((8, N), F32),
        grid=(N // tn,),
        in_specs=[pl.BlockSpec((8, D), lambda j: (0, 0)),
                  pl.BlockSpec((D, tn), lambda j: (0, j)),
                  pl.BlockSpec((1, tn), lambda j: (0, j))],
        out_specs=pl.BlockSpec((8, tn), lambda j: (0, j)),
        compiler_params=_params(("parallel",)),
    )(cp, ada_w, ada_b.reshape(1, N))
    return out[:B]


def _norm_kernel(*refs, modulate, n_out):
    outs = refs[len(refs) - n_out:]
    if modulate:
        x_ref, g_ref, sc_ref, sh_ref = refs[:4]
    else:
        x_ref, g_ref = refs[:2]
    y = _rms(x_ref[0], g_ref[...])
    if modulate:
        y = y * (1.0 + sc_ref[0]) + sh_ref[0]
    for o_ref in outs:
        o_ref[0] = y.astype(o_ref.dtype)


def _norm(x, g, sc=None, sh=None, *, width=None, col_block=0, ts=256, also_f32=False):
    B, S, W = x.shape
    width = W if width is None else width
    ts = _tile(S, ts, 8)
    modulate = sc is not None
    in_specs = [pl.BlockSpec((1, ts, width), lambda b, i: (b, i, col_block)),
                pl.BlockSpec((1, width), lambda b, i: (0, 0))]
    args = [x, g.reshape(1, width)]
    if modulate:
        in_specs += [pl.BlockSpec((1, 1, width), lambda b, i: (b, 0, 0))] * 2
        args += [sc, sh]
    dtypes = (CDT, F32) if also_f32 else (CDT,)
    out = pl.pallas_call(
        functools.partial(_norm_kernel, modulate=modulate, n_out=len(dtypes)),
        out_shape=tuple(jax.ShapeDtypeStruct((B, S, width), d) for d in dtypes),
        grid=(B, S // ts),
        in_specs=in_specs,
        out_specs=tuple(pl.BlockSpec((1, ts, width), lambda b, i: (b, i, 0)) for _ in dtypes),
        compiler_params=_params(("parallel", "parallel")),
    )(*args)
    return out if also_f32 else out[0]


def _rope_angles(pos, dim):
    inv = ROPE_THETA ** (-jnp.arange(0, dim, 2, dtype=F32) / dim)
    ang = pos.astype(F32)[..., None] * inv
    return jnp.cos(ang), jnp.sin(ang)


def _tables128(pos):
    c, s = _rope_angles(pos, 128)
    return jnp.concatenate([c, c], -1), jnp.concatenate([-s, s], -1)


def _tables64pair(pos):
    c, s = _rope_angles(pos, 64)
    return jnp.concatenate([c, c, c, c], -1), jnp.concatenate([-s, s, -s, s], -1)


def _pairs(S, tq, tk, window=None):
    qi, kj, fl = [], [], []
    for i in range(S // tq):
        lo = 0 if window is None else max(0, i * tq - window + 1)
        js = list(range(lo // tk, (i * tq + tq - 1) // tk + 1))
        for n, j in enumerate(js):
            diag = (j + 1) * tk - 1 > i * tq
            f = (1 if n == 0 else 0) | (2 if n == len(js) - 1 else 0)
            f |= 4 if (diag or window is not None) else 0
            qi.append(i), kj.append(j), fl.append(f)
    return (jnp.asarray(qi, I32), jnp.asarray(kj, I32), jnp.asarray(fl, I32))


def _flash_kernel(qi_ref, kj_ref, fl_ref, q_ref, k_ref, v_ref, *rest, G, tq, tk, rb, mode, window):
    if mode == "bias":
        bias_ref, o_ref, m_sc, acc_sc = rest
    elif mode == "sel":
        sel_ref, e_ref, o_ref, m_sc, acc_sc = rest
    else:
        o_ref, m_sc, acc_sc = rest
    p = pl.program_id(2)
    qi, kj, fl = qi_ref[p], kj_ref[p], fl_ref[p]
    dk = q_ref.shape[-1]
    dv = v_ref.shape[-1]
    R = G * tq

    @pl.when((fl & 1) != 0)
    def _():
        m_sc[...] = jnp.full(m_sc.shape, NEG, F32)
        acc_sc[...] = jnp.zeros(acc_sc.shape, F32)

    def compute(position_mask):
        mask = None
        if position_mask:
            row = qi * tq + lax.broadcasted_iota(I32, (tq, tk), 0)
            col = kj * tk + lax.broadcasted_iota(I32, (tq, tk), 1)
            mask = col <= row
            if window is not None:
                mask = mask & (col > row - window)
        if mode == "sel":
            hit = jnp.dot(sel_ref[0, 0], e_ref[...], preferred_element_type=F32) > 0.5
            mask = hit if mask is None else (mask & hit)
        k = k_ref[0, 0]
        v_ext = jnp.concatenate([v_ref[0, 0], jnp.ones((tk, LANE), v_ref.dtype)], axis=1)
        for r in range(R // rb):
            g, t0 = divmod(r * rb, tq)
            rows = slice(r * rb, (r + 1) * rb)
            s = lax.dot_general(q_ref[0, g, t0:t0 + rb, :], k, (((1,), (1,)), ((), ())),
                                preferred_element_type=F32)
            if mode == "bias":
                s = s + bias_ref[0, 0, t0:t0 + rb, :].astype(F32)
            if mask is not None:
                s = jnp.where(mask[t0:t0 + rb], s, NEG)
            m_prev = m_sc[rows, :]
            m_new = jnp.maximum(m_prev, jnp.max(s, axis=-1, keepdims=True))
            alpha = jnp.exp2(m_prev - m_new)
            pr = jnp.concatenate([jnp.exp2(s[:, c * LANE:(c + 1) * LANE] - m_new).astype(v_ref.dtype)
                                  for c in range(tk // LANE)], axis=1)
            pv = jnp.dot(pr, v_ext, preferred_element_type=F32)
            acc_sc[rows, :] = jnp.concatenate([alpha] * (dv // LANE + 1), axis=1) * acc_sc[rows, :] + pv
            m_sc[rows, :] = m_new

    if mode == "bias":
        compute(False)
    else:
        pl.when((fl & 4) != 0)(functools.partial(compute, True))
        pl.when((fl & 4) == 0)(functools.partial(compute, False))

    @pl.when((fl & 2) != 0)
    def _():
        for g in range(G):
            rows = slice(g * tq, (g + 1) * tq)
            o = acc_sc[rows, 0:dv] / acc_sc[rows, dv:dv + LANE]
            o_ref[0, :, g * dv:(g + 1) * dv] = o.astype(o_ref.dtype)


def _flash(q, k, v, *, mode="causal", window=None, bias=None, sel=None, expand=None,
           tq=512, tk=512, rb=128, out_dtype=None):
    B, Hq, S, dk = q.shape
    Hkv, dv = k.shape[1], v.shape[-1]
    G = Hq // Hkv
    tq, tk = _tile(S, tq, 8), _tile(S, tk)
    qi, kj, fl = _pairs(S, tq, tk, window)
    P = int(qi.shape[0])
    in_specs = [pl.BlockSpec((1, G, tq, dk), lambda b, h, p, qi, kj, fl: (b, h, qi[p], 0)),
                pl.BlockSpec((1, 1, tk, dk), lambda b, h, p, qi, kj, fl: (b, h, kj[p], 0)),
                pl.BlockSpec((1, 1, tk, dv), lambda b, h, p, qi, kj, fl: (b, h, kj[p], 0))]
    args = [q, k, v]
    if mode == "bias":
        in_specs.append(pl.BlockSpec((1, 1, tq, tk), lambda b, h, p, qi, kj, fl: (b, kj[p], qi[p], 0)))
        args.append(bias)
    elif mode == "sel":
        in_specs += [pl.BlockSpec((1, 1, tq, LANE), lambda b, h, p, qi, kj, fl: (b, h, qi[p], 0)),
                     pl.BlockSpec((LANE, tk), lambda b, h, p, qi, kj, fl: (0, kj[p]))]
        args += [sel, expand]
    kern = functools.partial(_flash_kernel, G=G, tq=tq, tk=tk, rb=min(rb, tq), mode=mode, window=window)
    return pl.pallas_call(
        kern,
        out_shape=jax.ShapeDtypeStruct((B, S, Hq * dv), CDT if out_dtype is None else out_dtype),
        grid_spec=pltpu.PrefetchScalarGridSpec(
            num_scalar_prefetch=3,
            grid=(B, Hkv, P),
            in_specs=in_specs,
            out_specs=pl.BlockSpec((1, tq, G * dv), lambda b, h, p, qi, kj, fl: (b, qi[p], h)),
            scratch_shapes=[pltpu.VMEM((G * tq, LANE), F32), pltpu.VMEM((G * tq, dv + LANE), F32)]),
        compiler_params=_params(("parallel", "parallel", "arbitrary")),
    )(qi, kj, fl, *args)


def _mla_qprep_kernel(x_ref, cos_ref, sin_ref, gn_ref, gr_ref, o_ref, *, H, scale):
    ts = x_ref.shape[1]
    lane = lax.broadcasted_iota(I32, (ts, LANE), 1)
    lo = lane < 64
    cos4, sin4 = cos_ref[0], sin_ref[0]
    for h in range(H):
        xn = x_ref[0, :, h * LANE:(h + 1) * LANE]
        o_ref[0, h, :, 0:LANE] = (_rms(xn, gn_ref[...]) * scale).astype(o_ref.dtype)
    for j in range(H // 2):
        xr = x_ref[0, :, (H + j) * LANE:(H + j + 1) * LANE]
        ss = xr * xr
        s_lo = jnp.sum(jnp.where(lo, ss, 0.0), axis=-1, keepdims=True)
        s_hi = jnp.sum(jnp.where(lo, 0.0, ss), axis=-1, keepdims=True)
        inv = jnp.where(lo, lax.rsqrt(s_lo / 64.0 + EPS), lax.rsqrt(s_hi / 64.0 + EPS))
        r = _rope64pair(xr * inv * gr_ref[...], cos4, sin4, lane) * scale
        o_ref[0, 2 * j, :, LANE:2 * LANE] = jnp.where(lo, r, 0.0).astype(o_ref.dtype)
        o_ref[0, 2 * j + 1, :, LANE:2 * LANE] = jnp.where(lo, pltpu.roll(r, 64, 1), 0.0).astype(o_ref.dtype)


def _mla_kvprep_kernel(x_ref, kr_ref, cos_ref, sin_ref, gn_ref, gr_ref, k_ref, v_ref, *, H):
    ts = x_ref.shape[1]
    lane = lax.broadcasted_iota(I32, (ts, LANE), 1)
    kr = kr_ref[0]
    inv = lax.rsqrt(jnp.sum(kr * kr, axis=-1, keepdims=True) / 64.0 + EPS)
    r = _rope64pair(kr * inv * gr_ref[...], cos_ref[0], sin_ref[0], lane).astype(k_ref.dtype)
    for h in range(H):
        xn = x_ref[0, :, h * LANE:(h + 1) * LANE]
        k_ref[0, h, :, 0:LANE] = _rms(xn, gn_ref[...]).astype(k_ref.dtype)
        k_ref[0, h, :, LANE:2 * LANE] = r
        v_ref[0, h] = x_ref[0, :, (H + h) * LANE:(H + h + 1) * LANE].astype(v_ref.dtype)


def _mla(p, kr_block, pos, q_a_norm, kv_a_norm, w_uq, w_ukv, q_norm, k_norm):
    B, S, _ = p.shape
    H = MLA_HEADS
    N = B * S
    cqn = _norm(p, q_a_norm, width=MLA_Q_RANK, col_block=0)
    ckvn = _norm(p, kv_a_norm, width=MLA_KV_RANK, col_block=MLA_Q_RANK // MLA_KV_RANK)
    hh = np.arange(H)[:, None]
    q_perm = np.concatenate([(hh * 192 + np.arange(128)).ravel(), (hh * 192 + 128 + np.arange(64)).ravel()])
    kv_perm = np.concatenate([(hh * 256 + np.arange(128)).ravel(), (hh * 256 + 128 + np.arange(128)).ravel()])
    q_raw = _mm(cqn.reshape(N, -1), w_uq[:, q_perm].astype(CDT), tn=1024, tk=MLA_Q_RANK).reshape(B, S, -1)
    kv_raw = _mm(ckvn.reshape(N, -1), w_ukv[:, kv_perm].astype(CDT), tn=1024, tk=MLA_KV_RANK).reshape(B, S, -1)
    cos4, sin4 = _tables64pair(pos)
    ts = _tile(S, 256, 8)
    scale = (MLA_NOPE + MLA_ROPE) ** -0.5 * LOG2E
    gr = q_norm[MLA_NOPE:]
    tab = pl.BlockSpec((1, ts, LANE), lambda b, i: (b, i, 0))
    vec = pl.BlockSpec((1, LANE), lambda b, i: (0, 0))
    q = pl.pallas_call(
        functools.partial(_mla_qprep_kernel, H=H, scale=scale),
        out_shape=jax.ShapeDtypeStruct((B, H, S, 2 * LANE), CDT),
        grid=(B, S // ts),
        in_specs=[pl.BlockSpec((1, ts, H * 192), lambda b, i: (b, i, 0)), tab, tab, vec, vec],
        out_specs=pl.BlockSpec((1, H, ts, 2 * LANE), lambda b, i: (b, 0, i, 0)),
        compiler_params=_params(("parallel", "parallel")),
    )(q_raw, cos4, sin4, q_norm[:MLA_NOPE].reshape(1, -1), jnp.concatenate([gr, gr]).reshape(1, -1))
    gkr = jnp.concatenate([k_norm[MLA_NOPE:], jnp.zeros((64,), F32)])
    k, v = pl.pallas_call(
        functools.partial(_mla_kvprep_kernel, H=H),
        out_shape=(jax.ShapeDtypeStruct((B, H, S, 2 * LANE), CDT),
                   jax.ShapeDtypeStruct((B, H, S, LANE), CDT)),
        grid=(B, S // ts),
        in_specs=[pl.BlockSpec((1, ts, H * 256), lambda b, i: (b, i, 0)),
                  pl.BlockSpec((1, ts, LANE), lambda b, i: (b, i, kr_block)), tab, tab, vec, vec],
        out_specs=(pl.BlockSpec((1, H, ts, 2 * LANE), lambda b, i: (b, 0, i, 0)),
                   pl.BlockSpec((1, H, ts, LANE), lambda b, i: (b, 0, i, 0))),
        compiler_params=_params(("parallel", "parallel")),
    )(kv_raw, p, cos4, sin4, k_norm[:MLA_NOPE].reshape(1, -1), gkr.reshape(1, -1))
    return _flash(q, k, v, mode="causal", tq=1024, tk=1024, rb=256)


def _nsa_prep_kernel(q_ref, kc_ref, vc_ref, ks_ref, vs_ref, kw_ref, vw_ref, cos_ref, sin_ref, g_ref,
                     qo, kso, vso, kwo, vwo, kco, vco, *, H, G, scale):
    cosf, sinf = cos_ref[0], sin_ref[0]
    for h in range(H):
        y = _rms(q_ref[0, :, h * LANE:(h + 1) * LANE], g_ref[0:1, :])
        qo[0, h] = (_rope128(y, cosf, sinf) * scale).astype(qo.dtype)
    for g in range(G):
        sl = slice(g * LANE, (g + 1) * LANE)
        kso[0, g] = _rope128(_rms(ks_ref[0, :, sl], g_ref[2:3, :]), cosf, sinf).astype(kso.dtype)
        kwo[0, g] = _rope128(_rms(kw_ref[0, :, sl], g_ref[3:4, :]), cosf, sinf).astype(kwo.dtype)
        vso[0, g] = vs_ref[0, :, sl].astype(vso.dtype)
        vwo[0, g] = vw_ref[0, :, sl].astype(vwo.dtype)
        kco[0, g] = kc_ref[0, :, sl].astype(kco.dtype)
        vco[0, g] = vc_ref[0, :, sl].astype(vco.dtype)


def _compress_kernel(xk_ref, xv_ref, wk_ref, wv_ref, pek_ref, pev_ref, g_ref, cos_ref, sin_ref,
                     kc_ref, vc_ref):
    nc = xk_ref.shape[2]

    def comp(x_ref, w_ref, pe_ref):
        y = jnp.dot(x_ref[0, 0], w_ref[...], preferred_element_type=F32)
        c = jnp.dot(pe_ref[...], w_ref[...], preferred_element_type=F32)
        const = c[0:1, 0:LANE] + c[1:2, LANE:2 * LANE]
        return y[:, 0:LANE] + pltpu.roll(y[:, LANE:2 * LANE], nc - 1, 0) + const

    kc = comp(xk_ref, wk_ref, pek_ref)
    kc_ref[0, 0] = _rope128(_rms(kc, g_ref[...]), cos_ref[0], sin_ref[0]).astype(kc_ref.dtype)
    vc_ref[0, 0] = comp(xv_ref, wv_ref, pev_ref).astype(vc_ref.dtype)


def _cmp_attn_kernel(q_ref, kc_ref, vc_ref, m_ref, o_ref, sel_ref, *, G, tq, n_cmp, n_sel, dv):
    i = pl.program_id(2)
    nc = kc_ref.shape[2]
    q = q_ref[0].reshape(G * tq, q_ref.shape[-1])
    s = lax.dot_general(q, kc_ref[0, 0], (((1,), (1,)), ((), ())), preferred_element_type=F32)
    s = s.reshape(G, tq, nc)
    t = i * tq + lax.broadcasted_iota(I32, (tq, nc), 0)
    n = lax.broadcasted_iota(I32, (tq, nc), 1)
    mask = (n * NSA_CMP_STRIDE + (NSA_CMP_LEN - 1) <= t) & (n < n_cmp)
    s = jnp.where(mask[None], s, NEG)
    mx = jnp.max(s, axis=-1, keepdims=True)
    e = jnp.where(mask[None], jnp.exp2(s - mx), 0.0)
    l = jnp.sum(e, axis=-1, keepdims=True)
    pc = e / jnp.where(l > 0.0, l, 1.0)
    o = jnp.dot(pc.reshape(G * tq, nc).astype(vc_ref.dtype), vc_ref[0, 0], preferred_element_type=F32)
    for g in range(G):
        o_ref[0, :, g * dv:(g + 1) * dv] = o[g * tq:(g + 1) * tq].astype(o_ref.dtype)

    ps = jnp.sum(pc, axis=0)
    hi = ps.astype(CDT)
    lo_part = (ps - hi.astype(F32)).astype(CDT)
    imp = (jnp.dot(hi, m_ref[...], preferred_element_type=F32)
           + jnp.dot(lo_part, m_ref[...], preferred_element_type=F32))
    blk = lax.broadcasted_iota(I32, (tq, LANE), 1)
    cur = (i * tq + lax.broadcasted_iota(I32, (tq, LANE), 0)) >> (NSA_SLC_LEN.bit_length() - 1)
    forced = (blk == 0) | (blk == cur) | (blk == cur - 1)
    imp = jnp.where(forced, FORCE, jnp.where(blk <= cur, imp, NEG))
    v = imp.T
    rowi = lax.broadcasted_iota(I32, (LANE, tq), 0)

    def take(_, carry):
        v, chosen = carry
        mval = jnp.max(v, axis=0, keepdims=True)
        first = jnp.min(jnp.where(v == mval, rowi, LANE), axis=0, keepdims=True)
        hit = rowi == first
        chosen = jnp.where(hit & (mval > NEG_HALF), 1.0, chosen)
        return jnp.where(hit, REMOVED, v), chosen

    _, chosen = lax.fori_loop(0, n_sel, take, (v, jnp.zeros((LANE, tq), F32)))
    sel_ref[0, 0] = chosen.T.astype(sel_ref.dtype)


def _nsa_combine_kernel(oc_ref, os_ref, ow_ref, g_ref, o_ref, *, H, dv):
    gate = _sigmoid(g_ref[0])
    for h in range(H):
        sl = slice(h * dv, (h + 1) * dv)
        o = (oc_ref[0, :, sl] * gate[:, h:h + 1] + os_ref[0, :, sl] * gate[:, H + h:H + h + 1]
             + ow_ref[0, :, sl] * gate[:, 2 * H + h:2 * H + h + 1])
        o_ref[0, :, sl] = o.astype(o_ref.dtype)


def _nsa(p, q_block, kv_block0, gate_block, pos, qk_norm, cmp_pos, cmp_w):
    B, S, _ = p.shape
    H, G, DH = NSA_HEADS, NSA_KV_GROUPS, NSA_HEAD_DIM
    HPG = H // G
    scale = DH ** -0.5 * LOG2E
    ts = _tile(S, 256, 8)
    cosf, sinf = _tables128(pos)
    tab = pl.BlockSpec((1, ts, LANE), lambda b, i: (b, i, 0))
    kvspec = [pl.BlockSpec((1, ts, G * DH), functools.partial(lambda b, i, m: (b, i, kv_block0 + m), m=m))
              for m in range(6)]
    head_out = lambda n: pl.BlockSpec((1, n, ts, DH), lambda b, i: (b, 0, i, 0))
    kv_shape = jax.ShapeDtypeStruct((B, G, S, DH), CDT)
    q, ks, vs, kw, vw, kcr, vcr = pl.pallas_call(
        functools.partial(_nsa_prep_kernel, H=H, G=G, scale=scale),
        out_shape=(jax.ShapeDtypeStruct((B, H, S, DH), CDT),) + (kv_shape,) * 6,
        grid=(B, S // ts),
        in_specs=[pl.BlockSpec((1, ts, H * DH), lambda b, i: (b, i, q_block))] + kvspec
                 + [tab, tab, pl.BlockSpec((4, DH), lambda b, i: (0, 0))],
        out_specs=(head_out(H),) + (head_out(G),) * 6,
        compiler_params=_params(("parallel", "parallel")),
    )(p, p, p, p, p, p, p, cosf, sinf, qk_norm)

    half = NSA_CMP_LEN // 2
    nc = S // NSA_CMP_STRIDE
    n_cmp = (S - NSA_CMP_LEN) // NSA_CMP_STRIDE + 1
    cmp_end = jnp.minimum(jnp.arange(nc) * NSA_CMP_STRIDE + NSA_CMP_LEN - 1, S - 1)
    ccos, csin = _tables128(pos[:, cmp_end])
    wcat = lambda w: jnp.concatenate([w[:half].reshape(half * DH, DH), w[half:].reshape(half * DH, DH)], 1).astype(CDT)
    pecat = lambda pe: jnp.zeros((8, half * DH), F32).at[0].set(pe[:half].reshape(-1)).at[1].set(
        pe[half:].reshape(-1)).astype(CDT)
    xspec = pl.BlockSpec((1, 1, nc, half * DH), lambda b, g: (b, g, 0, 0))
    wspec = pl.BlockSpec((half * DH, 2 * DH), lambda b, g: (0, 0))
    pespec = pl.BlockSpec((8, half * DH), lambda b, g: (0, 0))
    cspec = pl.BlockSpec((1, 1, nc, DH), lambda b, g: (b, g, 0, 0))
    ctab = pl.BlockSpec((1, nc, DH), lambda b, g: (b, 0, 0))
    kc, vc = pl.pallas_call(
        _compress_kernel,
        out_shape=(jax.ShapeDtypeStruct((B, G, nc, DH), CDT),) * 2,
        grid=(B, G),
        in_specs=[xspec, xspec, wspec, wspec, pespec, pespec,
                  pl.BlockSpec((1, DH), lambda b, g: (0, 0)), ctab, ctab],
        out_specs=(cspec, cspec),
        compiler_params=_params(("parallel", "parallel")),
    )(kcr.reshape(B, G, nc, half * DH), vcr.reshape(B, G, nc, half * DH), wcat(cmp_w[0]), wcat(cmp_w[1]),
      pecat(cmp_pos[0]), pecat(cmp_pos[1]), qk_norm[1].reshape(1, DH), ccos, csin)

    n_slc = S // NSA_SLC_LEN
    assert n_slc <= LANE
    r, cl = NSA_SLC_LEN // NSA_CMP_STRIDE, NSA_CMP_LEN // NSA_CMP_STRIDE
    m_np = np.zeros((nc, LANE), np.float32)
    for j in range(n_slc):
        for a in range(r):
            for c in range(cl):
                ci = j * r + a + c - (cl - 1)
                if 0 <= ci < n_cmp:
                    m_np[ci, j] += 1.0
    tq = _tile(S, 256, 8)
    o_c, sel = pl.pallas_call(
        functools.partial(_cmp_attn_kernel, G=HPG, tq=tq, n_cmp=n_cmp, n_sel=min(NSA_N_SEL, n_slc), dv=DH),
        out_shape=(jax.ShapeDtypeStruct((B, S, H * DH), F32),
                   jax.ShapeDtypeStruct((B, G, S, LANE), CDT)),
        grid=(B, G, S // tq),
        in_specs=[pl.BlockSpec((1, HPG, tq, DH), lambda b, g, i: (b, g, i, 0)),
                  pl.BlockSpec((1, 1, nc, DH), lambda b, g, i: (b, g, 0, 0)),
                  pl.BlockSpec((1, 1, nc, DH), lambda b, g, i: (b, g, 0, 0)),
                  pl.BlockSpec((nc, LANE), lambda b, g, i: (0, 0))],
        out_specs=(pl.BlockSpec((1, tq, HPG * DH), lambda b, g, i: (b, i, g)),
                   pl.BlockSpec((1, 1, tq, LANE), lambda b, g, i: (b, g, i, 0))),
        compiler_params=_params(("parallel", "parallel", "parallel")),
    )(q, kc, vc, jnp.asarray(m_np, CDT))

    expand = jnp.asarray((np.arange(S)[None, :] // NSA_SLC_LEN) == np.arange(LANE)[:, None], CDT)
    o_s = _flash(q, ks, vs, mode="sel", sel=sel, expand=expand, tq=256, tk=1024, out_dtype=F32)
    o_w = _flash(q, kw, vw, mode="window", window=NSA_WINDOW, tq=512, tk=512, out_dtype=F32)
    ospec = pl.BlockSpec((1, ts, H * DH), lambda b, i: (b, i, 0))
    return pl.pallas_call(
        functools.partial(_nsa_combine_kernel, H=H, dv=DH),
        out_shape=jax.ShapeDtypeStruct((B, S, H * DH), CDT),
        grid=(B, S // ts),
        in_specs=[ospec, ospec, ospec, pl.BlockSpec((1, ts, LANE), lambda b, i: (b, i, gate_block))],
        out_specs=ospec,
        compiler_params=_params(("parallel", "parallel")),
    )(o_c, o_s, o_w, p)


def _dsa_prep_kernel(q_ref, k_ref, v_ref, qi_ref, ki_ref, wi_ref, cos_ref, sin_ref, cos4_ref, sin4_ref,
                     g_ref, gi_ref, qo, ko, vo, qio, kilo, kihi, wio, *, H, HKV, scale, wscale):
    ts = q_ref.shape[1]
    lane = lax.broadcasted_iota(I32, (ts, LANE), 1)
    cosf, sinf, cos4, sin4 = cos_ref[0], sin_ref[0], cos4_ref[0], sin4_ref[0]
    for h in range(H):
        y = _rms(q_ref[0, :, h * LANE:(h + 1) * LANE], g_ref[0:1, :])
        qo[0, h] = (_rope128(y, cosf, sinf) * scale).astype(qo.dtype)
    for h in range(HKV):
        sl = slice(h * LANE, (h + 1) * LANE)
        ko[0, h] = _rope128(_rms(k_ref[0, :, sl], g_ref[1:2, :]), cosf, sinf).astype(ko.dtype)
        vo[0, h] = v_ref[0, :, sl].astype(vo.dtype)
    for j in range(qi_ref.shape[2] // LANE):
        qio[0, j] = _rope64pair(qi_ref[0, :, j * LANE:(j + 1) * LANE], cos4, sin4, lane).astype(qio.dtype)
    ki = ki_ref[0]
    inv = lax.rsqrt(jnp.sum(ki * ki, axis=-1, keepdims=True) / 64.0 + EPS)
    r = _rope64pair(ki * inv * gi_ref[...], cos4, sin4, lane)
    kilo[0] = r.astype(kilo.dtype)
    kihi[0] = pltpu.roll(r, 64, 1).astype(kihi.dtype)
    wio[0] = wi_ref[0] * wscale


def _indexer_kernel(qi_ref, kj_ref, fl_ref, q_ref, klo_ref, khi_ref, w_ref, o_ref, wb_sc, sc_sc, acc_sc,
                    *, tq, tk, topk, n_pairs, rg):
    p = pl.program_id(1)
    qi, kj, fl = qi_ref[p], kj_ref[p], fl_ref[p]
    n_tiles = o_ref.shape[1]

    @pl.when((fl & 1) != 0)
    def _():
        w = w_ref[0]
        for h in range(2 * n_pairs):
            wb_sc[h] = jnp.broadcast_to(w[:, h:h + 1], (tq, LANE))

    acc_sc[...] = jnp.zeros((tq, tk), F32)
    cw = min(tk, 2 * LANE)

    def pair(j, _):
        q = q_ref[0, j]
        wa = jnp.tile(wb_sc[2 * j], (1, cw // LANE))
        wb = jnp.tile(wb_sc[2 * j + 1], (1, cw // LANE))
        for c in range(tk // cw):
            cols = slice(c * cw, (c + 1) * cw)
            sa = lax.dot_general(q, klo_ref[0, cols, :], (((1,), (1,)), ((), ())), preferred_element_type=F32)
            sb = lax.dot_general(q, khi_ref[0, cols, :], (((1,), (1,)), ((), ())), preferred_element_type=F32)
            acc_sc[:, cols] += wa * jnp.maximum(sa, 0.0) + wb * jnp.maximum(sb, 0.0)
        return 0

    lax.fori_loop(0, n_pairs, pair, 0)
    score = acc_sc[...]
    row = qi * tq + lax.broadcasted_iota(I32, (tq, tk), 0)
    col = kj * tk + lax.broadcasted_iota(I32, (tq, tk), 1)
    score = jnp.where(col <= row, score, NEG)
    bits = pltpu.bitcast(score, I32)
    key = bits ^ ((bits >> 31) & 0x7FFFFFFF)
    sc_sc[kj] = key

    @pl.when((fl & 2) != 0)
    def _():
        n_chunks = kj + 1
        nh_bits = int(np.float32(NEG_HALF).view(np.int32))
        key_neg_half = nh_bits ^ 0x7FFFFFFF if nh_bits < 0 else nh_bits
        for g in range(tq // rg):
            rows = pl.ds(g * rg, rg)

            def bit_step(b, thr):
                cand = thr + jnp.left_shift(jnp.int32(1), 31 - b)

                def count(c, cnt):
                    blk = sc_sc[c, rows, :]
                    for u in range(tk // LANE):
                        cnt = cnt + (blk[:, u * LANE:(u + 1) * LANE] >= cand).astype(I32)
                    return cnt

                cnt = lax.fori_loop(0, n_chunks, count, jnp.zeros((rg, LANE), I32))
                tot = jnp.sum(cnt, axis=1, keepdims=True)
                return jnp.where(tot >= topk, cand, thr)

            thr = lax.fori_loop(0, 32, bit_step, jnp.full((rg, LANE), -2**31, I32))
            thr = jnp.maximum(thr, key_neg_half + 1)
            thr_t = jnp.tile(thr, (1, tk // LANE))

            def emit(c, _):
                o_ref[0, c, rows, :] = jnp.where(sc_sc[c, rows, :] >= thr_t, 0.0, NEG).astype(o_ref.dtype)
                return 0

            def emit_masked(c, _):
                o_ref[0, c, rows, :] = jnp.full((rg, tk), NEG, o_ref.dtype)
                return 0

            lax.fori_loop(0, n_chunks, emit, 0)
            lax.fori_loop(n_chunks, n_tiles, emit_masked, 0)


def _dsa(p, pos, qk_norm, idx_k_norm):
    B, S, _ = p.shape
    H, HKV, DH = DSA_HEADS, DSA_KV_HEADS, DSA_HEAD_DIM
    NP = IDX_HEADS // 2
    ts = _tile(S, 256, 8)
    cosf, sinf = _tables128(pos)
    cos4, sin4 = _tables64pair(pos)
    tab = pl.BlockSpec((1, ts, LANE), lambda b, i: (b, i, 0))
    kw = HKV * DH
    gi = jnp.concatenate([idx_k_norm, jnp.zeros((LANE - IDX_DIM,), F32)]).reshape(1, LANE)
    head_out = lambda n: pl.BlockSpec((1, n, ts, DH), lambda b, i: (b, 0, i, 0))
    q, k, v, qidx, kilo, kihi, wi = pl.pallas_call(
        functools.partial(_dsa_prep_kernel, H=H, HKV=HKV, scale=DH ** -0.5 * LOG2E,
                          wscale=IDX_HEADS ** -0.5 * IDX_DIM ** -0.5),
        out_shape=(jax.ShapeDtypeStruct((B, H, S, DH), CDT), jax.ShapeDtypeStruct((B, HKV, S, DH), CDT),
                   jax.ShapeDtypeStruct((B, HKV, S, DH), CDT), jax.ShapeDtypeStruct((B, NP, S, LANE), CDT),
                   jax.ShapeDtypeStruct((B, S, LANE), CDT), jax.ShapeDtypeStruct((B, S, LANE), CDT),
                   jax.ShapeDtypeStruct((B, S, LANE), F32)),
        grid=(B, S // ts),
        in_specs=[pl.BlockSpec((1, ts, H * DH), lambda b, i: (b, i, 0)),
                  pl.BlockSpec((1, ts, kw), lambda b, i: (b, i, H * DH // kw)),
                  pl.BlockSpec((1, ts, kw), lambda b, i: (b, i, H * DH // kw + 1)),
                  pl.BlockSpec((1, ts, NP * LANE), lambda b, i: (b, i, (H * DH + 2 * kw) // (NP * LANE))),
                  pl.BlockSpec((1, ts, LANE), lambda b, i: (b, i, (H * DH + 2 * kw + NP * LANE) // LANE)),
                  pl.BlockSpec((1, ts, LANE), lambda b, i: (b, i, (H * DH + 2 * kw + NP * LANE) // LANE + 1)),
                  tab, tab, tab, tab,
                  pl.BlockSpec((2, DH), lambda b, i: (0, 0)), pl.BlockSpec((1, LANE), lambda b, i: (0, 0))],
        out_specs=(head_out(H), head_out(HKV), head_out(HKV), head_out(NP), tab, tab, tab),
        compiler_params=_params(("parallel", "parallel")),
    )(p, p, p, p, p, p, cosf, sinf, cos4, sin4, qk_norm, gi)

    topk = min(DSA_TOPK_MAX, S // 4)
    tq, tk = _tile(S, 256, 8), _tile(S, 1024)
    qi_t, kj_t, fl_t = _pairs(S, tq, tk)
    bias = pl.pallas_call(
        functools.partial(_indexer_kernel, tq=tq, tk=tk, topk=topk, n_pairs=NP, rg=min(64, tq)),
        out_shape=jax.ShapeDtypeStruct((B, S // tk, S, tk), CDT),
        grid_spec=pltpu.PrefetchScalarGridSpec(
            num_scalar_prefetch=3,
            grid=(B, int(qi_t.shape[0])),
            in_specs=[pl.BlockSpec((1, NP, tq, LANE), lambda b, p, qi, kj, fl: (b, 0, qi[p], 0)),
                      pl.BlockSpec((1, tk, LANE), lambda b, p, qi, kj, fl: (b, kj[p], 0)),
                      pl.BlockSpec((1, tk, LANE), lambda b, p, qi, kj, fl: (b, kj[p], 0)),
                      pl.BlockSpec((1, tq, LANE), lambda b, p, qi, kj, fl: (b, qi[p], 0))],
            out_specs=pl.BlockSpec((1, S // tk, tq, tk), lambda b, p, qi, kj, fl: (b, 0, qi[p], 0)),
            scratch_shapes=[pltpu.VMEM((2 * NP, tq, LANE), F32), pltpu.VMEM((S // tk, tq, tk), I32),
                            pltpu.VMEM((tq, tk), F32)]),
        compiler_params=_params(("parallel", "arbitrary")),
    )(qi_t, kj_t, fl_t, qidx, kilo, kihi, wi)
    return _flash(q, k, v, mode="bias", bias=bias, tq=2 * tq, tk=tk)


def _router_kernel(l_ref, i_ref, p_ref, *, n_experts):
    x = l_ref[...]
    lane = lax.broadcasted_iota(I32, x.shape, 1)
    x = jnp.where(lane < n_experts, x, -jnp.inf)
    m1 = jnp.max(x, axis=1, keepdims=True)
    i1 = jnp.min(jnp.where(x == m1, lane, LANE), axis=1, keepdims=True)
    x2 = jnp.where(lane == i1, -jnp.inf, x)
    m2 = jnp.max(x2, axis=1, keepdims=True)
    i2 = jnp.min(jnp.where(x2 == m2, lane, LANE), axis=1, keepdims=True)
    e2 = jnp.exp(m2 - m1)
    p1 = 1.0 / (1.0 + e2)
    p2 = e2 / (1.0 + e2)
    i_ref[...] = jnp.where(lane == 0, i1, jnp.where(lane == 1, i2, 0))
    p_ref[...] = jnp.where(lane == 0, p1, jnp.where(lane == 1, p2, 0.0))


def _row_copy(src_hbm, row, dst_vmem, r, sem):
    return pltpu.make_async_copy(src_hbm.at[pl.ds(row, 1), :], dst_vmem.at[pl.ds(r, 1), :], sem)


def _dispatch_kernel(tok_ref, h_hbm, o_ref, buf, sem, *, tm):
    base = pl.program_id(0) * tm

    def start(r, _):
        _row_copy(h_hbm, tok_ref[base + r], buf, r, sem).start()
        return 0

    def wait(r, _):
        _row_copy(h_hbm, 0, buf, r, sem).wait()
        return 0

    lax.fori_loop(0, tm, start, 0)
    lax.fori_loop(0, tm, wait, 0)
    o_ref[...] = buf[...].astype(o_ref.dtype)


def _combine_kernel(s1_ref, s2_ref, y_hbm, x_ref, g_ref, o_ref, buf1, buf2, sem, *, tm):
    base = pl.program_id(0) * tm

    def start(r, _):
        _row_copy(y_hbm, s1_ref[base + r], buf1, r, sem).start()
        _row_copy(y_hbm, s2_ref[base + r], buf2, r, sem).start()
        return 0

    def wait(r, _):
        _row_copy(y_hbm, 0, buf1, r, sem).wait()
        _row_copy(y_hbm, 0, buf2, r, sem).wait()
        return 0

    lax.fori_loop(0, tm, start, 0)
    lax.fori_loop(0, tm, wait, 0)
    o_ref[...] = x_ref[...] + g_ref[0] * (buf1[...] + buf2[...])


def _moe(h, h32, x, g_f, w_router, w_gate, w_up, w_down, S):
    N, D = h.shape
    E, _, DE = w_gate.shape
    wr = jnp.zeros((D, LANE), F32).at[:, :E].set(w_router).astype(CDT)
    logits = _mm(h, wr, tn=LANE, tk=D)
    tr = _tile(N, 1024, 8)
    spec = pl.BlockSpec((tr, LANE), lambda i: (i, 0))
    idx, prob = pl.pallas_call(
        functools.partial(_router_kernel, n_experts=E),
        out_shape=(jax.ShapeDtypeStruct((N, LANE), I32), jax.ShapeDtypeStruct((N, LANE), F32)),
        grid=(N // tr,),
        in_specs=[spec],
        out_specs=(spec, spec),
        compiler_params=_params(("parallel",)),
    )(logits)

    tm = min(512, N)
    n_rows = 2 * N + E * tm
    e_flat = jnp.concatenate([idx[:, 0], idx[:, 1]])
    onehot = (e_flat[:, None] == jnp.arange(E)[None, :]).astype(I32)
    csum = jnp.cumsum(onehot, axis=0)
    rank = jnp.take_along_axis(csum, e_flat[:, None], axis=1)[:, 0] - 1
    padded = (csum[-1] + tm - 1) // tm * tm
    ends = jnp.cumsum(padded)
    pos = (ends - padded)[e_flat] + rank
    tile_expert = jnp.minimum(jnp.sum(jnp.arange(n_rows // tm)[:, None] * tm >= ends[None, :], axis=1), E - 1)
    tok = jnp.tile(jnp.arange(N, dtype=I32), 2)
    row_token = jnp.zeros((n_rows,), I32).at[pos].set(tok)
    row_prob = jnp.zeros((n_rows,), F32).at[pos].set(jnp.concatenate([prob[:, 0], prob[:, 1]]))

    tg = min(256, N)
    xs = pl.pallas_call(
        functools.partial(_dispatch_kernel, tm=tg),
        out_shape=jax.ShapeDtypeStruct((n_rows, D), CDT),
        grid_spec=pltpu.PrefetchScalarGridSpec(
            num_scalar_prefetch=1,
            grid=(n_rows // tg,),
            in_specs=[pl.BlockSpec(memory_space=pl.ANY)],
            out_specs=pl.BlockSpec((tg, D), lambda i, tok: (i, 0)),
            scratch_shapes=[pltpu.VMEM((tg, D), F32), pltpu.SemaphoreType.DMA(())]),
        compiler_params=_params(("arbitrary",)),
    )(row_token, h32)
    te = tile_expert.astype(I32)
    hid = _mm(xs, w_gate.astype(CDT), mode="swiglu", b2=w_up.astype(CDT), group=te, out_dtype=CDT,
              tm=tm, tn=DE, tk=1024)
    ys = _mm(hid, w_down.astype(CDT), mode="rowscale", rowscale=row_prob[:, None], group=te,
             tm=tm, tn=1024, tk=DE)
    return pl.pallas_call(
        functools.partial(_combine_kernel, tm=tg),
        out_shape=jax.ShapeDtypeStruct((N, D), F32),
        grid_spec=pltpu.PrefetchScalarGridSpec(
            num_scalar_prefetch=2,
            grid=(N // tg,),
            in_specs=[pl.BlockSpec(memory_space=pl.ANY),
                      pl.BlockSpec((tg, D), lambda i, s1, s2: (i, 0)),
                      pl.BlockSpec((1, 1, D), lambda i, s1, s2: ((i * tg) // S, 0, 0))],
            out_specs=pl.BlockSpec((tg, D), lambda i, s1, s2: (i, 0)),
            scratch_shapes=[pltpu.VMEM((tg, D), F32), pltpu.VMEM((tg, D), F32),
                            pltpu.SemaphoreType.DMA(())]),
        compiler_params=_params(("arbitrary",)),
    )(pos[:N].astype(I32), pos[N:].astype(I32), ys, x, g_f)


def _pad_cols(blocks, total):
    cols = []
    for w, width in blocks:
        cols.append(w)
        if width > w.shape[1]:
            cols.append(jnp.zeros((w.shape[0], width - w.shape[1]), w.dtype))
    out = jnp.concatenate(cols, axis=1)
    if total > out.shape[1]:
        out = jnp.concatenate([out, jnp.zeros((out.shape[0], total - out.shape[1]), out.dtype)], axis=1)
    return out.astype(CDT)


def _round_up(n, m):
    return (n + m - 1) // m * m


def kernel(x, c, positions, ada_w, ada_b, ada_table, norm_g, ev_w_in, ev_w_out, mla_q_a_norm, mla_kv_a_norm, mla_w_uq, mla_w_ukv, mla_q_norm, mla_k_norm, nsa_qk_norm, nsa_cmp_pos, nsa_cmp_w, ffn_w_gate, ffn_w_up, ffn_w_down, od_w_in, od_w_out, dsa_qk_norm, idx_k_norm, moe_router, moe_w_gate, moe_w_up, moe_w_down):
    B, S, D = x.shape
    N = B * S
    depth = ada_table.shape[0]
    cond = _cond(c, ada_w, ada_b).reshape(B, 6, D)

    mla_in = MLA_Q_RANK + MLA_KV_RANK + MLA_ROPE
    nq = NSA_HEADS * NSA_HEAD_DIM
    nkv = 6 * NSA_KV_GROUPS * NSA_HEAD_DIM
    hn = np.arange(NSA_HEADS)
    gate_perm = np.concatenate([hn * 3 + r for r in range(3)])

    x2 = x.reshape(N, D)
    for l in range(depth):
        i = l // 2
        mod = cond + ada_table[l]
        sh_a, sc_a, g_a, sh_f, sc_f, g_f = [mod[:, j, None, :] for j in range(6)]
        h = _norm(x2.reshape(B, S, D), norm_g[l, 0], sc_a, sh_a).reshape(N, D)
        if l % 2 == 0:
            w = ev_w_in[i]
            nsa = w[:, mla_in:]
            blocks = [(w[:, :MLA_Q_RANK + MLA_KV_RANK], MLA_Q_RANK + MLA_KV_RANK),
                      (nsa[:, :nq + nkv], nq + nkv),
                      (w[:, MLA_Q_RANK + MLA_KV_RANK:mla_in], LANE),
                      (nsa[:, nq + nkv:][:, gate_perm], LANE)]
            width = MLA_Q_RANK + MLA_KV_RANK + nq + nkv + 2 * LANE
            w_in = _pad_cols(blocks, _round_up(width, 512))
            p = _mm(h, w_in).reshape(B, S, -1)
            off = MLA_Q_RANK + MLA_KV_RANK
            a_out = _mla(p, (off + nq + nkv) // LANE, positions, mla_q_a_norm[i], mla_kv_a_norm[i],
                         mla_w_uq[i], mla_w_ukv[i], mla_q_norm[i], mla_k_norm[i])
            b_out = _nsa(p, off // nq, (off + nq) // (NSA_KV_GROUPS * NSA_HEAD_DIM),
                         (off + nq + nkv) // LANE + 1, positions, nsa_qk_norm[i], nsa_cmp_pos[i], nsa_cmp_w[i])
            mix = jnp.concatenate([a_out, b_out], axis=-1).reshape(N, -1)
            w_out = ev_w_out[i]
        else:
            w = od_w_in[i]
            main = DSA_HEADS * DSA_HEAD_DIM + 2 * DSA_KV_HEADS * DSA_HEAD_DIM + IDX_HEADS * IDX_DIM
            blocks = [(w[:, :main], main), (w[:, main:main + IDX_DIM], LANE), (w[:, main + IDX_DIM:], LANE)]
            w_in = _pad_cols(blocks, _round_up(main + 2 * LANE, 512))
            p = _mm(h, w_in).reshape(B, S, -1)
            mix = _dsa(p, positions, dsa_qk_norm[i], idx_k_norm[i]).reshape(N, -1)
            w_out = od_w_out[i]
        x2 = _mm(mix, w_out.astype(CDT), mode="res", x=x2, g=g_a, rows_per_batch=S)
        if l % 2 == 0:
            h = _norm(x2.reshape(B, S, D), norm_g[l, 1], sc_f, sh_f).reshape(N, D)
            hid = _mm(h, ffn_w_gate[i].astype(CDT), mode="swiglu", b2=ffn_w_up[i].astype(CDT), out_dtype=CDT)
            x2 = _mm(hid, ffn_w_down[i].astype(CDT), mode="res", x=x2, g=g_f, tn=1024, tk=2048,
                     rows_per_batch=S)
        else:
            h, h32 = _norm(x2.reshape(B, S, D), norm_g[l, 1], sc_f, sh_f, also_f32=True)
            x2 = _moe(h.reshape(N, D), h32.reshape(N, D), x2, g_f, moe_router[i], moe_w_gate[i], moe_w_up[i],
                      moe_w_down[i], S)
    return x2.reshape(B, S, D)
```

```python
import functools

import numpy as np
import jax
import jax.numpy as jnp
from jax import lax
from jax.experimental import pallas as pl
from jax.experimental.pallas import tpu as pltpu

F32 = jnp.float32
I32 = jnp.int32
CDT = jnp.bfloat16

ROPE_THETA = 10000.0
EPS = 1e-6
NEG = -1e30
NEG_HALF = -5e29
FORCE = 1e9
REMOVED = -3e38
LOG2E = 1.4426950408889634

MLA_HEADS, MLA_Q_RANK, MLA_KV_RANK, MLA_NOPE, MLA_ROPE, MLA_V = 16, 1536, 512, 128, 64, 128
NSA_HEADS, NSA_KV_GROUPS, NSA_HEAD_DIM = 16, 4, 128
NSA_CMP_LEN, NSA_CMP_STRIDE, NSA_SLC_LEN, NSA_N_SEL, NSA_WINDOW = 32, 16, 64, 16, 512
DSA_HEADS, DSA_KV_HEADS, DSA_HEAD_DIM, IDX_HEADS, IDX_DIM, DSA_TOPK_MAX = 32, 8, 128, 32, 64, 256
N_EXPERTS = 8

LANE = 128
VMEM_LIMIT = 56 * 2**20


def _tile(n, pref, mult=LANE):
    if n <= pref:
        return n
    t = (pref // mult) * mult
    while t >= mult:
        if n % t == 0:
            return t
        t -= mult
    return n


def _params(sem):
    return pltpu.CompilerParams(dimension_semantics=sem, vmem_limit_bytes=VMEM_LIMIT)


def _silu(x):
    return x / (1.0 + jnp.exp(-x))


def _sigmoid(x):
    return 1.0 / (1.0 + jnp.exp(-x))


def _rms(x, g):
    return x * lax.rsqrt(jnp.mean(x * x, axis=-1, keepdims=True) + EPS) * g


def _rope128(y, cosf, sinf):
    return y * cosf + pltpu.roll(y, 64, 1) * sinf


def _rope64pair(y, cos4, sin4, lane):
    rot = jnp.where((lane & 63) < 32, pltpu.roll(y, 96, 1), pltpu.roll(y, 32, 1))
    return y * cos4 + rot * sin4


def _mm_kernel(*refs, nk, mode, grouped, split_a):
    if grouped:
        refs = refs[1:]
    k = pl.program_id(2)
    a_ref = refs[0]
    a = a_ref[...]
    if split_a:
        a = jnp.where(k == 0, a, refs[1][...])
        refs = refs[1:]
    n_in = {"plain": 2, "res": 4, "swiglu": 3}[mode]
    o_ref = refs[n_in]
    acc, acc2 = (tuple(refs[n_in + 1:]) + (None, None))[:2]
    if mode == "plain":
        b_ref, = refs[1:n_in]
    elif mode == "res":
        b_ref, x_ref, g_ref = refs[1:n_in]
    else:
        b_ref, b2_ref = refs[1:n_in]

    part = jnp.dot(a, b_ref[...].astype(a.dtype), preferred_element_type=F32)
    if mode == "swiglu":
        part2 = jnp.dot(a, b2_ref[...].astype(a.dtype), preferred_element_type=F32)

    if nk > 1:
        @pl.when(k == 0)
        def _():
            acc[...] = part
            if mode == "swiglu":
                acc2[...] = part2

        @pl.when(k > 0)
        def _():
            acc[...] += part
            if mode == "swiglu":
                acc2[...] += part2

    def finish():
        r = acc[...] if nk > 1 else part
        if mode == "plain":
            o_ref[...] = r.astype(o_ref.dtype)
        elif mode == "res":
            o_ref[...] = (x_ref[...] + g_ref[0] * r).astype(o_ref.dtype)
        else:
            r2 = acc2[...] if nk > 1 else part2
            o_ref[...] = (_silu(r) * r2).astype(o_ref.dtype)

    if nk > 1:
        pl.when(k == nk - 1)(finish)
    else:
        finish()


def _mm(a, b, *, mode="plain", b2=None, x=None, g=None, group=None, out_dtype=F32,
        tm=1024, tn=512, tk=4096, rows_per_batch=None):
    split_a = isinstance(a, tuple)
    a_list = list(a) if split_a else [a]
    M, Ka = a_list[0].shape
    K = Ka * len(a_list)
    N = b.shape[-1]
    tm, tn, tk = _tile(M, tm, 8), _tile(N, tn), (Ka if split_a else _tile(K, tk))
    nk = K // tk
    grid = (M // tm, N // tn, nk)
    grouped = group is not None
    if grouped:
        assert group.shape == (M // tm,)
        b_spec = pl.BlockSpec((None, tk, tn), lambda i, j, k, ge: (ge[i], k, j))
    else:
        b_spec = pl.BlockSpec((tk, tn), lambda i, j, k, *_: (k, j))
    if split_a:
        in_specs = [pl.BlockSpec((tm, tk), lambda i, j, k, *_: (i, 0))] * 2 + [b_spec]
    else:
        in_specs = [pl.BlockSpec((tm, tk), lambda i, j, k, *_: (i, k)), b_spec]
    args = a_list + [b]
    scratch = [pltpu.VMEM((tm, tn), F32)] if nk > 1 else []
    if mode == "swiglu":
        in_specs.append(b_spec)
        args.append(b2)
        scratch = scratch * 2
    if mode == "res":
        rpb = rows_per_batch
        assert rpb % tm == 0
        in_specs += [pl.BlockSpec((tm, tn), lambda i, j, k, *_: (i, j)),
                     pl.BlockSpec((1, 1, tn), lambda i, j, k, *_: ((i * tm) // rpb, 0, j))]
        args += [x, g]
    return pl.pallas_call(
        functools.partial(_mm_kernel, nk=nk, mode=mode, grouped=grouped, split_a=split_a),
        out_shape=jax.ShapeDtypeStruct((M, N), out_dtype),
        grid_spec=pltpu.PrefetchScalarGridSpec(
            num_scalar_prefetch=1 if grouped else 0,
            grid=grid,
            in_specs=in_specs,
            out_specs=pl.BlockSpec((tm, tn), lambda i, j, k, *_: (i, j)),
            scratch_shapes=scratch),
        compiler_params=_params(("parallel", "parallel", "arbitrary")),
    )(*(([group] if grouped else []) + args))


def _cond_kernel(c_ref, w_ref, b_ref, o_ref):
    a = _silu(c_ref[...]).astype(CDT)
    o_ref[...] = jnp.dot(a, w_ref[...].astype(CDT), preferred_element_type=F32) + b_ref[...]


def _cond(c, ada_w, ada_b):
    B, D = c.shape
    N = ada_w.shape[1]
    cp = jnp.zeros((8, D), F32).at[:B].set(c)
    tn = _tile(N, 512)
    out = pl.pallas_call(
        _cond_kernel,
        out_shape=jax.ShapeDtypeStruct((8, N), F32),
        grid=(N // tn,),
        in_specs=[pl.BlockSpec((8, D), lambda j: (0, 0)),
                  pl.BlockSpec((D, tn), lambda j: (0, j)),
                  pl.BlockSpec((1, tn), lambda j: (0, j))],
        out_specs=pl.BlockSpec((8, tn), lambda j: (0, j)),
        compiler_params=_params(("parallel",)),
    )(cp, ada_w, ada_b.reshape(1, N))
    return out[:B]


def _norm_kernel(*refs, modulate, n_out):
    outs = refs[len(refs) - n_out:]
    if modulate:
        x_ref, g_ref, sc_ref, sh_ref = refs[:4]
    else:
        x_ref, g_ref = refs[:2]
    y = _rms(x_ref[0], g_ref[...])
    if modulate:
        y = y * (1.0 + sc_ref[0]) + sh_ref[0]
    for o_ref in outs:
        o_ref[0] = y.astype(o_ref.dtype)


def _norm(x, g, sc=None, sh=None, *, width=None, col_block=0, ts=256, also_f32=False):
    B, S, W = x.shape
    width = W if width is None else width
    ts = _tile(S, ts, 8)
    modulate = sc is not None
    in_specs = [pl.BlockSpec((1, ts, width), lambda b, i: (b, i, col_block)),
                pl.BlockSpec((1, width), lambda b, i: (0, 0))]
    args = [x, g.reshape(1, width)]
    if modulate:
        in_specs += [pl.BlockSpec((1, 1, width), lambda b, i: (b, 0, 0))] * 2
        args += [sc, sh]
    dtypes = (CDT, F32) if also_f32 else (CDT,)
    out = pl.pallas_call(
        functools.partial(_norm_kernel, modulate=modulate, n_out=len(dtypes)),
        out_shape=tuple(jax.ShapeDtypeStruct((B, S, width), d) for d in dtypes),
        grid=(B, S // ts),
        in_specs=in_specs,
        out_specs=tuple(pl.BlockSpec((1, ts, width), lambda b, i: (b, i, 0)) for _ in dtypes),
        compiler_params=_params(("parallel", "parallel")),
    )(*args)
    return out if also_f32 else out[0]


def _rope_angles(pos, dim):
    inv = ROPE_THETA ** (-jnp.arange(0, dim, 2, dtype=F32) / dim)
    ang = pos.astype(F32)[..., None] * inv
    return jnp.cos(ang), jnp.sin(ang)


def _tables128(pos):
    c, s = _rope_angles(pos, 128)
    return jnp.concatenate([c, c], -1), jnp.concatenate([-s, s], -1)


def _tables64pair(pos):
    c, s = _rope_angles(pos, 64)
    return jnp.concatenate([c, c, c, c], -1), jnp.concatenate([-s, s, -s, s], -1)


def _pairs(S, tq, tk, window=None):
    qi, kj, fl = [], [], []
    for i in range(S // tq):
        lo = 0 if window is None else max(0, i * tq - window + 1)
        js = list(range(lo // tk, (i * tq + tq - 1) // tk + 1))
        for n, j in enumerate(js):
            diag = (j + 1) * tk - 1 > i * tq
            f = (1 if n == 0 else 0) | (2 if n == len(js) - 1 else 0)
            f |= 4 if (diag or window is not None) else 0
            qi.append(i), kj.append(j), fl.append(f)
    return (jnp.asarray(qi, I32), jnp.asarray(kj, I32), jnp.asarray(fl, I32))


def _flash_kernel(qi_ref, kj_ref, fl_ref, q_ref, k_ref, v_ref, *rest, G, tq, tk, rb, mode, window):
    if mode == "bias":
        bias_ref, o_ref, m_sc, acc_sc = rest
    elif mode == "sel":
        sel_ref, e_ref, o_ref, m_sc, acc_sc = rest
    else:
        o_ref, m_sc, acc_sc = rest
    p = pl.program_id(2)
    qi, kj, fl = qi_ref[p], kj_ref[p], fl_ref[p]
    dk = q_ref.shape[-1]
    dv = v_ref.shape[-1]
    R = G * tq

    @pl.when((fl & 1) != 0)
    def _():
        m_sc[...] = jnp.full(m_sc.shape, NEG, F32)
        acc_sc[...] = jnp.zeros(acc_sc.shape, F32)

    def compute(position_mask):
        mask = None
        if position_mask:
            row = qi * tq + lax.broadcasted_iota(I32, (tq, tk), 0)
            col = kj * tk + lax.broadcasted_iota(I32, (tq, tk), 1)
            mask = col <= row
            if window is not None:
                mask = mask & (col > row - window)
        if mode == "sel":
            hit = jnp.dot(sel_ref[0, 0], e_ref[...], preferred_element_type=F32) > 0.5
            mask = hit if mask is None else (mask & hit)
        k = k_ref[0, 0]
        v_ext = jnp.concatenate([v_ref[0, 0], jnp.ones((tk, LANE), v_ref.dtype)], axis=1)
        for r in range(R // rb):
            g, t0 = divmod(r * rb, tq)
            rows = slice(r * rb, (r + 1) * rb)
            s = lax.dot_general(q_ref[0, g, t0:t0 + rb, :], k, (((1,), (1,)), ((), ())),
                                preferred_element_type=F32)
            if mode == "bias":
                s = s + bias_ref[0, 0, t0:t0 + rb, :].astype(F32)
            if mask is not None:
                s = jnp.where(mask[t0:t0 + rb], s, NEG)
            m_prev = m_sc[rows, :]
            m_new = jnp.maximum(m_prev, jnp.max(s, axis=-1, keepdims=True))
            alpha = jnp.exp2(m_prev - m_new)
            pr = jnp.concatenate([jnp.exp2(s[:, c * LANE:(c + 1) * LANE] - m_new).astype(v_ref.dtype)
                                  for c in range(tk // LANE)], axis=1)
            pv = jnp.dot(pr, v_ext, preferred_element_type=F32)
            acc_sc[rows, :] = jnp.concatenate([alpha] * (dv // LANE + 1), axis=1) * acc_sc[rows, :] + pv
            m_sc[rows, :] = m_new

    if mode == "bias":
        compute(False)
    else:
        pl.when((fl & 4) != 0)(functools.partial(compute, True))
        pl.when((fl & 4) == 0)(functools.partial(compute, False))

    @pl.when((fl & 2) != 0)
    def _():
        for g in range(G):
            rows = slice(g * tq, (g + 1) * tq)
            o = acc_sc[rows, 0:dv] / acc_sc[rows, dv:dv + LANE]
            o_ref[0, :, g * dv:(g + 1) * dv] = o.astype(o_ref.dtype)


def _flash(q, k, v, *, mode="causal", window=None, bias=None, sel=None, expand=None,
           tq=512, tk=512, rb=128, out_dtype=None):
    B, Hq, S, dk = q.shape
    Hkv, dv = k.shape[1], v.shape[-1]
    G = Hq // Hkv
    tq, tk = _tile(S, tq, 8), _tile(S, tk)
    qi, kj, fl = _pairs(S, tq, tk, window)
    P = int(qi.shape[0])
    in_specs = [pl.BlockSpec((1, G, tq, dk), lambda b, h, p, qi, kj, fl: (b, h, qi[p], 0)),
                pl.BlockSpec((1, 1, tk, dk), lambda b, h, p, qi, kj, fl: (b, h, kj[p], 0)),
                pl.BlockSpec((1, 1, tk, dv), lambda b, h, p, qi, kj, fl: (b, h, kj[p], 0))]
    args = [q, k, v]
    if mode == "bias":
        in_specs.append(pl.BlockSpec((1, 1, tq, tk), lambda b, h, p, qi, kj, fl: (b, kj[p], qi[p], 0)))
        args.append(bias)
    elif mode == "sel":
        in_specs += [pl.BlockSpec((1, 1, tq, LANE), lambda b, h, p, qi, kj, fl: (b, h, qi[p], 0)),
                     pl.BlockSpec((LANE, tk), lambda b, h, p, qi, kj, fl: (0, kj[p]))]
        args += [sel, expand]
    kern = functools.partial(_flash_kernel, G=G, tq=tq, tk=tk, rb=min(rb, tq), mode=mode, window=window)
    return pl.pallas_call(
        kern,
        out_shape=jax.ShapeDtypeStruct((B, S, Hq * dv), CDT if out_dtype is None else out_dtype),
        grid_spec=pltpu.PrefetchScalarGridSpec(
            num_scalar_prefetch=3,
            grid=(B, Hkv, P),
            in_specs=in_specs,
            out_specs=pl.BlockSpec((1, tq, G * dv), lambda b, h, p, qi, kj, fl: (b, qi[p], h)),
            scratch_shapes=[pltpu.VMEM((G * tq, LANE), F32), pltpu.VMEM((G * tq, dv + LANE), F32)]),
        compiler_params=_params(("parallel", "parallel", "arbitrary")),
    )(qi, kj, fl, *args)


def _mla_qprep_kernel(x_ref, cos_ref, sin_ref, gn_ref, gr_ref, o_ref, *, H, scale):
    ts = x_ref.shape[1]
    lane = lax.broadcasted_iota(I32, (ts, LANE), 1)
    lo = lane < 64
    cos4, sin4 = cos_ref[0], sin_ref[0]
    for h in range(H):
        xn = x_ref[0, :, h * LANE:(h + 1) * LANE]
        o_ref[0, h, :, 0:LANE] = (_rms(xn, gn_ref[...]) * scale).astype(o_ref.dtype)
    for j in range(H // 2):
        xr = x_ref[0, :, (H + j) * LANE:(H + j + 1) * LANE]
        ss = xr * xr
        s_lo = jnp.sum(jnp.where(lo, ss, 0.0), axis=-1, keepdims=True)
        s_hi = jnp.sum(jnp.where(lo, 0.0, ss), axis=-1, keepdims=True)
        inv = jnp.where(lo, lax.rsqrt(s_lo / 64.0 + EPS), lax.rsqrt(s_hi / 64.0 + EPS))
        r = _rope64pair(xr * inv * gr_ref[...], cos4, sin4, lane) * scale
        o_ref[0, 2 * j, :, LANE:2 * LANE] = jnp.where(lo, r, 0.0).astype(o_ref.dtype)
        o_ref[0, 2 * j + 1, :, LANE:2 * LANE] = jnp.where(lo, pltpu.roll(r, 64, 1), 0.0).astype(o_ref.dtype)


def _mla_kvprep_kernel(x_ref, kr_ref, cos_ref, sin_ref, gn_ref, gr_ref, k_ref, v_ref, *, H):
    ts = x_ref.shape[1]
    lane = lax.broadcasted_iota(I32, (ts, LANE), 1)
    kr = kr_ref[0]
    inv = lax.rsqrt(jnp.sum(kr * kr, axis=-1, keepdims=True) / 64.0 + EPS)
    r = _rope64pair(kr * inv * gr_ref[...], cos_ref[0], sin_ref[0], lane).astype(k_ref.dtype)
    for h in range(H):
        xn = x_ref[0, :, h * LANE:(h + 1) * LANE]
        k_ref[0, h, :, 0:LANE] = _rms(xn, gn_ref[...]).astype(k_ref.dtype)
        k_ref[0, h, :, LANE:2 * LANE] = r
        v_ref[0, h] = x_ref[0, :, (H + h) * LANE:(H + h + 1) * LANE].astype(v_ref.dtype)


def _mla(p, kr_block, pos, q_a_norm, kv_a_norm, w_uq, w_ukv, q_norm, k_norm):
    B, S, _ = p.shape
    H = MLA_HEADS
    N = B * S
    cqn = _norm(p, q_a_norm, width=MLA_Q_RANK, col_block=0)
    ckvn = _norm(p, kv_a_norm, width=MLA_KV_RANK, col_block=MLA_Q_RANK // MLA_KV_RANK)
    hh = np.arange(H)[:, None]
    q_perm = np.concatenate([(hh * 192 + np.arange(128)).ravel(), (hh * 192 + 128 + np.arange(64)).ravel()])
    kv_perm = np.concatenate([(hh * 256 + np.arange(128)).ravel(), (hh * 256 + 128 + np.arange(128)).ravel()])
    q_raw = _mm(cqn.reshape(N, -1), w_uq[:, q_perm].astype(CDT), tn=1024, tk=MLA_Q_RANK).reshape(B, S, -1)
    kv_raw = _mm(ckvn.reshape(N, -1), w_ukv[:, kv_perm].astype(CDT), tn=1024, tk=MLA_KV_RANK).reshape(B, S, -1)
    cos4, sin4 = _tables64pair(pos)
    ts = _tile(S, 256, 8)
    scale = (MLA_NOPE + MLA_ROPE) ** -0.5 * LOG2E
    gr = q_norm[MLA_NOPE:]
    tab = pl.BlockSpec((1, ts, LANE), lambda b, i: (b, i, 0))
    vec = pl.BlockSpec((1, LANE), lambda b, i: (0, 0))
    q = pl.pallas_call(
        functools.partial(_mla_qprep_kernel, H=H, scale=scale),
        out_shape=jax.ShapeDtypeStruct((B, H, S, 2 * LANE), CDT),
        grid=(B, S // ts),
        in_specs=[pl.BlockSpec((1, ts, H * 192), lambda b, i: (b, i, 0)), tab, tab, vec, vec],
        out_specs=pl.BlockSpec((1, H, ts, 2 * LANE), lambda b, i: (b, 0, i, 0)),
        compiler_params=_params(("parallel", "parallel")),
    )(q_raw, cos4, sin4, q_norm[:MLA_NOPE].reshape(1, -1), jnp.concatenate([gr, gr]).reshape(1, -1))
    gkr = jnp.concatenate([k_norm[MLA_NOPE:], jnp.zeros((64,), F32)])
    k, v = pl.pallas_call(
        functools.partial(_mla_kvprep_kernel, H=H),
        out_shape=(jax.ShapeDtypeStruct((B, H, S, 2 * LANE), CDT),
                   jax.ShapeDtypeStruct((B, H, S, LANE), CDT)),
        grid=(B, S // ts),
        in_specs=[pl.BlockSpec((1, ts, H * 256), lambda b, i: (b, i, 0)),
                  pl.BlockSpec((1, ts, LANE), lambda b, i: (b, i, kr_block)), tab, tab, vec, vec],
        out_specs=(pl.BlockSpec((1, H, ts, 2 * LANE), lambda b, i: (b, 0, i, 0)),
                   pl.BlockSpec((1, H, ts, LANE), lambda b, i: (b, 0, i, 0))),
        compiler_params=_params(("parallel", "parallel")),
    )(kv_raw, p, cos4, sin4, k_norm[:MLA_NOPE].reshape(1, -1), gkr.reshape(1, -1))
    return _flash(q, k, v, mode="causal", tq=1024, tk=1024, rb=256)


def _nsa_prep_kernel(q_ref, kc_ref, vc_ref, ks_ref, vs_ref, kw_ref, vw_ref, cos_ref, sin_ref, g_ref,
                     qo, kso, vso, kwo, vwo, kco, vco, *, H, G, scale):
    cosf, sinf = cos_ref[0], sin_ref[0]
    for h in range(H):
        y = _rms(q_ref[0, :, h * LANE:(h + 1) * LANE], g_ref[0:1, :])
        qo[0, h] = (_rope128(y, cosf, sinf) * scale).astype(qo.dtype)
    for g in range(G):
        sl = slice(g * LANE, (g + 1) * LANE)
        kso[0, g] = _rope128(_rms(ks_ref[0, :, sl], g_ref[2:3, :]), cosf, sinf).astype(kso.dtype)
        kwo[0, g] = _rope128(_rms(kw_ref[0, :, sl], g_ref[3:4, :]), cosf, sinf).astype(kwo.dtype)
        vso[0, g] = vs_ref[0, :, sl].astype(vso.dtype)
        vwo[0, g] = vw_ref[0, :, sl].astype(vwo.dtype)
        kco[0, g] = kc_ref[0, :, sl].astype(kco.dtype)
        vco[0, g] = vc_ref[0, :, sl].astype(vco.dtype)


def _compress_kernel(xk_ref, xv_ref, wk_ref, wv_ref, pek_ref, pev_ref, g_ref, cos_ref, sin_ref,
                     kc_ref, vc_ref):
    nc = xk_ref.shape[2]

    def comp(x_ref, w_ref, pe_ref):
        y = jnp.dot(x_ref[0, 0], w_ref[...], preferred_element_type=F32)
        c = jnp.dot(pe_ref[...], w_ref[...], preferred_element_type=F32)
        const = c[0:1, 0:LANE] + c[1:2, LANE:2 * LANE]
        return y[:, 0:LANE] + pltpu.roll(y[:, LANE:2 * LANE], nc - 1, 0) + const

    kc = comp(xk_ref, wk_ref, pek_ref)
    kc_ref[0, 0] = _rope128(_rms(kc, g_ref[...]), cos_ref[0], sin_ref[0]).astype(kc_ref.dtype)
    vc_ref[0, 0] = comp(xv_ref, wv_ref, pev_ref).astype(vc_ref.dtype)


def _cmp_attn_kernel(q_ref, kc_ref, vc_ref, m_ref, o_ref, sel_ref, *, G, tq, n_cmp, n_sel, dv):
    i = pl.program_id(2)
    nc = kc_ref.shape[2]
    q = q_ref[0].reshape(G * tq, q_ref.shape[-1])
    s = lax.dot_general(q, kc_ref[0, 0], (((1,), (1,)), ((), ())), preferred_element_type=F32)
    s = s.reshape(G, tq, nc)
    t = i * tq + lax.broadcasted_iota(I32, (tq, nc), 0)
    n = lax.broadcasted_iota(I32, (tq, nc), 1)
    mask = (n * NSA_CMP_STRIDE + (NSA_CMP_LEN - 1) <= t) & (n < n_cmp)
    s = jnp.where(mask[None], s, NEG)
    mx = jnp.max(s, axis=-1, keepdims=True)
    e = jnp.where(mask[None], jnp.exp2(s - mx), 0.0)
    l = jnp.sum(e, axis=-1, keepdims=True)
    pc = e / jnp.where(l > 0.0, l, 1.0)
    o = jnp.dot(pc.reshape(G * tq, nc).astype(vc_ref.dtype), vc_ref[0, 0], preferred_element_type=F32)
    for g in range(G):
        o_ref[0, :, g * dv:(g + 1) * dv] = o[g * tq:(g + 1) * tq].astype(o_ref.dtype)

    ps = jnp.sum(pc, axis=0)
    hi = ps.astype(CDT)
    lo_part = (ps - hi.astype(F32)).astype(CDT)
    imp = (jnp.dot(hi, m_ref[...], preferred_element_type=F32)
           + jnp.dot(lo_part, m_ref[...], preferred_element_type=F32))
    blk = lax.broadcasted_iota(I32, (tq, LANE), 1)
    cur = (i * tq + lax.broadcasted_iota(I32, (tq, LANE), 0)) >> (NSA_SLC_LEN.bit_length() - 1)
    forced = (blk == 0) | (blk == cur) | (blk == cur - 1)
    imp = jnp.where(forced, FORCE, jnp.where(blk <= cur, imp, NEG))
    v = imp.T
    rowi = lax.broadcasted_iota(I32, (LANE, tq), 0)

    def take(_, carry):
        v, chosen = carry
        mval = jnp.max(v, axis=0, keepdims=True)
        first = jnp.min(jnp.where(v == mval, rowi, LANE), axis=0, keepdims=True)
        hit = rowi == first
        chosen = jnp.where(hit & (mval > NEG_HALF), 1.0, chosen)
        return jnp.where(hit, REMOVED, v), chosen

    _, chosen = lax.fori_loop(0, n_sel, take, (v, jnp.zeros((LANE, tq), F32)))
    sel_ref[0, 0] = chosen.T.astype(sel_ref.dtype)


def _nsa_combine_kernel(oc_ref, os_ref, ow_ref, g_ref, o_ref, *, H, dv):
    gate = _sigmoid(g_ref[0])
    for h in range(H):
        sl = slice(h * dv, (h + 1) * dv)
        o = (oc_ref[0, :, sl] * gate[:, h:h + 1] + os_ref[0, :, sl] * gate[:, H + h:H + h + 1]
             + ow_ref[0, :, sl] * gate[:, 2 * H + h:2 * H + h + 1])
        o_ref[0, :, sl] = o.astype(o_ref.dtype)


def _nsa(p, q_block, kv_block0, gate_block, pos, qk_norm, cmp_pos, cmp_w):
    B, S, _ = p.shape
    H, G, DH = NSA_HEADS, NSA_KV_GROUPS, NSA_HEAD_DIM
    HPG = H // G
    scale = DH ** -0.5 * LOG2E
    ts = _tile(S, 256, 8)
    cosf, sinf = _tables128(pos)
    tab = pl.BlockSpec((1, ts, LANE), lambda b, i: (b, i, 0))
    kvspec = [pl.BlockSpec((1, ts, G * DH), functools.partial(lambda b, i, m: (b, i, kv_block0 + m), m=m))
              for m in range(6)]
    head_out = lambda n: pl.BlockSpec((1, n, ts, DH), lambda b, i: (b, 0, i, 0))
    kv_shape = jax.ShapeDtypeStruct((B, G, S, DH), CDT)
    q, ks, vs, kw, vw, kcr, vcr = pl.pallas_call(
        functools.partial(_nsa_prep_kernel, H=H, G=G, scale=scale),
        out_shape=(jax.ShapeDtypeStruct((B, H, S, DH), CDT),) + (kv_shape,) * 6,
        grid=(B, S // ts),
        in_specs=[pl.BlockSpec((1, ts, H * DH), lambda b, i: (b, i, q_block))] + kvspec
                 + [tab, tab, pl.BlockSpec((4, DH), lambda b, i: (0, 0))],
        out_specs=(head_out(H),) + (head_out(G),) * 6,
        compiler_params=_params(("parallel", "parallel")),
    )(p, p, p, p, p, p, p, cosf, sinf, qk_norm)

    half = NSA_CMP_LEN // 2
    nc = S // NSA_CMP_STRIDE
    n_cmp = (S - NSA_CMP_LEN) // NSA_CMP_STRIDE + 1
    cmp_end = jnp.minimum(jnp.arange(nc) * NSA_CMP_STRIDE + NSA_CMP_LEN - 1, S - 1)
    ccos, csin = _tables128(pos[:, cmp_end])
    wcat = lambda w: jnp.concatenate([w[:half].reshape(half * DH, DH), w[half:].reshape(half * DH, DH)], 1).astype(CDT)
    pecat = lambda pe: jnp.zeros((8, half * DH), F32).at[0].set(pe[:half].reshape(-1)).at[1].set(
        pe[half:].reshape(-1)).astype(CDT)
    xspec = pl.BlockSpec((1, 1, nc, half * DH), lambda b, g: (b, g, 0, 0))
    wspec = pl.BlockSpec((half * DH, 2 * DH), lambda b, g: (0, 0))
    pespec = pl.BlockSpec((8, half * DH), lambda b, g: (0, 0))
    cspec = pl.BlockSpec((1, 1, nc, DH), lambda b, g: (b, g, 0, 0))
    ctab = pl.BlockSpec((1, nc, DH), lambda b, g: (b, 0, 0))
    kc, vc = pl.pallas_call(
        _compress_kernel,
        out_shape=(jax.ShapeDtypeStruct((B, G, nc, DH), CDT),) * 2,
        grid=(B, G),
        in_specs=[xspec, xspec, wspec, wspec, pespec, pespec,
                  pl.BlockSpec((1, DH), lambda b, g: (0, 0)), ctab, ctab],
        out_specs=(cspec, cspec),
        compiler_params=_params(("parallel", "parallel")),
    )(kcr.reshape(B, G, nc, half * DH), vcr.reshape(B, G, nc, half * DH), wcat(cmp_w[0]), wcat(cmp_w[1]),
      pecat(cmp_pos[0]), pecat(cmp_pos[1]), qk_norm[1].reshape(1, DH), ccos, csin)

    n_slc = S // NSA_SLC_LEN
    assert n_slc <= LANE
    r, cl = NSA_SLC_LEN // NSA_CMP_STRIDE, NSA_CMP_LEN // NSA_CMP_STRIDE
    m_np = np.zeros((nc, LANE), np.float32)
    for j in range(n_slc):
        for a in range(r):
            for c in range(cl):
                ci = j * r + a + c - (cl - 1)
                if 0 <= ci < n_cmp:
                    m_np[ci, j] += 1.0
    tq = _tile(S, 256, 8)
    o_c, sel = pl.pallas_call(
        functools.partial(_cmp_attn_kernel, G=HPG, tq=tq, n_cmp=n_cmp, n_sel=min(NSA_N_SEL, n_slc), dv=DH),
        out_shape=(jax.ShapeDtypeStruct((B, S, H * DH), F32),
                   jax.ShapeDtypeStruct((B, G, S, LANE), CDT)),
        grid=(B, G, S // tq),
        in_specs=[pl.BlockSpec((1, HPG, tq, DH), lambda b, g, i: (b, g, i, 0)),
                  pl.BlockSpec((1, 1, nc, DH), lambda b, g, i: (b, g, 0, 0)),
                  pl.BlockSpec((1, 1, nc, DH), lambda b, g, i: (b, g, 0, 0)),
                  pl.BlockSpec((nc, LANE), lambda b, g, i: (0, 0))],
        out_specs=(pl.BlockSpec((1, tq, HPG * DH), lambda b, g, i: (b, i, g)),
                   pl.BlockSpec((1, 1, tq, LANE), lambda b, g, i: (b, g, i, 0))),
        compiler_params=_params(("parallel", "parallel", "parallel")),
    )(q, kc, vc, jnp.asarray(m_np, CDT))

    expand = jnp.asarray((np.arange(S)[None, :] // NSA_SLC_LEN) == np.arange(LANE)[:, None], CDT)
    o_s = _flash(q, ks, vs, mode="sel", sel=sel, expand=expand, tq=512, tk=1024, out_dtype=F32)
    o_w = _flash(q, kw, vw, mode="window", window=NSA_WINDOW, tq=512, tk=512, out_dtype=F32)
    ospec = pl.BlockSpec((1, ts, H * DH), lambda b, i: (b, i, 0))
    return pl.pallas_call(
        functools.partial(_nsa_combine_kernel, H=H, dv=DH),
        out_shape=jax.ShapeDtypeStruct((B, S, H * DH), CDT),
        grid=(B, S // ts),
        in_specs=[ospec, ospec, ospec, pl.BlockSpec((1, ts, LANE), lambda b, i: (b, i, gate_block))],
        out_specs=ospec,
        compiler_params=_params(("parallel", "parallel")),
    )(o_c, o_s, o_w, p)


def _dsa_prep_kernel(q_ref, k_ref, v_ref, qi_ref, ki_ref, wi_ref, cos_ref, sin_ref, cos4_ref, sin4_ref,
                     g_ref, gi_ref, qo, ko, vo, qio, kilo, kihi, wio, *, H, HKV, scale, wscale):
    ts = q_ref.shape[1]
    lane = lax.broadcasted_iota(I32, (ts, LANE), 1)
    cosf, sinf, cos4, sin4 = cos_ref[0], sin_ref[0], cos4_ref[0], sin4_ref[0]
    for h in range(H):
        y = _rms(q_ref[0, :, h * LANE:(h + 1) * LANE], g_ref[0:1, :])
        qo[0, h] = (_rope128(y, cosf, sinf) * scale).astype(qo.dtype)
    for h in range(HKV):
        sl = slice(h * LANE, (h + 1) * LANE)
        ko[0, h] = _rope128(_rms(k_ref[0, :, sl], g_ref[1:2, :]), cosf, sinf).astype(ko.dtype)
        vo[0, h] = v_ref[0, :, sl].astype(vo.dtype)
    for j in range(qi_ref.shape[2] // LANE):
        qio[0, j] = _rope64pair(qi_ref[0, :, j * LANE:(j + 1) * LANE], cos4, sin4, lane).astype(qio.dtype)
    ki = ki_ref[0]
    inv = lax.rsqrt(jnp.sum(ki * ki, axis=-1, keepdims=True) / 64.0 + EPS)
    r = _rope64pair(ki * inv * gi_ref[...], cos4, sin4, lane)
    kilo[0] = r.astype(kilo.dtype)
    kihi[0] = pltpu.roll(r, 64, 1).astype(kihi.dtype)
    wio[0] = wi_ref[0] * wscale


def _indexer_kernel(qi_ref, kj_ref, fl_ref, q_ref, klo_ref, khi_ref, w_ref, o_ref, wb_sc, sc_sc, acc_sc,
                    *, tq, tk, topk, n_pairs, rg):
    p = pl.program_id(1)
    qi, kj, fl = qi_ref[p], kj_ref[p], fl_ref[p]
    n_tiles = o_ref.shape[1]

    @pl.when((fl & 1) != 0)
    def _():
        w = w_ref[0]
        for h in range(2 * n_pairs):
            wb_sc[h] = jnp.broadcast_to(w[:, h:h + 1], (tq, LANE))

    acc_sc[...] = jnp.zeros((tq, tk), F32)
    cw = min(tk, 2 * LANE)

    def pair(j, _):
        q = q_ref[0, j]
        wa = jnp.tile(wb_sc[2 * j], (1, cw // LANE))
        wb = jnp.tile(wb_sc[2 * j + 1], (1, cw // LANE))
        for c in range(tk // cw):
            cols = slice(c * cw, (c + 1) * cw)
            sa = lax.dot_general(q, klo_ref[0, cols, :], (((1,), (1,)), ((), ())), preferred_element_type=F32)
            sb = lax.dot_general(q, khi_ref[0, cols, :], (((1,), (1,)), ((), ())), preferred_element_type=F32)
            acc_sc[:, cols] += wa * jnp.maximum(sa, 0.0) + wb * jnp.maximum(sb, 0.0)
        return 0

    lax.fori_loop(0, n_pairs, pair, 0, unroll=4)
    score = acc_sc[...]
    row = qi * tq + lax.broadcasted_iota(I32, (tq, tk), 0)
    col = kj * tk + lax.broadcasted_iota(I32, (tq, tk), 1)
    score = jnp.where(col <= row, score, NEG)
    bits = pltpu.bitcast(score, I32)
    key = bits ^ ((bits >> 31) & 0x7FFFFFFF)
    sc_sc[kj] = key

    @pl.when((fl & 2) != 0)
    def _():
        n_chunks = kj + 1
        nh_bits = int(np.float32(NEG_HALF).view(np.int32))
        key_neg_half = nh_bits ^ 0x7FFFFFFF if nh_bits < 0 else nh_bits
        for g in range(tq // rg):
            rows = pl.ds(g * rg, rg)

            def bit_step(state):
                b, thr, n_ge, _ = state
                cand = thr + jnp.left_shift(jnp.int32(1), 31 - b)

                def count(c, cnt):
                    blk = sc_sc[c, rows, :]
                    for u in range(tk // LANE):
                        cnt = cnt + (blk[:, u * LANE:(u + 1) * LANE] >= cand).astype(I32)
                    return cnt

                cnt = lax.fori_loop(0, n_chunks, count, jnp.zeros((rg, LANE), I32))
                tot = jnp.sum(cnt, axis=1, keepdims=True)
                take = tot >= topk
                n_ge = jnp.where(take, tot, n_ge)
                return b + 1, jnp.where(take, cand, thr), n_ge, jnp.max(n_ge)

            start = (jnp.int32(0), jnp.full((rg, LANE), -2**31, I32),
                     jnp.full((rg, LANE), 2**30, I32), jnp.int32(2**30))
            _, thr, _, _ = lax.while_loop(lambda st: (st[0] < 32) & (st[3] > topk), bit_step, start)
            thr = jnp.maximum(thr, key_neg_half + 1)
            thr_t = jnp.tile(thr, (1, tk // LANE))

            def emit(c, _):
                o_ref[0, c, rows, :] = jnp.where(sc_sc[c, rows, :] >= thr_t, 0.0, NEG).astype(o_ref.dtype)
                return 0

            def emit_masked(c, _):
                o_ref[0, c, rows, :] = jnp.full((rg, tk), NEG, o_ref.dtype)
                return 0

            lax.fori_loop(0, n_chunks, emit, 0)
            lax.fori_loop(n_chunks, n_tiles, emit_masked, 0)


def _dsa(p, pos, qk_norm, idx_k_norm):
    B, S, _ = p.shape
    H, HKV, DH = DSA_HEADS, DSA_KV_HEADS, DSA_HEAD_DIM
    NP = IDX_HEADS // 2
    ts = _tile(S, 256, 8)
    cosf, sinf = _tables128(pos)
    cos4, sin4 = _tables64pair(pos)
    tab = pl.BlockSpec((1, ts, LANE), lambda b, i: (b, i, 0))
    kw = HKV * DH
    gi = jnp.concatenate([idx_k_norm, jnp.zeros((LANE - IDX_DIM,), F32)]).reshape(1, LANE)
    head_out = lambda n: pl.BlockSpec((1, n, ts, DH), lambda b, i: (b, 0, i, 0))
    q, k, v, qidx, kilo, kihi, wi = pl.pallas_call(
        functools.partial(_dsa_prep_kernel, H=H, HKV=HKV, scale=DH ** -0.5 * LOG2E,
                          wscale=IDX_HEADS ** -0.5 * IDX_DIM ** -0.5),
        out_shape=(jax.ShapeDtypeStruct((B, H, S, DH), CDT), jax.ShapeDtypeStruct((B, HKV, S, DH), CDT),
                   jax.ShapeDtypeStruct((B, HKV, S, DH), CDT), jax.ShapeDtypeStruct((B, NP, S, LANE), CDT),
                   jax.ShapeDtypeStruct((B, S, LANE), CDT), jax.ShapeDtypeStruct((B, S, LANE), CDT),
                   jax.ShapeDtypeStruct((B, S, LANE), F32)),
        grid=(B, S // ts),
        in_specs=[pl.BlockSpec((1, ts, H * DH), lambda b, i: (b, i, 0)),
                  pl.BlockSpec((1, ts, kw), lambda b, i: (b, i, H * DH // kw)),
                  pl.BlockSpec((1, ts, kw), lambda b, i: (b, i, H * DH // kw + 1)),
                  pl.BlockSpec((1, ts, NP * LANE), lambda b, i: (b, i, (H * DH + 2 * kw) // (NP * LANE))),
                  pl.BlockSpec((1, ts, LANE), lambda b, i: (b, i, (H * DH + 2 * kw + NP * LANE) // LANE)),
                  pl.BlockSpec((1, ts, LANE), lambda b, i: (b, i, (H * DH + 2 * kw + NP * LANE) // LANE + 1)),
                  tab, tab, tab, tab,
                  pl.BlockSpec((2, DH), lambda b, i: (0, 0)), pl.BlockSpec((1, LANE), lambda b, i: (0, 0))],
        out_specs=(head_out(H), head_out(HKV), head_out(HKV), head_out(NP), tab, tab, tab),
        compiler_params=_params(("parallel", "parallel")),
    )(p, p, p, p, p, p, cosf, sinf, cos4, sin4, qk_norm, gi)

    topk = min(DSA_TOPK_MAX, S // 4)
    tq, tk = _tile(S, 256, 8), _tile(S, 1024)
    qi_t, kj_t, fl_t = _pairs(S, tq, tk)
    bias = pl.pallas_call(
        functools.partial(_indexer_kernel, tq=tq, tk=tk, topk=topk, n_pairs=NP, rg=min(64, tq)),
        out_shape=jax.ShapeDtypeStruct((B, S // tk, S, tk), CDT),
        grid_spec=pltpu.PrefetchScalarGridSpec(
            num_scalar_prefetch=3,
            grid=(B, int(qi_t.shape[0])),
            in_specs=[pl.BlockSpec((1, NP, tq, LANE), lambda b, p, qi, kj, fl: (b, 0, qi[p], 0)),
                      pl.BlockSpec((1, tk, LANE), lambda b, p, qi, kj, fl: (b, kj[p], 0)),
                      pl.BlockSpec((1, tk, LANE), lambda b, p, qi, kj, fl: (b, kj[p], 0)),
                      pl.BlockSpec((1, tq, LANE), lambda b, p, qi, kj, fl: (b, qi[p], 0))],
            out_specs=pl.BlockSpec((1, S // tk, tq, tk), lambda b, p, qi, kj, fl: (b, 0, qi[p], 0)),
            scratch_shapes=[pltpu.VMEM((2 * NP, tq, LANE), F32), pltpu.VMEM((S // tk, tq, tk), I32),
                            pltpu.VMEM((tq, tk), F32)]),
        compiler_params=_params(("parallel", "arbitrary")),
    )(qi_t, kj_t, fl_t, qidx, kilo, kihi, wi)
    return _flash(q, k, v, mode="bias", bias=bias, tq=2 * tq, tk=tk)


def _router_kernel(l_ref, i_ref, p_ref, *, n_experts):
    x = l_ref[...]
    lane = lax.broadcasted_iota(I32, x.shape, 1)
    x = jnp.where(lane < n_experts, x, -jnp.inf)
    m1 = jnp.max(x, axis=1, keepdims=True)
    i1 = jnp.min(jnp.where(x == m1, lane, LANE), axis=1, keepdims=True)
    x2 = jnp.where(lane == i1, -jnp.inf, x)
    m2 = jnp.max(x2, axis=1, keepdims=True)
    i2 = jnp.min(jnp.where(x2 == m2, lane, LANE), axis=1, keepdims=True)
    e2 = jnp.exp(m2 - m1)
    p1 = 1.0 / (1.0 + e2)
    p2 = e2 / (1.0 + e2)
    i_ref[...] = jnp.where(lane == 0, i1, jnp.where(lane == 1, i2, 0))
    p_ref[...] = jnp.where(lane == 0, p1, jnp.where(lane == 1, p2, 0.0))


def _row_copy(src_hbm, row, dst_vmem, r, sem):
    return pltpu.make_async_copy(src_hbm.at[pl.ds(row, 1), :], dst_vmem.at[pl.ds(r, 1), :], sem)


def _gather_pipeline(n_steps, tm, copies):
    i = pl.program_id(0)
    slot = i % 2

    def start_tile(tile, slot):
        def body(r, _):
            for src, idx_ref, dst, sem in copies(slot):
                _row_copy(src, idx_ref[tile * tm + r], dst, r, sem).start()
            return 0
        lax.fori_loop(0, tm, body, 0)

    def wait_tile(slot):
        def body(r, _):
            for src, _, dst, sem in copies(slot):
                _row_copy(src, 0, dst, r, sem).wait()
            return 0
        lax.fori_loop(0, tm, body, 0)

    @pl.when(i == 0)
    def _():
        start_tile(0, 0)

    @pl.when(i + 1 < n_steps)
    def _():
        start_tile(i + 1, 1 - slot)

    wait_tile(slot)
    return slot


def _dispatch_kernel(tok_ref, h_hbm, o_ref, buf, sem, *, tm, n_steps):
    slot = _gather_pipeline(n_steps, tm, lambda s: [(h_hbm, tok_ref, buf.at[s], sem.at[s])])
    o_ref[...] = buf[slot].astype(o_ref.dtype)


def _combine_kernel(s1_ref, s2_ref, y_hbm, x_ref, g_ref, p_ref, o_ref, buf1, buf2, sem, *, tm, n_steps):
    slot = _gather_pipeline(n_steps, tm, lambda s: [(y_hbm, s1_ref, buf1.at[s], sem.at[s]),
                                                    (y_hbm, s2_ref, buf2.at[s], sem.at[s])])
    p = p_ref[...]
    o_ref[...] = x_ref[...] + g_ref[0] * (p[:, 0:1] * buf1[slot] + p[:, 1:2] * buf2[slot])


def _moe(h, h32, x, g_f, w_router, w_gate, w_up, w_down, S):
    N, D = h.shape
    E, _, DE = w_gate.shape
    wr = jnp.zeros((D, LANE), F32).at[:, :E].set(w_router).astype(CDT)
    logits = _mm(h, wr, tn=LANE, tk=D)
    tr = _tile(N, 1024, 8)
    spec = pl.BlockSpec((tr, LANE), lambda i: (i, 0))
    idx, prob = pl.pallas_call(
        functools.partial(_router_kernel, n_experts=E),
        out_shape=(jax.ShapeDtypeStruct((N, LANE), I32), jax.ShapeDtypeStruct((N, LANE), F32)),
        grid=(N // tr,),
        in_specs=[spec],
        out_specs=(spec, spec),
        compiler_params=_params(("parallel",)),
    )(logits)

    tm = min(512, N)
    n_rows = 2 * N + E * tm
    e_flat = jnp.concatenate([idx[:, 0], idx[:, 1]])
    onehot = (e_flat[:, None] == jnp.arange(E)[None, :]).astype(I32)
    csum = jnp.cumsum(onehot, axis=0)
    rank = jnp.take_along_axis(csum, e_flat[:, None], axis=1)[:, 0] - 1
    padded = (csum[-1] + tm - 1) // tm * tm
    ends = jnp.cumsum(padded)
    pos = (ends - padded)[e_flat] + rank
    tile_expert = jnp.minimum(jnp.sum(jnp.arange(n_rows // tm)[:, None] * tm >= ends[None, :], axis=1), E - 1)
    tok = jnp.tile(jnp.arange(N, dtype=I32), 2)
    row_token = jnp.zeros((n_rows,), I32).at[pos].set(tok)

    tg = min(256, N)
    xs = pl.pallas_call(
        functools.partial(_dispatch_kernel, tm=tg, n_steps=n_rows // tg),
        out_shape=jax.ShapeDtypeStruct((n_rows, D), CDT),
        grid_spec=pltpu.PrefetchScalarGridSpec(
            num_scalar_prefetch=1,
            grid=(n_rows // tg,),
            in_specs=[pl.BlockSpec(memory_space=pl.ANY)],
            out_specs=pl.BlockSpec((tg, D), lambda i, tok: (i, 0)),
            scratch_shapes=[pltpu.VMEM((2, tg, D), F32), pltpu.SemaphoreType.DMA((2,))]),
        compiler_params=_params(("arbitrary",)),
    )(row_token, h32)
    te = tile_expert.astype(I32)
    hid = _mm(xs, w_gate.astype(CDT), mode="swiglu", b2=w_up.astype(CDT), group=te, out_dtype=CDT,
              tm=tm, tn=DE, tk=1024)
    ys = _mm(hid, w_down.astype(CDT), group=te, tm=tm, tn=1024, tk=DE)
    return pl.pallas_call(
        functools.partial(_combine_kernel, tm=tg, n_steps=N // tg),
        out_shape=jax.ShapeDtypeStruct((N, D), F32),
        grid_spec=pltpu.PrefetchScalarGridSpec(
            num_scalar_prefetch=2,
            grid=(N // tg,),
            in_specs=[pl.BlockSpec(memory_space=pl.ANY),
                      pl.BlockSpec((tg, D), lambda i, s1, s2: (i, 0)),
                      pl.BlockSpec((1, 1, D), lambda i, s1, s2: ((i * tg) // S, 0, 0)),
                      pl.BlockSpec((tg, LANE), lambda i, s1, s2: (i, 0))],
            out_specs=pl.BlockSpec((tg, D), lambda i, s1, s2: (i, 0)),
            scratch_shapes=[pltpu.VMEM((2, tg, D), F32), pltpu.VMEM((2, tg, D), F32),
                            pltpu.SemaphoreType.DMA((2,))]),
        compiler_params=_params(("arbitrary",)),
    )(pos[:N].astype(I32), pos[N:].astype(I32), ys, x, g_f, prob)


def _pad_cols(blocks, total):
    cols = []
    for w, width in blocks:
        cols.append(w)
        if width > w.shape[1]:
            cols.append(jnp.zeros((w.shape[0], width - w.shape[1]), w.dtype))
    out = jnp.concatenate(cols, axis=1)
    if total > out.shape[1]:
        out = jnp.concatenate([out, jnp.zeros((out.shape[0], total - out.shape[1]), out.dtype)], axis=1)
    return out.astype(CDT)


def _round_up(n, m):
    return (n + m - 1) // m * m


def kernel(x, c, positions, ada_w, ada_b, ada_table, norm_g, ev_w_in, ev_w_out, mla_q_a_norm, mla_kv_a_norm, mla_w_uq, mla_w_ukv, mla_q_norm, mla_k_norm, nsa_qk_norm, nsa_cmp_pos, nsa_cmp_w, ffn_w_gate, ffn_w_up, ffn_w_down, od_w_in, od_w_out, dsa_qk_norm, idx_k_norm, moe_router, moe_w_gate, moe_w_up, moe_w_down):
    B, S, D = x.shape
    N = B * S
    depth = ada_table.shape[0]
    cond = _cond(c, ada_w, ada_b).reshape(B, 6, D)

    mla_in = MLA_Q_RANK + MLA_KV_RANK + MLA_ROPE
    nq = NSA_HEADS * NSA_HEAD_DIM
    nkv = 6 * NSA_KV_GROUPS * NSA_HEAD_DIM
    hn = np.arange(NSA_HEADS)
    gate_perm = np.concatenate([hn * 3 + r for r in range(3)])

    x2 = x.reshape(N, D)
    for l in range(depth):
        i = l // 2
        mod = cond + ada_table[l]
        sh_a, sc_a, g_a, sh_f, sc_f, g_f = [mod[:, j, None, :] for j in range(6)]
        h = _norm(x2.reshape(B, S, D), norm_g[l, 0], sc_a, sh_a).reshape(N, D)
        if l % 2 == 0:
            w = ev_w_in[i]
            nsa = w[:, mla_in:]
            blocks = [(w[:, :MLA_Q_RANK + MLA_KV_RANK], MLA_Q_RANK + MLA_KV_RANK),
                      (nsa[:, :nq + nkv], nq + nkv),
                      (w[:, MLA_Q_RANK + MLA_KV_RANK:mla_in], LANE),
                      (nsa[:, nq + nkv:][:, gate_perm], LANE)]
            width = MLA_Q_RANK + MLA_KV_RANK + nq + nkv + 2 * LANE
            w_in = _pad_cols(blocks, _round_up(width, 512))
            p = _mm(h, w_in).reshape(B, S, -1)
            off = MLA_Q_RANK + MLA_KV_RANK
            a_out = _mla(p, (off + nq + nkv) // LANE, positions, mla_q_a_norm[i], mla_kv_a_norm[i],
                         mla_w_uq[i], mla_w_ukv[i], mla_q_norm[i], mla_k_norm[i])
            b_out = _nsa(p, off // nq, (off + nq) // (NSA_KV_GROUPS * NSA_HEAD_DIM),
                         (off + nq + nkv) // LANE + 1, positions, nsa_qk_norm[i], nsa_cmp_pos[i], nsa_cmp_w[i])
            mix = (a_out.reshape(N, -1), b_out.reshape(N, -1))
            w_out = ev_w_out[i]
        else:
            w = od_w_in[i]
            main = DSA_HEADS * DSA_HEAD_DIM + 2 * DSA_KV_HEADS * DSA_HEAD_DIM + IDX_HEADS * IDX_DIM
            blocks = [(w[:, :main], main), (w[:, main:main + IDX_DIM], LANE), (w[:, main + IDX_DIM:], LANE)]
            w_in = _pad_cols(blocks, _round_up(main + 2 * LANE, 512))
            p = _mm(h, w_in).reshape(B, S, -1)
            mix = _dsa(p, positions, dsa_qk_norm[i], idx_k_norm[i]).reshape(N, -1)
            w_out = od_w_out[i]
        x2 = _mm(mix, w_out.astype(CDT), mode="res", x=x2, g=g_a, rows_per_batch=S)
        if l % 2 == 0:
            h = _norm(x2.reshape(B, S, D), norm_g[l, 1], sc_f, sh_f).reshape(N, D)
            hid = _mm(h, ffn_w_gate[i].astype(CDT), mode="swiglu", b2=ffn_w_up[i].astype(CDT), out_dtype=CDT)
            x2 = _mm(hid, ffn_w_down[i].astype(CDT), mode="res", x=x2, g=g_f, tn=1024, tk=2048,
                     rows_per_batch=S)
        else:
            h, h32 = _norm(x2.reshape(B, S, D), norm_g[l, 1], sc_f, sh_f, also_f32=True)
            x2 = _moe(h.reshape(N, D), h32.reshape(N, D), x2, g_f, moe_router[i], moe_w_gate[i], moe_w_up[i],
                      moe_w_down[i], S)
    return x2.reshape(B, S, D)
```

```python
import functools

import numpy as np
import jax
import jax.numpy as jnp
from jax import lax
from jax.experimental import pallas as pl
from jax.experimental.pallas import tpu as pltpu

F32 = jnp.float32
I32 = jnp.int32
CDT = jnp.bfloat16

ROPE_THETA = 10000.0
EPS = 1e-6
NEG = -1e30
NEG_HALF = -5e29
FORCE = 1e9
REMOVED = -3e38
LOG2E = 1.4426950408889634

MLA_HEADS, MLA_Q_RANK, MLA_KV_RANK, MLA_NOPE, MLA_ROPE, MLA_V = 16, 1536, 512, 128, 64, 128
NSA_HEADS, NSA_KV_GROUPS, NSA_HEAD_DIM = 16, 4, 128
NSA_CMP_LEN, NSA_CMP_STRIDE, NSA_SLC_LEN, NSA_N_SEL, NSA_WINDOW = 32, 16, 64, 16, 512
DSA_HEADS, DSA_KV_HEADS, DSA_HEAD_DIM, IDX_HEADS, IDX_DIM, DSA_TOPK_MAX = 32, 8, 128, 32, 64, 256
N_EXPERTS = 8

LANE = 128
VMEM_LIMIT = 56 * 2**20


def _tile(n, pref, mult=LANE):
    if n <= pref:
        return n
    t = (pref // mult) * mult
    while t >= mult:
        if n % t == 0:
            return t
        t -= mult
    return n


def _params(sem):
    return pltpu.CompilerParams(dimension_semantics=sem, vmem_limit_bytes=VMEM_LIMIT)


def _silu(x):
    return x / (1.0 + jnp.exp(-x))


def _sigmoid(x):
    return 1.0 / (1.0 + jnp.exp(-x))


def _rms(x, g):
    return x * lax.rsqrt(jnp.mean(x * x, axis=-1, keepdims=True) + EPS) * g


def _rope128(y, cosf, sinf):
    return y * cosf + pltpu.roll(y, 64, 1) * sinf


def _rope64pair(y, cos4, sin4, lane):
    rot = jnp.where((lane & 63) < 32, pltpu.roll(y, 96, 1), pltpu.roll(y, 32, 1))
    return y * cos4 + rot * sin4


def _mm_kernel(*refs, nk, mode, grouped, split_a):
    if grouped:
        refs = refs[1:]
    k = pl.program_id(2)
    a_ref = refs[0]
    a = a_ref[...]
    if split_a:
        a = jnp.where(k == 0, a, refs[1][...])
        refs = refs[1:]
    n_in = {"plain": 2, "res": 4, "swiglu": 3}[mode]
    o_ref = refs[n_in]
    acc, acc2 = (tuple(refs[n_in + 1:]) + (None, None))[:2]
    if mode == "plain":
        b_ref, = refs[1:n_in]
    elif mode == "res":
        b_ref, x_ref, g_ref = refs[1:n_in]
    else:
        b_ref, b2_ref = refs[1:n_in]

    part = jnp.dot(a, b_ref[...].astype(a.dtype), preferred_element_type=F32)
    if mode == "swiglu":
        part2 = jnp.dot(a, b2_ref[...].astype(a.dtype), preferred_element_type=F32)

    if nk > 1:
        @pl.when(k == 0)
        def _():
            acc[...] = part
            if mode == "swiglu":
                acc2[...] = part2

        @pl.when(k > 0)
        def _():
            acc[...] += part
            if mode == "swiglu":
                acc2[...] += part2

    def finish():
        r = acc[...] if nk > 1 else part
        if mode == "plain":
            o_ref[...] = r.astype(o_ref.dtype)
        elif mode == "res":
            o_ref[...] = (x_ref[...] + g_ref[0] * r).astype(o_ref.dtype)
        else:
            r2 = acc2[...] if nk > 1 else part2
            o_ref[...] = (_silu(r) * r2).astype(o_ref.dtype)

    if nk > 1:
        pl.when(k == nk - 1)(finish)
    else:
        finish()


def _mm(a, b, *, mode="plain", b2=None, x=None, g=None, group=None, out_dtype=F32,
        tm=1024, tn=512, tk=4096, rows_per_batch=None):
    split_a = isinstance(a, tuple)
    a_list = list(a) if split_a else [a]
    M, Ka = a_list[0].shape
    K = Ka * len(a_list)
    N = b.shape[-1]
    tm, tn, tk = _tile(M, tm, 8), _tile(N, tn), (Ka if split_a else _tile(K, tk))
    nk = K // tk
    grid = (M // tm, N // tn, nk)
    grouped = group is not None
    if grouped:
        assert group.shape == (M // tm,)
        b_spec = pl.BlockSpec((None, tk, tn), lambda i, j, k, ge: (ge[i], k, j))
    else:
        b_spec = pl.BlockSpec((tk, tn), lambda i, j, k, *_: (k, j))
    if split_a:
        in_specs = [pl.BlockSpec((tm, tk), lambda i, j, k, *_: (i, 0))] * 2 + [b_spec]
    else:
        in_specs = [pl.BlockSpec((tm, tk), lambda i, j, k, *_: (i, k)), b_spec]
    args = a_list + [b]
    scratch = [pltpu.VMEM((tm, tn), F32)] if nk > 1 else []
    if mode == "swiglu":
        in_specs.append(b_spec)
        args.append(b2)
        scratch = scratch * 2
    if mode == "res":
        rpb = rows_per_batch
        assert rpb % tm == 0
        in_specs += [pl.BlockSpec((tm, tn), lambda i, j, k, *_: (i, j)),
                     pl.BlockSpec((1, 1, tn), lambda i, j, k, *_: ((i * tm) // rpb, 0, j))]
        args += [x, g]
    return pl.pallas_call(
        functools.partial(_mm_kernel, nk=nk, mode=mode, grouped=grouped, split_a=split_a),
        out_shape=jax.ShapeDtypeStruct((M, N), out_dtype),
        grid_spec=pltpu.PrefetchScalarGridSpec(
            num_scalar_prefetch=1 if grouped else 0,
            grid=grid,
            in_specs=in_specs,
            out_specs=pl.BlockSpec((tm, tn), lambda i, j, k, *_: (i, j)),
            scratch_shapes=scratch),
        compiler_params=_params(("parallel", "parallel", "arbitrary")),
    )(*(([group] if grouped else []) + args))


def _cond_kernel(c_ref, w_ref, b_ref, o_ref):
    a = _silu(c_ref[...]).astype(CDT)
    o_ref[...] = jnp.dot(a, w_ref[...].astype(CDT), preferred_element_type=F32) + b_ref[...]


def _cond(c, ada_w, ada_b):
    B, D = c.shape
    N = ada_w.shape[1]
    cp = jnp.zeros((8, D), F32).at[:B].set(c)
    tn = _tile(N, 512)
    out = pl.pallas_call(
        _cond_kernel,
        out_shape=jax.ShapeDtypeStruct((8, N), F32),
        grid=(N // tn,),
        in_specs=[pl.BlockSpec((8, D), lambda j: (0, 0)),
                  pl.BlockSpec((D, tn), lambda j: (0, j)),
                  pl.BlockSpec((1, tn), lambda j: (0, j))],
        out_specs=pl.BlockSpec((8, tn), lambda j: (0, j)),
        compiler_params=_params(("parallel",)),
    )(cp, ada_w, ada_b.reshape(1, N))
    return out[:B]


def _norm_kernel(*refs, modulate, n_out):
    outs = refs[len(refs) - n_out:]
    if modulate:
        x_ref, g_ref, sc_ref, sh_ref = refs[:4]
    else:
        x_ref, g_ref = refs[:2]
    y = _rms(x_ref[0], g_ref[...])
    if modulate:
        y = y * (1.0 + sc_ref[0]) + sh_ref[0]
    for o_ref in outs:
        o_ref[0] = y.astype(o_ref.dtype)


def _norm(x, g, sc=None, sh=None, *, width=None, col_block=0, ts=256, also_f32=False):
    B, S, W = x.shape
    width = W if width is None else width
    ts = _tile(S, ts, 8)
    modulate = sc is not None
    in_specs = [pl.BlockSpec((1, ts, width), lambda b, i: (b, i, col_block)),
                pl.BlockSpec((1, width), lambda b, i: (0, 0))]
    args = [x, g.reshape(1, width)]
    if modulate:
        in_specs += [pl.BlockSpec((1, 1, width), lambda b, i: (b, 0, 0))] * 2
        args += [sc, sh]
    dtypes = (CDT, F32) if also_f32 else (CDT,)
    out = pl.pallas_call(
        functools.partial(_norm_kernel, modulate=modulate, n_out=len(dtypes)),
        out_shape=tuple(jax.ShapeDtypeStruct((B, S, width), d) for d in dtypes),
        grid=(B, S // ts),
        in_specs=in_specs,
        out_specs=tuple(pl.BlockSpec((1, ts, width), lambda b, i: (b, i, 0)) for _ in dtypes),
        compiler_params=_params(("parallel", "parallel")),
    )(*args)
    return out if also_f32 else out[0]


def _rope_angles(pos, dim):
    inv = ROPE_THETA ** (-jnp.arange(0, dim, 2, dtype=F32) / dim)
    ang = pos.astype(F32)[..., None] * inv
    return jnp.cos(ang), jnp.sin(ang)


def _tables128(pos):
    c, s = _rope_angles(pos, 128)
    return jnp.concatenate([c, c], -1), jnp.concatenate([-s, s], -1)


def _tables64pair(pos):
    c, s = _rope_angles(pos, 64)
    return jnp.concatenate([c, c, c, c], -1), jnp.concatenate([-s, s, -s, s], -1)


def _pairs(S, tq, tk, window=None):
    qi, kj, fl = [], [], []
    for i in range(S // tq):
        lo = 0 if window is None else max(0, i * tq - window + 1)
        js = list(range(lo // tk, (i * tq + tq - 1) // tk + 1))
        for n, j in enumerate(js):
            diag = (j + 1) * tk - 1 > i * tq
            f = (1 if n == 0 else 0) | (2 if n == len(js) - 1 else 0)
            f |= 4 if (diag or window is not None) else 0
            qi.append(i), kj.append(j), fl.append(f)
    return (jnp.asarray(qi, I32), jnp.asarray(kj, I32), jnp.asarray(fl, I32))


def _flash_kernel(qi_ref, kj_ref, fl_ref, q_ref, k_ref, v_ref, *rest, G, tq, tk, rb, mode, window):
    if mode == "bias":
        bias_ref, o_ref, m_sc, acc_sc = rest
    elif mode == "sel":
        sel_ref, e_ref, o_ref, m_sc, acc_sc = rest
    else:
        o_ref, m_sc, acc_sc = rest
    p = pl.program_id(2)
    qi, kj, fl = qi_ref[p], kj_ref[p], fl_ref[p]
    dk = q_ref.shape[-1]
    dv = v_ref.shape[-1]
    R = G * tq

    @pl.when((fl & 1) != 0)
    def _():
        m_sc[...] = jnp.full(m_sc.shape, NEG, F32)
        acc_sc[...] = jnp.zeros(acc_sc.shape, F32)

    def compute(position_mask):
        mask = None
        if position_mask:
            row = qi * tq + lax.broadcasted_iota(I32, (tq, tk), 0)
            col = kj * tk + lax.broadcasted_iota(I32, (tq, tk), 1)
            mask = col <= row
            if window is not None:
                mask = mask & (col > row - window)
        if mode == "sel":
            hit = jnp.dot(sel_ref[0, 0], e_ref[...], preferred_element_type=F32) > 0.5
            mask = hit if mask is None else (mask & hit)
        ones = jnp.ones((tk, LANE), v_ref.dtype)
        v_exts = [jnp.concatenate([v_ref[0, h], ones], axis=1) for h in range(v_ref.shape[1])]
        for r in range(R // rb):
            g, t0 = divmod(r * rb, tq)
            kv = g % k_ref.shape[1]
            v_ext = v_exts[kv]
            rows = slice(r * rb, (r + 1) * rb)
            s = lax.dot_general(q_ref[0, g, t0:t0 + rb, :], k_ref[0, kv], (((1,), (1,)), ((), ())),
                                preferred_element_type=F32)
            if mode == "bias":
                s = s + bias_ref[0, 0, t0:t0 + rb, :].astype(F32)
            if mask is not None:
                s = jnp.where(mask[t0:t0 + rb], s, NEG)
            m_prev = m_sc[rows, :]
            m_new = jnp.maximum(m_prev, jnp.max(s, axis=-1, keepdims=True))
            alpha = jnp.exp2(m_prev - m_new)
            pr = jnp.concatenate([jnp.exp2(s[:, c * LANE:(c + 1) * LANE] - m_new).astype(v_ref.dtype)
                                  for c in range(tk // LANE)], axis=1)
            pv = jnp.dot(pr, v_ext, preferred_element_type=F32)
            acc_sc[rows, :] = jnp.concatenate([alpha] * (dv // LANE + 1), axis=1) * acc_sc[rows, :] + pv
            m_sc[rows, :] = m_new

    if mode == "bias":
        compute(False)
    else:
        pl.when((fl & 4) != 0)(functools.partial(compute, True))
        pl.when((fl & 4) == 0)(functools.partial(compute, False))

    @pl.when((fl & 2) != 0)
    def _():
        for g in range(G):
            rows = slice(g * tq, (g + 1) * tq)
            o = acc_sc[rows, 0:dv] / acc_sc[rows, dv:dv + LANE]
            o_ref[0, :, g * dv:(g + 1) * dv] = o.astype(o_ref.dtype)


def _flash(q, k, v, *, mode="causal", window=None, bias=None, sel=None, expand=None,
           tq=512, tk=512, rb=128, heads_per_step=1, out_dtype=None):
    B, Hq, S, dk = q.shape
    Hkv, dv = k.shape[1], v.shape[-1]
    G = Hq // Hkv
    kvh = 1
    if heads_per_step > 1:
        assert G == 1 and Hq % heads_per_step == 0
        G = kvh = heads_per_step
    tq, tk = _tile(S, tq, 8), _tile(S, tk)
    qi, kj, fl = _pairs(S, tq, tk, window)
    P = int(qi.shape[0])
    in_specs = [pl.BlockSpec((1, G, tq, dk), lambda b, h, p, qi, kj, fl: (b, h, qi[p], 0)),
                pl.BlockSpec((1, kvh, tk, dk), lambda b, h, p, qi, kj, fl: (b, h, kj[p], 0)),
                pl.BlockSpec((1, kvh, tk, dv), lambda b, h, p, qi, kj, fl: (b, h, kj[p], 0))]
    args = [q, k, v]
    if mode == "bias":
        in_specs.append(pl.BlockSpec((1, 1, tq, tk), lambda b, h, p, qi, kj, fl: (b, kj[p], qi[p], 0)))
        args.append(bias)
    elif mode == "sel":
        in_specs += [pl.BlockSpec((1, 1, tq, LANE), lambda b, h, p, qi, kj, fl: (b, h, qi[p], 0)),
                     pl.BlockSpec((LANE, tk), lambda b, h, p, qi, kj, fl: (0, kj[p]))]
        args += [sel, expand]
    kern = functools.partial(_flash_kernel, G=G, tq=tq, tk=tk, rb=min(rb, tq), mode=mode, window=window)
    return pl.pallas_call(
        kern,
        out_shape=jax.ShapeDtypeStruct((B, S, Hq * dv), CDT if out_dtype is None else out_dtype),
        grid_spec=pltpu.PrefetchScalarGridSpec(
            num_scalar_prefetch=3,
            grid=(B, Hkv // kvh, P),
            in_specs=in_specs,
            out_specs=pl.BlockSpec((1, tq, G * dv), lambda b, h, p, qi, kj, fl: (b, qi[p], h)),
            scratch_shapes=[pltpu.VMEM((G * tq, LANE), F32), pltpu.VMEM((G * tq, dv + LANE), F32)]),
        compiler_params=_params(("parallel", "parallel", "arbitrary")),
    )(qi, kj, fl, *args)


def _mla_qprep_kernel(x_ref, cos_ref, sin_ref, gn_ref, gr_ref, o_ref, *, H, scale):
    ts = x_ref.shape[1]
    lane = lax.broadcasted_iota(I32, (ts, LANE), 1)
    lo = lane < 64
    cos4, sin4 = cos_ref[0], sin_ref[0]
    for h in range(H):
        xn = x_ref[0, :, h * LANE:(h + 1) * LANE]
        o_ref[0, h, :, 0:LANE] = (_rms(xn, gn_ref[...]) * scale).astype(o_ref.dtype)
    for j in range(H // 2):
        xr = x_ref[0, :, (H + j) * LANE:(H + j + 1) * LANE]
        ss = xr * xr
        s_lo = jnp.sum(jnp.where(lo, ss, 0.0), axis=-1, keepdims=True)
        s_hi = jnp.sum(jnp.where(lo, 0.0, ss), axis=-1, keepdims=True)
        inv = jnp.where(lo, lax.rsqrt(s_lo / 64.0 + EPS), lax.rsqrt(s_hi / 64.0 + EPS))
        r = _rope64pair(xr * inv * gr_ref[...], cos4, sin4, lane) * scale
        o_ref[0, 2 * j, :, LANE:2 * LANE] = jnp.where(lo, r, 0.0).astype(o_ref.dtype)
        o_ref[0, 2 * j + 1, :, LANE:2 * LANE] = jnp.where(lo, pltpu.roll(r, 64, 1), 0.0).astype(o_ref.dtype)


def _mla_kvprep_kernel(x_ref, kr_ref, cos_ref, sin_ref, gn_ref, gr_ref, k_ref, v_ref, *, H):
    ts = x_ref.shape[1]
    lane = lax.broadcasted_iota(I32, (ts, LANE), 1)
    kr = kr_ref[0]
    inv = lax.rsqrt(jnp.sum(kr * kr, axis=-1, keepdims=True) / 64.0 + EPS)
    r = _rope64pair(kr * inv * gr_ref[...], cos_ref[0], sin_ref[0], lane).astype(k_ref.dtype)
    for h in range(H):
        xn = x_ref[0, :, h * LANE:(h + 1) * LANE]
        k_ref[0, h, :, 0:LANE] = _rms(xn, gn_ref[...]).astype(k_ref.dtype)
        k_ref[0, h, :, LANE:2 * LANE] = r
        v_ref[0, h] = x_ref[0, :, (H + h) * LANE:(H + h + 1) * LANE].astype(v_ref.dtype)


def _mla(p, kr_block, pos, q_a_norm, kv_a_norm, w_uq, w_ukv, q_norm, k_norm):
    B, S, _ = p.shape
    H = MLA_HEADS
    N = B * S
    cqn = _norm(p, q_a_norm, width=MLA_Q_RANK, col_block=0)
    ckvn = _norm(p, kv_a_norm, width=MLA_KV_RANK, col_block=MLA_Q_RANK // MLA_KV_RANK)
    hh = np.arange(H)[:, None]
    q_perm = np.concatenate([(hh * 192 + np.arange(128)).ravel(), (hh * 192 + 128 + np.arange(64)).ravel()])
    kv_perm = np.concatenate([(hh * 256 + np.arange(128)).ravel(), (hh * 256 + 128 + np.arange(128)).ravel()])
    q_raw = _mm(cqn.reshape(N, -1), w_uq[:, q_perm].astype(CDT), tn=1024, tk=MLA_Q_RANK).reshape(B, S, -1)
    kv_raw = _mm(ckvn.reshape(N, -1), w_ukv[:, kv_perm].astype(CDT), tn=1024, tk=MLA_KV_RANK).reshape(B, S, -1)
    cos4, sin4 = _tables64pair(pos)
    ts = _tile(S, 256, 8)
    scale = (MLA_NOPE + MLA_ROPE) ** -0.5 * LOG2E
    gr = q_norm[MLA_NOPE:]
    tab = pl.BlockSpec((1, ts, LANE), lambda b, i: (b, i, 0))
    vec = pl.BlockSpec((1, LANE), lambda b, i: (0, 0))
    q = pl.pallas_call(
        functools.partial(_mla_qprep_kernel, H=H, scale=scale),
        out_shape=jax.ShapeDtypeStruct((B, H, S, 2 * LANE), CDT),
        grid=(B, S // ts),
        in_specs=[pl.BlockSpec((1, ts, H * 192), lambda b, i: (b, i, 0)), tab, tab, vec, vec],
        out_specs=pl.BlockSpec((1, H, ts, 2 * LANE), lambda b, i: (b, 0, i, 0)),
        compiler_params=_params(("parallel", "parallel")),
    )(q_raw, cos4, sin4, q_norm[:MLA_NOPE].reshape(1, -1), jnp.concatenate([gr, gr]).reshape(1, -1))
    gkr = jnp.concatenate([k_norm[MLA_NOPE:], jnp.zeros((64,), F32)])
    k, v = pl.pallas_call(
        functools.partial(_mla_kvprep_kernel, H=H),
        out_shape=(jax.ShapeDtypeStruct((B, H, S, 2 * LANE), CDT),
                   jax.ShapeDtypeStruct((B, H, S, LANE), CDT)),
        grid=(B, S // ts),
        in_specs=[pl.BlockSpec((1, ts, H * 256), lambda b, i: (b, i, 0)),
                  pl.BlockSpec((1, ts, LANE), lambda b, i: (b, i, kr_block)), tab, tab, vec, vec],
        out_specs=(pl.BlockSpec((1, H, ts, 2 * LANE), lambda b, i: (b, 0, i, 0)),
                   pl.BlockSpec((1, H, ts, LANE), lambda b, i: (b, 0, i, 0))),
        compiler_params=_params(("parallel", "parallel")),
    )(kv_raw, p, cos4, sin4, k_norm[:MLA_NOPE].reshape(1, -1), gkr.reshape(1, -1))
    return _flash(q, k, v, mode="causal", tq=1024, tk=1024, rb=256, heads_per_step=4)


def _nsa_prep_kernel(q_ref, kc_ref, vc_ref, ks_ref, vs_ref, kw_ref, vw_ref, cos_ref, sin_ref, g_ref,
                     qo, kso, vso, kwo, vwo, kco, vco, *, H, G, scale):
    cosf, sinf = cos_ref[0], sin_ref[0]
    for h in range(H):
        y = _rms(q_ref[0, :, h * LANE:(h + 1) * LANE], g_ref[0:1, :])
        qo[0, h] = (_rope128(y, cosf, sinf) * scale).astype(qo.dtype)
    for g in range(G):
        sl = slice(g * LANE, (g + 1) * LANE)
        kso[0, g] = _rope128(_rms(ks_ref[0, :, sl], g_ref[2:3, :]), cosf, sinf).astype(kso.dtype)
        kwo[0, g] = _rope128(_rms(kw_ref[0, :, sl], g_ref[3:4, :]), cosf, sinf).astype(kwo.dtype)
        vso[0, g] = vs_ref[0, :, sl].astype(vso.dtype)
        vwo[0, g] = vw_ref[0, :, sl].astype(vwo.dtype)
        kco[0, g] = kc_ref[0, :, sl].astype(kco.dtype)
        vco[0, g] = vc_ref[0, :, sl].astype(vco.dtype)


def _compress_kernel(xk_ref, xv_ref, wk_ref, wv_ref, pek_ref, pev_ref, g_ref, cos_ref, sin_ref,
                     kc_ref, vc_ref):
    nc = xk_ref.shape[2]

    def comp(x_ref, w_ref, pe_ref):
        y = jnp.dot(x_ref[0, 0], w_ref[...], preferred_element_type=F32)
        c = jnp.dot(pe_ref[...], w_ref[...], preferred_element_type=F32)
        const = c[0:1, 0:LANE] + c[1:2, LANE:2 * LANE]
        return y[:, 0:LANE] + pltpu.roll(y[:, LANE:2 * LANE], nc - 1, 0) + const

    kc = comp(xk_ref, wk_ref, pek_ref)
    kc_ref[0, 0] = _rope128(_rms(kc, g_ref[...]), cos_ref[0], sin_ref[0]).astype(kc_ref.dtype)
    vc_ref[0, 0] = comp(xv_ref, wv_ref, pev_ref).astype(vc_ref.dtype)


def _cmp_attn_kernel(q_ref, kc_ref, vc_ref, m_ref, o_ref, sel_ref, *, G, tq, n_cmp, n_sel, dv):
    i = pl.program_id(2)
    nc = kc_ref.shape[2]
    q = q_ref[0].reshape(G * tq, q_ref.shape[-1])
    s = lax.dot_general(q, kc_ref[0, 0], (((1,), (1,)), ((), ())), preferred_element_type=F32)
    s = s.reshape(G, tq, nc)
    t = i * tq + lax.broadcasted_iota(I32, (tq, nc), 0)
    n = lax.broadcasted_iota(I32, (tq, nc), 1)
    mask = (n * NSA_CMP_STRIDE + (NSA_CMP_LEN - 1) <= t) & (n < n_cmp)
    s = jnp.where(mask[None], s, NEG)
    mx = jnp.max(s, axis=-1, keepdims=True)
    e = jnp.where(mask[None], jnp.exp2(s - mx), 0.0)
    l = jnp.sum(e, axis=-1, keepdims=True)
    pc = e / jnp.where(l > 0.0, l, 1.0)
    o = jnp.dot(pc.reshape(G * tq, nc).astype(vc_ref.dtype), vc_ref[0, 0], preferred_element_type=F32)
    for g in range(G):
        o_ref[0, :, g * dv:(g + 1) * dv] = o[g * tq:(g + 1) * tq].astype(o_ref.dtype)

    ps = jnp.sum(pc, axis=0)
    hi = ps.astype(CDT)
    lo_part = (ps - hi.astype(F32)).astype(CDT)
    imp = (jnp.dot(hi, m_ref[...], preferred_element_type=F32)
           + jnp.dot(lo_part, m_ref[...], preferred_element_type=F32))
    blk = lax.broadcasted_iota(I32, (tq, LANE), 1)
    cur = (i * tq + lax.broadcasted_iota(I32, (tq, LANE), 0)) >> (NSA_SLC_LEN.bit_length() - 1)
    forced = (blk == 0) | (blk == cur) | (blk == cur - 1)
    imp = jnp.where(forced, FORCE, jnp.where(blk <= cur, imp, NEG))
    v = imp.T
    rowi = lax.broadcasted_iota(I32, (LANE, tq), 0)

    def take(_, carry):
        v, chosen = carry
        mval = jnp.max(v, axis=0, keepdims=True)
        first = jnp.min(jnp.where(v == mval, rowi, LANE), axis=0, keepdims=True)
        hit = rowi == first
        chosen = jnp.where(hit & (mval > NEG_HALF), 1.0, chosen)
        return jnp.where(hit, REMOVED, v), chosen

    _, chosen = lax.fori_loop(0, n_sel, take, (v, jnp.zeros((LANE, tq), F32)))
    sel_ref[0, 0] = chosen.T.astype(sel_ref.dtype)


def _nsa_combine_kernel(oc_ref, os_ref, ow_ref, g_ref, o_ref, *, H, dv):
    gate = _sigmoid(g_ref[0])
    for h in range(H):
        sl = slice(h * dv, (h + 1) * dv)
        o = (oc_ref[0, :, sl] * gate[:, h:h + 1] + os_ref[0, :, sl] * gate[:, H + h:H + h + 1]
             + ow_ref[0, :, sl] * gate[:, 2 * H + h:2 * H + h + 1])
        o_ref[0, :, sl] = o.astype(o_ref.dtype)


def _nsa(p, q_block, kv_block0, gate_block, pos, qk_norm, cmp_pos, cmp_w):
    B, S, _ = p.shape
    H, G, DH = NSA_HEADS, NSA_KV_GROUPS, NSA_HEAD_DIM
    HPG = H // G
    scale = DH ** -0.5 * LOG2E
    ts = _tile(S, 256, 8)
    cosf, sinf = _tables128(pos)
    tab = pl.BlockSpec((1, ts, LANE), lambda b, i: (b, i, 0))
    kvspec = [pl.BlockSpec((1, ts, G * DH), functools.partial(lambda b, i, m: (b, i, kv_block0 + m), m=m))
              for m in range(6)]
    head_out = lambda n: pl.BlockSpec((1, n, ts, DH), lambda b, i: (b, 0, i, 0))
    kv_shape = jax.ShapeDtypeStruct((B, G, S, DH), CDT)
    q, ks, vs, kw, vw, kcr, vcr = pl.pallas_call(
        functools.partial(_nsa_prep_kernel, H=H, G=G, scale=scale),
        out_shape=(jax.ShapeDtypeStruct((B, H, S, DH), CDT),) + (kv_shape,) * 6,
        grid=(B, S // ts),
        in_specs=[pl.BlockSpec((1, ts, H * DH), lambda b, i: (b, i, q_block))] + kvspec
                 + [tab, tab, pl.BlockSpec((4, DH), lambda b, i: (0, 0))],
        out_specs=(head_out(H),) + (head_out(G),) * 6,
        compiler_params=_params(("parallel", "parallel")),
    )(p, p, p, p, p, p, p, cosf, sinf, qk_norm)

    half = NSA_CMP_LEN // 2
    nc = S // NSA_CMP_STRIDE
    n_cmp = (S - NSA_CMP_LEN) // NSA_CMP_STRIDE + 1
    cmp_end = jnp.minimum(jnp.arange(nc) * NSA_CMP_STRIDE + NSA_CMP_LEN - 1, S - 1)
    ccos, csin = _tables128(pos[:, cmp_end])
    wcat = lambda w: jnp.concatenate([w[:half].reshape(half * DH, DH), w[half:].reshape(half * DH, DH)], 1).astype(CDT)
    pecat = lambda pe: jnp.zeros((8, half * DH), F32).at[0].set(pe[:half].reshape(-1)).at[1].set(
        pe[half:].reshape(-1)).astype(CDT)
    xspec = pl.BlockSpec((1, 1, nc, half * DH), lambda b, g: (b, g, 0, 0))
    wspec = pl.BlockSpec((half * DH, 2 * DH), lambda b, g: (0, 0))
    pespec = pl.BlockSpec((8, half * DH), lambda b, g: (0, 0))
    cspec = pl.BlockSpec((1, 1, nc, DH), lambda b, g: (b, g, 0, 0))
    ctab = pl.BlockSpec((1, nc, DH), lambda b, g: (b, 0, 0))
    kc, vc = pl.pallas_call(
        _compress_kernel,
        out_shape=(jax.ShapeDtypeStruct((B, G, nc, DH), CDT),) * 2,
        grid=(B, G),
        in_specs=[xspec, xspec, wspec, wspec, pespec, pespec,
                  pl.BlockSpec((1, DH), lambda b, g: (0, 0)), ctab, ctab],
        out_specs=(cspec, cspec),
        compiler_params=_params(("parallel", "parallel")),
    )(kcr.reshape(B, G, nc, half * DH), vcr.reshape(B, G, nc, half * DH), wcat(cmp_w[0]), wcat(cmp_w[1]),
      pecat(cmp_pos[0]), pecat(cmp_pos[1]), qk_norm[1].reshape(1, DH), ccos, csin)

    n_slc = S // NSA_SLC_LEN
    assert n_slc <= LANE
    r, cl = NSA_SLC_LEN // NSA_CMP_STRIDE, NSA_CMP_LEN // NSA_CMP_STRIDE
    m_np = np.zeros((nc, LANE), np.float32)
    for j in range(n_slc):
        for a in range(r):
            for c in range(cl):
                ci = j * r + a + c - (cl - 1)
                if 0 <= ci < n_cmp:
                    m_np[ci, j] += 1.0
    tq = _tile(S, 256, 8)
    o_c, sel = pl.pallas_call(
        functools.partial(_cmp_attn_kernel, G=HPG, tq=tq, n_cmp=n_cmp, n_sel=min(NSA_N_SEL, n_slc), dv=DH),
        out_shape=(jax.ShapeDtypeStruct((B, S, H * DH), F32),
                   jax.ShapeDtypeStruct((B, G, S, LANE), CDT)),
        grid=(B, G, S // tq),
        in_specs=[pl.BlockSpec((1, HPG, tq, DH), lambda b, g, i: (b, g, i, 0)),
                  pl.BlockSpec((1, 1, nc, DH), lambda b, g, i: (b, g, 0, 0)),
                  pl.BlockSpec((1, 1, nc, DH), lambda b, g, i: (b, g, 0, 0)),
                  pl.BlockSpec((nc, LANE), lambda b, g, i: (0, 0))],
        out_specs=(pl.BlockSpec((1, tq, HPG * DH), lambda b, g, i: (b, i, g)),
                   pl.BlockSpec((1, 1, tq, LANE), lambda b, g, i: (b, g, i, 0))),
        compiler_params=_params(("parallel", "parallel", "parallel")),
    )(q, kc, vc, jnp.asarray(m_np, CDT))

    expand = jnp.asarray((np.arange(S)[None, :] // NSA_SLC_LEN) == np.arange(LANE)[:, None], CDT)
    o_s = _flash(q, ks, vs, mode="sel", sel=sel, expand=expand, tq=1024, tk=1024, out_dtype=F32)
    o_w = _flash(q, kw, vw, mode="window", window=NSA_WINDOW, tq=512, tk=512, out_dtype=F32)
    ospec = pl.BlockSpec((1, ts, H * DH), lambda b, i: (b, i, 0))
    return pl.pallas_call(
        functools.partial(_nsa_combine_kernel, H=H, dv=DH),
        out_shape=jax.ShapeDtypeStruct((B, S, H * DH), CDT),
        grid=(B, S // ts),
        in_specs=[ospec, ospec, ospec, pl.BlockSpec((1, ts, LANE), lambda b, i: (b, i, gate_block))],
        out_specs=ospec,
        compiler_params=_params(("parallel", "parallel")),
    )(o_c, o_s, o_w, p)


def _dsa_prep_kernel(q_ref, k_ref, v_ref, qi_ref, ki_ref, wi_ref, cos_ref, sin_ref, cos4_ref, sin4_ref,
                     g_ref, gi_ref, qo, ko, vo, qio, kilo, kihi, wio, *, H, HKV, scale, wscale):
    ts = q_ref.shape[1]
    lane = lax.broadcasted_iota(I32, (ts, LANE), 1)
    cosf, sinf, cos4, sin4 = cos_ref[0], sin_ref[0], cos4_ref[0], sin4_ref[0]
    for h in range(H):
        y = _rms(q_ref[0, :, h * LANE:(h + 1) * LANE], g_ref[0:1, :])
        qo[0, h] = (_rope128(y, cosf, sinf) * scale).astype(qo.dtype)
    for h in range(HKV):
        sl = slice(h * LANE, (h + 1) * LANE)
        ko[0, h] = _rope128(_rms(k_ref[0, :, sl], g_ref[1:2, :]), cosf, sinf).astype(ko.dtype)
        vo[0, h] = v_ref[0, :, sl].astype(vo.dtype)
    for j in range(qi_ref.shape[2] // LANE):
        qio[0, j] = _rope64pair(qi_ref[0, :, j * LANE:(j + 1) * LANE], cos4, sin4, lane).astype(qio.dtype)
    ki = ki_ref[0]
    inv = lax.rsqrt(jnp.sum(ki * ki, axis=-1, keepdims=True) / 64.0 + EPS)
    r = _rope64pair(ki * inv * gi_ref[...], cos4, sin4, lane)
    kilo[0] = r.astype(kilo.dtype)
    kihi[0] = pltpu.roll(r, 64, 1).astype(kihi.dtype)
    wio[0] = wi_ref[0] * wscale


def _indexer_kernel(qi_ref, kj_ref, fl_ref, q_ref, klo_ref, khi_ref, w_ref, o_ref, wb_sc, sc_sc, acc_sc,
                    *, tq, tk, topk, n_pairs, rg):
    p = pl.program_id(1)
    qi, kj, fl = qi_ref[p], kj_ref[p], fl_ref[p]
    n_tiles = o_ref.shape[1]

    @pl.when((fl & 1) != 0)
    def _():
        w = w_ref[0]
        for h in range(2 * n_pairs):
            wb_sc[h] = jnp.broadcast_to(w[:, h:h + 1], (tq, LANE))

    acc_sc[...] = jnp.zeros((tq, tk), F32)
    cw = min(tk, 2 * LANE)

    def pair(j, _):
        q = q_ref[0, j]
        wa = jnp.tile(wb_sc[2 * j], (1, cw // LANE))
        wb = jnp.tile(wb_sc[2 * j + 1], (1, cw // LANE))
        for c in range(tk // cw):
            cols = slice(c * cw, (c + 1) * cw)
            sa = lax.dot_general(q, klo_ref[0, cols, :], (((1,), (1,)), ((), ())), preferred_element_type=F32)
            sb = lax.dot_general(q, khi_ref[0, cols, :], (((1,), (1,)), ((), ())), preferred_element_type=F32)
            acc_sc[:, cols] += wa * jnp.maximum(sa, 0.0) + wb * jnp.maximum(sb, 0.0)
        return 0

    lax.fori_loop(0, n_pairs, pair, 0, unroll=4)
    score = acc_sc[...]
    row = qi * tq + lax.broadcasted_iota(I32, (tq, tk), 0)
    col = kj * tk + lax.broadcasted_iota(I32, (tq, tk), 1)
    score = jnp.where(col <= row, score, NEG)
    bits = pltpu.bitcast(score, I32)
    key = bits ^ ((bits >> 31) & 0x7FFFFFFF)
    sc_sc[kj] = key

    @pl.when((fl & 2) != 0)
    def _():
        n_chunks = kj + 1
        nh_bits = int(np.float32(NEG_HALF).view(np.int32))
        key_neg_half = nh_bits ^ 0x7FFFFFFF if nh_bits < 0 else nh_bits
        for g in range(tq // rg):
            rows = pl.ds(g * rg, rg)

            def bit_step(state):
                b, thr, n_ge, _ = state
                cand = thr + jnp.left_shift(jnp.int32(1), 31 - b)

                def count(c, cnt):
                    blk = sc_sc[c, rows, :]
                    for u in range(tk // LANE):
                        cnt = cnt + (blk[:, u * LANE:(u + 1) * LANE] >= cand).astype(I32)
                    return cnt

                cnt = lax.fori_loop(0, n_chunks, count, jnp.zeros((rg, LANE), I32))
                tot = jnp.sum(cnt, axis=1, keepdims=True)
                take = tot >= topk
                n_ge = jnp.where(take, tot, n_ge)
                return b + 1, jnp.where(take, cand, thr), n_ge, jnp.max(n_ge)

            start = (jnp.int32(0), jnp.full((rg, LANE), -2**31, I32),
                     jnp.full((rg, LANE), 2**30, I32), jnp.int32(2**30))
            _, thr, _, _ = lax.while_loop(lambda st: (st[0] < 32) & (st[3] > topk), bit_step, start)
            thr = jnp.maximum(thr, key_neg_half + 1)
            thr_t = jnp.tile(thr, (1, tk // LANE))

            def emit(c, _):
                o_ref[0, c, rows, :] = jnp.where(sc_sc[c, rows, :] >= thr_t, 0.0, NEG).astype(o_ref.dtype)
                return 0

            def emit_masked(c, _):
                o_ref[0, c, rows, :] = jnp.full((rg, tk), NEG, o_ref.dtype)
                return 0

            lax.fori_loop(0, n_chunks, emit, 0)
            lax.fori_loop(n_chunks, n_tiles, emit_masked, 0)


def _dsa(p, pos, qk_norm, idx_k_norm):
    B, S, _ = p.shape
    H, HKV, DH = DSA_HEADS, DSA_KV_HEADS, DSA_HEAD_DIM
    NP = IDX_HEADS // 2
    ts = _tile(S, 256, 8)
    cosf, sinf = _tables128(pos)
    cos4, sin4 = _tables64pair(pos)
    tab = pl.BlockSpec((1, ts, LANE), lambda b, i: (b, i, 0))
    kw = HKV * DH
    gi = jnp.concatenate([idx_k_norm, jnp.zeros((LANE - IDX_DIM,), F32)]).reshape(1, LANE)
    head_out = lambda n: pl.BlockSpec((1, n, ts, DH), lambda b, i: (b, 0, i, 0))
    q, k, v, qidx, kilo, kihi, wi = pl.pallas_call(
        functools.partial(_dsa_prep_kernel, H=H, HKV=HKV, scale=DH ** -0.5 * LOG2E,
                          wscale=IDX_HEADS ** -0.5 * IDX_DIM ** -0.5),
        out_shape=(jax.ShapeDtypeStruct((B, H, S, DH), CDT), jax.ShapeDtypeStruct((B, HKV, S, DH), CDT),
                   jax.ShapeDtypeStruct((B, HKV, S, DH), CDT), jax.ShapeDtypeStruct((B, NP, S, LANE), CDT),
                   jax.ShapeDtypeStruct((B, S, LANE), CDT), jax.ShapeDtypeStruct((B, S, LANE), CDT),
                   jax.ShapeDtypeStruct((B, S, LANE), F32)),
        grid=(B, S // ts),
        in_specs=[pl.BlockSpec((1, ts, H * DH), lambda b, i: (b, i, 0)),
                  pl.BlockSpec((1, ts, kw), lambda b, i: (b, i, H * DH // kw)),
                  pl.BlockSpec((1, ts, kw), lambda b, i: (b, i, H * DH // kw + 1)),
                  pl.BlockSpec((1, ts, NP * LANE), lambda b, i: (b, i, (H * DH + 2 * kw) // (NP * LANE))),
                  pl.BlockSpec((1, ts, LANE), lambda b, i: (b, i, (H * DH + 2 * kw + NP * LANE) // LANE)),
                  pl.BlockSpec((1, ts, LANE), lambda b, i: (b, i, (H * DH + 2 * kw + NP * LANE) // LANE + 1)),
                  tab, tab, tab, tab,
                  pl.BlockSpec((2, DH), lambda b, i: (0, 0)), pl.BlockSpec((1, LANE), lambda b, i: (0, 0))],
        out_specs=(head_out(H), head_out(HKV), head_out(HKV), head_out(NP), tab, tab, tab),
        compiler_params=_params(("parallel", "parallel")),
    )(p, p, p, p, p, p, cosf, sinf, cos4, sin4, qk_norm, gi)

    topk = min(DSA_TOPK_MAX, S // 4)
    tq, tk = _tile(S, 256, 8), _tile(S, 1024)
    qi_t, kj_t, fl_t = _pairs(S, tq, tk)
    bias = pl.pallas_call(
        functools.partial(_indexer_kernel, tq=tq, tk=tk, topk=topk, n_pairs=NP, rg=min(128, tq)),
        out_shape=jax.ShapeDtypeStruct((B, S // tk, S, tk), CDT),
        grid_spec=pltpu.PrefetchScalarGridSpec(
            num_scalar_prefetch=3,
            grid=(B, int(qi_t.shape[0])),
            in_specs=[pl.BlockSpec((1, NP, tq, LANE), lambda b, p, qi, kj, fl: (b, 0, qi[p], 0)),
                      pl.BlockSpec((1, tk, LANE), lambda b, p, qi, kj, fl: (b, kj[p], 0)),
                      pl.BlockSpec((1, tk, LANE), lambda b, p, qi, kj, fl: (b, kj[p], 0)),
                      pl.BlockSpec((1, tq, LANE), lambda b, p, qi, kj, fl: (b, qi[p], 0))],
            out_specs=pl.BlockSpec((1, S // tk, tq, tk), lambda b, p, qi, kj, fl: (b, 0, qi[p], 0)),
            scratch_shapes=[pltpu.VMEM((2 * NP, tq, LANE), F32), pltpu.VMEM((S // tk, tq, tk), I32),
                            pltpu.VMEM((tq, tk), F32)]),
        compiler_params=_params(("parallel", "arbitrary")),
    )(qi_t, kj_t, fl_t, qidx, kilo, kihi, wi)
    return _flash(q, k, v, mode="bias", bias=bias, tq=4 * tq, tk=tk)


def _router_kernel(l_ref, i_ref, p_ref, *, n_experts):
    x = l_ref[...]
    lane = lax.broadcasted_iota(I32, x.shape, 1)
    x = jnp.where(lane < n_experts, x, -jnp.inf)
    m1 = jnp.max(x, axis=1, keepdims=True)
    i1 = jnp.min(jnp.where(x == m1, lane, LANE), axis=1, keepdims=True)
    x2 = jnp.where(lane == i1, -jnp.inf, x)
    m2 = jnp.max(x2, axis=1, keepdims=True)
    i2 = jnp.min(jnp.where(x2 == m2, lane, LANE), axis=1, keepdims=True)
    e2 = jnp.exp(m2 - m1)
    p1 = 1.0 / (1.0 + e2)
    p2 = e2 / (1.0 + e2)
    i_ref[...] = jnp.where(lane == 0, i1, jnp.where(lane == 1, i2, 0))
    p_ref[...] = jnp.where(lane == 0, p1, jnp.where(lane == 1, p2, 0.0))


def _row_copy(src_hbm, row, dst_vmem, r, sem):
    return pltpu.make_async_copy(src_hbm.at[pl.ds(row, 1), :], dst_vmem.at[pl.ds(r, 1), :], sem)


def _gather_pipeline(n_steps, tm, copies):
    i = pl.program_id(0)
    slot = i % 2

    def start_tile(tile, slot):
        def body(r, _):
            for src, idx_ref, dst, sem in copies(slot):
                _row_copy(src, idx_ref[tile * tm + r], dst, r, sem).start()
            return 0
        lax.fori_loop(0, tm, body, 0)

    def wait_tile(slot):
        def body(r, _):
            for src, _, dst, sem in copies(slot):
                _row_copy(src, 0, dst, r, sem).wait()
            return 0
        lax.fori_loop(0, tm, body, 0)

    @pl.when(i == 0)
    def _():
        start_tile(0, 0)

    @pl.when(i + 1 < n_steps)
    def _():
        start_tile(i + 1, 1 - slot)

    wait_tile(slot)
    return slot


def _dispatch_kernel(tok_ref, h_hbm, o_ref, buf, sem, *, tm, n_steps):
    slot = _gather_pipeline(n_steps, tm, lambda s: [(h_hbm, tok_ref, buf.at[s], sem.at[s])])
    o_ref[...] = buf[slot].astype(o_ref.dtype)


def _combine_kernel(s1_ref, s2_ref, y_hbm, x_ref, g_ref, p_ref, o_ref, buf1, buf2, sem, *, tm, n_steps):
    slot = _gather_pipeline(n_steps, tm, lambda s: [(y_hbm, s1_ref, buf1.at[s], sem.at[s]),
                                                    (y_hbm, s2_ref, buf2.at[s], sem.at[s])])
    p = p_ref[...]
    o_ref[...] = x_ref[...] + g_ref[0] * (p[:, 0:1] * buf1[slot] + p[:, 1:2] * buf2[slot])


def _moe(h, h32, x, g_f, w_router, w_gate, w_up, w_down, S):
    N, D = h.shape
    E, _, DE = w_gate.shape
    wr = jnp.zeros((D, LANE), F32).at[:, :E].set(w_router).astype(CDT)
    logits = _mm(h, wr, tn=LANE, tk=D)
    tr = _tile(N, 1024, 8)
    spec = pl.BlockSpec((tr, LANE), lambda i: (i, 0))
    idx, prob = pl.pallas_call(
        functools.partial(_router_kernel, n_experts=E),
        out_shape=(jax.ShapeDtypeStruct((N, LANE), I32), jax.ShapeDtypeStruct((N, LANE), F32)),
        grid=(N // tr,),
        in_specs=[spec],
        out_specs=(spec, spec),
        compiler_params=_params(("parallel",)),
    )(logits)

    tm = min(512, N)
    n_rows = 2 * N + E * tm
    e_flat = jnp.concatenate([idx[:, 0], idx[:, 1]])
    onehot = (e_flat[:, None] == jnp.arange(E)[None, :]).astype(I32)
    csum = jnp.cumsum(onehot, axis=0)
    rank = jnp.take_along_axis(csum, e_flat[:, None], axis=1)[:, 0] - 1
    padded = (csum[-1] + tm - 1) // tm * tm
    ends = jnp.cumsum(padded)
    pos = (ends - padded)[e_flat] + rank
    tile_expert = jnp.minimum(jnp.sum(jnp.arange(n_rows // tm)[:, None] * tm >= ends[None, :], axis=1), E - 1)
    tok = jnp.tile(jnp.arange(N, dtype=I32), 2)
    row_token = jnp.zeros((n_rows,), I32).at[pos].set(tok)

    tg = min(256, N)
    xs = pl.pallas_call(
        functools.partial(_dispatch_kernel, tm=tg, n_steps=n_rows // tg),
        out_shape=jax.ShapeDtypeStruct((n_rows, D), CDT),
        grid_spec=pltpu.PrefetchScalarGridSpec(
            num_scalar_prefetch=1,
            grid=(n_rows // tg,),
            in_specs=[pl.BlockSpec(memory_space=pl.ANY)],
            out_specs=pl.BlockSpec((tg, D), lambda i, tok: (i, 0)),
            scratch_shapes=[pltpu.VMEM((2, tg, D), F32), pltpu.SemaphoreType.DMA((2,))]),
        compiler_params=_params(("arbitrary",)),
    )(row_token, h32)
    te = tile_expert.astype(I32)
    hid = _mm(xs, w_gate.astype(CDT), mode="swiglu", b2=w_up.astype(CDT), group=te, out_dtype=CDT,
              tm=tm, tn=DE, tk=1024)
    ys = _mm(hid, w_down.astype(CDT), group=te, tm=tm, tn=1024, tk=DE)
    return pl.pallas_call(
        functools.partial(_combine_kernel, tm=tg, n_steps=N // tg),
        out_shape=jax.ShapeDtypeStruct((N, D), F32),
        grid_spec=pltpu.PrefetchScalarGridSpec(
            num_scalar_prefetch=2,
            grid=(N // tg,),
            in_specs=[pl.BlockSpec(memory_space=pl.ANY),
                      pl.BlockSpec((tg, D), lambda i, s1, s2: (i, 0)),
                      pl.BlockSpec((1, 1, D), lambda i, s1, s2: ((i * tg) // S, 0, 0)),
                      pl.BlockSpec((tg, LANE), lambda i, s1, s2: (i, 0))],
            out_specs=pl.BlockSpec((tg, D), lambda i, s1, s2: (i, 0)),
            scratch_shapes=[pltpu.VMEM((2, tg, D), F32), pltpu.VMEM((2, tg, D), F32),
                            pltpu.SemaphoreType.DMA((2,))]),
        compiler_params=_params(("arbitrary",)),
    )(pos[:N].astype(I32), pos[N:].astype(I32), ys, x, g_f, prob)


def _pad_cols(blocks, total):
    cols = []
    for w, width in blocks:
        cols.append(w)
        if width > w.shape[1]:
            cols.append(jnp.zeros((w.shape[0], width - w.shape[1]), w.dtype))
    out = jnp.concatenate(cols, axis=1)
    if total > out.shape[1]:
        out = jnp.concatenate([out, jnp.zeros((out.shape[0], total - out.shape[1]), out.dtype)], axis=1)
    return out.astype(CDT)


def _round_up(n, m):
    return (n + m - 1) // m * m


def kernel(x, c, positions, ada_w, ada_b, ada_table, norm_g, ev_w_in, ev_w_out, mla_q_a_norm, mla_kv_a_norm, mla_w_uq, mla_w_ukv, mla_q_norm, mla_k_norm, nsa_qk_norm, nsa_cmp_pos, nsa_cmp_w, ffn_w_gate, ffn_w_up, ffn_w_down, od_w_in, od_w_out, dsa_qk_norm, idx_k_norm, moe_router, moe_w_gate, moe_w_up, moe_w_down):
    B, S, D = x.shape
    N = B * S
    depth = ada_table.shape[0]
    cond = _cond(c, ada_w, ada_b).reshape(B, 6, D)

    mla_in = MLA_Q_RANK + MLA_KV_RANK + MLA_ROPE
    nq = NSA_HEADS * NSA_HEAD_DIM
    nkv = 6 * NSA_KV_GROUPS * NSA_HEAD_DIM
    hn = np.arange(NSA_HEADS)
    gate_perm = np.concatenate([hn * 3 + r for r in range(3)])

    x2 = x.reshape(N, D)
    for l in range(depth):
        i = l // 2
        mod = cond + ada_table[l]
        sh_a, sc_a, g_a, sh_f, sc_f, g_f = [mod[:, j, None, :] for j in range(6)]
        h = _norm(x2.reshape(B, S, D), norm_g[l, 0], sc_a, sh_a).reshape(N, D)
        if l % 2 == 0:
            w = ev_w_in[i]
            nsa = w[:, mla_in:]
            blocks = [(w[:, :MLA_Q_RANK + MLA_KV_RANK], MLA_Q_RANK + MLA_KV_RANK),
                      (nsa[:, :nq + nkv], nq + nkv),
                      (w[:, MLA_Q_RANK + MLA_KV_RANK:mla_in], LANE),
                      (nsa[:, nq + nkv:][:, gate_perm], LANE)]
            width = MLA_Q_RANK + MLA_KV_RANK + nq + nkv + 2 * LANE
            w_in = _pad_cols(blocks, _round_up(width, 512))
            p = _mm(h, w_in).reshape(B, S, -1)
            off = MLA_Q_RANK + MLA_KV_RANK
            a_out = _mla(p, (off + nq + nkv) // LANE, positions, mla_q_a_norm[i], mla_kv_a_norm[i],
                         mla_w_uq[i], mla_w_ukv[i], mla_q_norm[i], mla_k_norm[i])
            b_out = _nsa(p, off // nq, (off + nq) // (NSA_KV_GROUPS * NSA_HEAD_DIM),
                         (off + nq + nkv) // LANE + 1, positions, nsa_qk_norm[i], nsa_cmp_pos[i], nsa_cmp_w[i])
            mix = (a_out.reshape(N, -1), b_out.reshape(N, -1))
            w_out = ev_w_out[i]
        else:
            w = od_w_in[i]
            main = DSA_HEADS * DSA_HEAD_DIM + 2 * DSA_KV_HEADS * DSA_HEAD_DIM + IDX_HEADS * IDX_DIM
            blocks = [(w[:, :main], main), (w[:, main:main + IDX_DIM], LANE), (w[:, main + IDX_DIM:], LANE)]
            w_in = _pad_cols(blocks, _round_up(main + 2 * LANE, 512))
            p = _mm(h, w_in).reshape(B, S, -1)
            mix = _dsa(p, positions, dsa_qk_norm[i], idx_k_norm[i]).reshape(N, -1)
            w_out = od_w_out[i]
        x2 = _mm(mix, w_out.astype(CDT), mode="res", x=x2, g=g_a, rows_per_batch=S)
        if l % 2 == 0:
            h = _norm(x2.reshape(B, S, D), norm_g[l, 1], sc_f, sh_f).reshape(N, D)
            hid = _mm(h, ffn_w_gate[i].astype(CDT), mode="swiglu", b2=ffn_w_up[i].astype(CDT), out_dtype=CDT)
            x2 = _mm(hid, ffn_w_down[i].astype(CDT), mode="res", x=x2, g=g_f, tn=1024, tk=2048,
                     rows_per_batch=S)
        else:
            h, h32 = _norm(x2.reshape(B, S, D), norm_g[l, 1], sc_f, sh_f, also_f32=True)
            x2 = _moe(h.reshape(N, D), h32.reshape(N, D), x2, g_f, moe_router[i], moe_w_gate[i], moe_w_up[i],
                      moe_w_down[i], S)
    return x2.reshape(B, S, D)
```

```python
import functools

import numpy as np
import jax
import jax.numpy as jnp
from jax import lax
from jax.experimental import pallas as pl
from jax.experimental.pallas import tpu as pltpu

F32 = jnp.float32
I32 = jnp.int32
CDT = jnp.bfloat16

ROPE_THETA = 10000.0
EPS = 1e-6
NEG = -1e30
NEG_HALF = -5e29
FORCE = 1e9
REMOVED = -3e38
LOG2E = 1.4426950408889634

MLA_HEADS, MLA_Q_RANK, MLA_KV_RANK, MLA_NOPE, MLA_ROPE, MLA_V = 16, 1536, 512, 128, 64, 128
NSA_HEADS, NSA_KV_GROUPS, NSA_HEAD_DIM = 16, 4, 128
NSA_CMP_LEN, NSA_CMP_STRIDE, NSA_SLC_LEN, NSA_N_SEL, NSA_WINDOW = 32, 16, 64, 16, 512
DSA_HEADS, DSA_KV_HEADS, DSA_HEAD_DIM, IDX_HEADS, IDX_DIM, DSA_TOPK_MAX = 32, 8, 128, 32, 64, 256
N_EXPERTS = 8

LANE = 128
VMEM_LIMIT = 56 * 2**20


def _tile(n, pref, mult=LANE):
    if n <= pref:
        return n
    t = (pref // mult) * mult
    while t >= mult:
        if n % t == 0:
            return t
        t -= mult
    return n


def _params(sem):
    return pltpu.CompilerParams(dimension_semantics=sem, vmem_limit_bytes=VMEM_LIMIT)


def _silu(x):
    return x / (1.0 + jnp.exp(-x))


def _sigmoid(x):
    return 1.0 / (1.0 + jnp.exp(-x))


def _rms(x, g):
    return x * lax.rsqrt(jnp.mean(x * x, axis=-1, keepdims=True) + EPS) * g


def _rope128(y, cosf, sinf):
    return y * cosf + pltpu.roll(y, 64, 1) * sinf


def _rope64pair(y, cos4, sin4, lane):
    rot = jnp.where((lane & 63) < 32, pltpu.roll(y, 96, 1), pltpu.roll(y, 32, 1))
    return y * cos4 + rot * sin4


def _mm_kernel(*refs, nk, mode, grouped, split_a):
    if grouped:
        refs = refs[1:]
    k = pl.program_id(2)
    a_refs = refs[:2] if split_a else refs[:1]
    if split_a:
        refs = refs[1:]
    n_in = {"plain": 2, "res": 4, "swiglu": 3}[mode]
    o_ref = refs[n_in]
    acc, acc2 = (tuple(refs[n_in + 1:]) + (None, None))[:2]
    if mode == "plain":
        b_ref, = refs[1:n_in]
    elif mode == "res":
        b_ref, x_ref, g_ref = refs[1:n_in]
    else:
        b_ref, b2_ref = refs[1:n_in]

    def product(w_ref):
        def with_a(r):
            return jnp.dot(r[...], w_ref[...].astype(r.dtype), preferred_element_type=F32)
        if split_a:
            return lax.cond(k == 0, lambda: with_a(a_refs[0]), lambda: with_a(a_refs[1]))
        return with_a(a_refs[0])

    part = product(b_ref)
    if mode == "swiglu":
        part2 = product(b2_ref)

    if nk > 1:
        @pl.when(k == 0)
        def _():
            acc[...] = part
            if mode == "swiglu":
                acc2[...] = part2

        @pl.when(k > 0)
        def _():
            acc[...] += part
            if mode == "swiglu":
                acc2[...] += part2

    def finish():
        r = acc[...] if nk > 1 else part
        if mode == "plain":
            o_ref[...] = r.astype(o_ref.dtype)
        elif mode == "res":
            o_ref[...] = (x_ref[...] + g_ref[0] * r).astype(o_ref.dtype)
        else:
            r2 = acc2[...] if nk > 1 else part2
            o_ref[...] = (_silu(r) * r2).astype(o_ref.dtype)

    if nk > 1:
        pl.when(k == nk - 1)(finish)
    else:
        finish()


def _mm(a, b, *, mode="plain", b2=None, x=None, g=None, group=None, out_dtype=F32,
        tm=1024, tn=512, tk=4096, rows_per_batch=None):
    split_a = isinstance(a, tuple)
    a_list = list(a) if split_a else [a]
    M, Ka = a_list[0].shape
    K = Ka * len(a_list)
    N = b.shape[-1]
    tm, tn, tk = _tile(M, tm, 8), _tile(N, tn), (Ka if split_a else _tile(K, tk))
    nk = K // tk
    grid = (M // tm, N // tn, nk)
    grouped = group is not None
    if grouped:
        assert group.shape == (M // tm,)
        b_spec = pl.BlockSpec((None, tk, tn), lambda i, j, k, ge: (ge[i], k, j))
    else:
        b_spec = pl.BlockSpec((tk, tn), lambda i, j, k, *_: (k, j))
    if split_a:
        in_specs = [pl.BlockSpec((tm, tk), lambda i, j, k, *_: (i, 0))] * 2 + [b_spec]
    else:
        in_specs = [pl.BlockSpec((tm, tk), lambda i, j, k, *_: (i, k)), b_spec]
    args = a_list + [b]
    scratch = [pltpu.VMEM((tm, tn), F32)] if nk > 1 else []
    if mode == "swiglu":
        in_specs.append(b_spec)
        args.append(b2)
        scratch = scratch * 2
    if mode == "res":
        rpb = rows_per_batch
        assert rpb % tm == 0
        in_specs += [pl.BlockSpec((tm, tn), lambda i, j, k, *_: (i, j)),
                     pl.BlockSpec((1, 1, tn), lambda i, j, k, *_: ((i * tm) // rpb, 0, j))]
        args += [x, g]
    return pl.pallas_call(
        functools.partial(_mm_kernel, nk=nk, mode=mode, grouped=grouped, split_a=split_a),
        out_shape=jax.ShapeDtypeStruct((M, N), out_dtype),
        grid_spec=pltpu.PrefetchScalarGridSpec(
            num_scalar_prefetch=1 if grouped else 0,
            grid=grid,
            in_specs=in_specs,
            out_specs=pl.BlockSpec((tm, tn), lambda i, j, k, *_: (i, j)),
            scratch_shapes=scratch),
        compiler_params=_params(("parallel", "parallel", "arbitrary")),
    )(*(([group] if grouped else []) + args))


def _cond_kernel(c_ref, w_ref, b_ref, o_ref):
    a = _silu(c_ref[...]).astype(CDT)
    o_ref[...] = jnp.dot(a, w_ref[...].astype(CDT), preferred_element_type=F32) + b_ref[...]


def _cond(c, ada_w, ada_b):
    B, D = c.shape
    N = ada_w.shape[1]
    cp = jnp.zeros((8, D), F32).at[:B].set(c)
    tn = _tile(N, 512)
    out = pl.pallas_call(
        _cond_kernel,
        out_shape=jax.ShapeDtypeStruct((8, N), F32),
        grid=(N // tn,),
        in_specs=[pl.BlockSpec((8, D), lambda j: (0, 0)),
                  pl.BlockSpec((D, tn), lambda j: (0, j)),
                  pl.BlockSpec((1, tn), lambda j: (0, j))],
        out_specs=pl.BlockSpec((8, tn), lambda j: (0, j)),
        compiler_params=_params(("parallel",)),
    )(cp, ada_w, ada_b.reshape(1, N))
    return out[:B]


def _norm_kernel(*refs, modulate, n_out):
    outs = refs[len(refs) - n_out:]
    if modulate:
        x_ref, g_ref, sc_ref, sh_ref = refs[:4]
    else:
        x_ref, g_ref = refs[:2]
    y = _rms(x_ref[0], g_ref[...])
    if modulate:
        y = y * (1.0 + sc_ref[0]) + sh_ref[0]
    for o_ref in outs:
        o_ref[0] = y.astype(o_ref.dtype)


def _norm(x, g, sc=None, sh=None, *, width=None, col_block=0, ts=512, also_f32=False):
    B, S, W = x.shape
    width = W if width is None else width
    ts = _tile(S, ts, 8)
    modulate = sc is not None
    in_specs = [pl.BlockSpec((1, ts, width), lambda b, i: (b, i, col_block)),
                pl.BlockSpec((1, width), lambda b, i: (0, 0))]
    args = [x, g.reshape(1, width)]
    if modulate:
        in_specs += [pl.BlockSpec((1, 1, width), lambda b, i: (b, 0, 0))] * 2
        args += [sc, sh]
    dtypes = (CDT, F32) if also_f32 else (CDT,)
    out = pl.pallas_call(
        functools.partial(_norm_kernel, modulate=modulate, n_out=len(dtypes)),
        out_shape=tuple(jax.ShapeDtypeStruct((B, S, width), d) for d in dtypes),
        grid=(B, S // ts),
        in_specs=in_specs,
        out_specs=tuple(pl.BlockSpec((1, ts, width), lambda b, i: (b, i, 0)) for _ in dtypes),
        compiler_params=_params(("parallel", "parallel")),
    )(*args)
    return out if also_f32 else out[0]


def _rope_angles(pos, dim):
    inv = ROPE_THETA ** (-jnp.arange(0, dim, 2, dtype=F32) / dim)
    ang = pos.astype(F32)[..., None] * inv
    return jnp.cos(ang), jnp.sin(ang)


def _tables128(pos):
    c, s = _rope_angles(pos, 128)
    return jnp.concatenate([c, c], -1), jnp.concatenate([-s, s], -1)


def _tables64pair(pos):
    c, s = _rope_angles(pos, 64)
    return jnp.concatenate([c, c, c, c], -1), jnp.concatenate([-s, s, -s, s], -1)


def _pairs(S, tq, tk, window=None):
    qi, kj, fl = [], [], []
    for i in range(S // tq):
        lo = 0 if window is None else max(0, i * tq - window + 1)
        js = list(range(lo // tk, (i * tq + tq - 1) // tk + 1))
        for n, j in enumerate(js):
            diag = (j + 1) * tk - 1 > i * tq
            f = (1 if n == 0 else 0) | (2 if n == len(js) - 1 else 0)
            f |= 4 if (diag or window is not None) else 0
            qi.append(i), kj.append(j), fl.append(f)
    return (jnp.asarray(qi, I32), jnp.asarray(kj, I32), jnp.asarray(fl, I32))


def _flash_kernel(qi_ref, kj_ref, fl_ref, q_ref, k_ref, v_ref, *rest, G, tq, tk, rb, mode, window):
    if mode == "bias":
        bias_ref, o_ref, m_sc, acc_sc = rest
    elif mode == "sel":
        sel_ref, e_ref, o_ref, m_sc, acc_sc = rest
    else:
        o_ref, m_sc, acc_sc = rest
    p = pl.program_id(2)
    qi, kj, fl = qi_ref[p], kj_ref[p], fl_ref[p]
    dk = q_ref.shape[-1]
    dv = v_ref.shape[-1]
    R = G * tq

    @pl.when((fl & 1) != 0)
    def _():
        m_sc[...] = jnp.full(m_sc.shape, NEG, F32)
        acc_sc[...] = jnp.zeros(acc_sc.shape, F32)

    def compute(position_mask):
        mask = None
        if position_mask:
            row = qi * tq + lax.broadcasted_iota(I32, (tq, tk), 0)
            col = kj * tk + lax.broadcasted_iota(I32, (tq, tk), 1)
            mask = col <= row
            if window is not None:
                mask = mask & (col > row - window)
        if mode == "sel":
            hit = jnp.dot(sel_ref[0, 0], e_ref[...], preferred_element_type=F32) > 0.5
            mask = hit if mask is None else (mask & hit)
        ones = jnp.ones((tk, LANE), v_ref.dtype)
        v_exts = [jnp.concatenate([v_ref[0, h], ones], axis=1) for h in range(v_ref.shape[1])]
        for r in range(R // rb):
            g, t0 = divmod(r * rb, tq)
            kv = g % k_ref.shape[1]
            v_ext = v_exts[kv]
            rows = slice(r * rb, (r + 1) * rb)
            s = lax.dot_general(q_ref[0, g, t0:t0 + rb, :], k_ref[0, kv], (((1,), (1,)), ((), ())),
                                preferred_element_type=F32)
            if mode == "bias":
                s = s + bias_ref[0, 0, t0:t0 + rb, :].astype(F32)
            if mask is not None:
                s = jnp.where(mask[t0:t0 + rb], s, NEG)
            m_prev = m_sc[rows, :]
            m_new = jnp.maximum(m_prev, jnp.max(s, axis=-1, keepdims=True))
            alpha = jnp.exp2(m_prev - m_new)
            pr = jnp.concatenate([jnp.exp2(s[:, c * LANE:(c + 1) * LANE] - m_new).astype(v_ref.dtype)
                                  for c in range(tk // LANE)], axis=1)
            pv = jnp.dot(pr, v_ext, preferred_element_type=F32)
            acc_sc[rows, :] = jnp.concatenate([alpha] * (dv // LANE + 1), axis=1) * acc_sc[rows, :] + pv
            m_sc[rows, :] = m_new

    if mode == "bias":
        compute(False)
    else:
        pl.when((fl & 4) != 0)(functools.partial(compute, True))
        pl.when((fl & 4) == 0)(functools.partial(compute, False))

    @pl.when((fl & 2) != 0)
    def _():
        for g in range(G):
            rows = slice(g * tq, (g + 1) * tq)
            o = acc_sc[rows, 0:dv] / acc_sc[rows, dv:dv + LANE]
            o_ref[0, :, g * dv:(g + 1) * dv] = o.astype(o_ref.dtype)


def _flash(q, k, v, *, mode="causal", window=None, bias=None, sel=None, expand=None,
           tq=512, tk=512, rb=128, heads_per_step=1, out_dtype=None):
    B, Hq, S, dk = q.shape
    Hkv, dv = k.shape[1], v.shape[-1]
    G = Hq // Hkv
    kvh = 1
    if heads_per_step > 1:
        assert G == 1 and Hq % heads_per_step == 0
        G = kvh = heads_per_step
    tq, tk = _tile(S, tq, 8), _tile(S, tk)
    qi, kj, fl = _pairs(S, tq, tk, window)
    P = int(qi.shape[0])
    in_specs = [pl.BlockSpec((1, G, tq, dk), lambda b, h, p, qi, kj, fl: (b, h, qi[p], 0)),
                pl.BlockSpec((1, kvh, tk, dk), lambda b, h, p, qi, kj, fl: (b, h, kj[p], 0)),
                pl.BlockSpec((1, kvh, tk, dv), lambda b, h, p, qi, kj, fl: (b, h, kj[p], 0))]
    args = [q, k, v]
    if mode == "bias":
        in_specs.append(pl.BlockSpec((1, 1, tq, tk), lambda b, h, p, qi, kj, fl: (b, kj[p], qi[p], 0)))
        args.append(bias)
    elif mode == "sel":
        in_specs += [pl.BlockSpec((1, 1, tq, LANE), lambda b, h, p, qi, kj, fl: (b, h, qi[p], 0)),
                     pl.BlockSpec((LANE, tk), lambda b, h, p, qi, kj, fl: (0, kj[p]))]
        args += [sel, expand]
    kern = functools.partial(_flash_kernel, G=G, tq=tq, tk=tk, rb=min(rb, tq), mode=mode, window=window)
    return pl.pallas_call(
        kern,
        out_shape=jax.ShapeDtypeStruct((B, S, Hq * dv), CDT if out_dtype is None else out_dtype),
        grid_spec=pltpu.PrefetchScalarGridSpec(
            num_scalar_prefetch=3,
            grid=(B, Hkv // kvh, P),
            in_specs=in_specs,
            out_specs=pl.BlockSpec((1, tq, G * dv), lambda b, h, p, qi, kj, fl: (b, qi[p], h)),
            scratch_shapes=[pltpu.VMEM((G * tq, LANE), F32), pltpu.VMEM((G * tq, dv + LANE), F32)]),
        compiler_params=_params(("parallel", "parallel", "arbitrary")),
    )(qi, kj, fl, *args)


def _mla_qprep_kernel(x_ref, cos_ref, sin_ref, gn_ref, gr_ref, o_ref, *, H, scale):
    ts = x_ref.shape[1]
    lane = lax.broadcasted_iota(I32, (ts, LANE), 1)
    lo = lane < 64
    cos4, sin4 = cos_ref[0], sin_ref[0]
    for h in range(H):
        xn = x_ref[0, :, h * LANE:(h + 1) * LANE]
        o_ref[0, h, :, 0:LANE] = (_rms(xn, gn_ref[...]) * scale).astype(o_ref.dtype)
    for j in range(H // 2):
        xr = x_ref[0, :, (H + j) * LANE:(H + j + 1) * LANE]
        ss = xr * xr
        s_lo = jnp.sum(jnp.where(lo, ss, 0.0), axis=-1, keepdims=True)
        s_hi = jnp.sum(jnp.where(lo, 0.0, ss), axis=-1, keepdims=True)
        inv = jnp.where(lo, lax.rsqrt(s_lo / 64.0 + EPS), lax.rsqrt(s_hi / 64.0 + EPS))
        r = _rope64pair(xr * inv * gr_ref[...], cos4, sin4, lane) * scale
        o_ref[0, 2 * j, :, LANE:2 * LANE] = jnp.where(lo, r, 0.0).astype(o_ref.dtype)
        o_ref[0, 2 * j + 1, :, LANE:2 * LANE] = jnp.where(lo, pltpu.roll(r, 64, 1), 0.0).astype(o_ref.dtype)


def _mla_kvprep_kernel(x_ref, kr_ref, cos_ref, sin_ref, gn_ref, gr_ref, k_ref, v_ref, *, H):
    ts = x_ref.shape[1]
    lane = lax.broadcasted_iota(I32, (ts, LANE), 1)
    kr = kr_ref[0]
    inv = lax.rsqrt(jnp.sum(kr * kr, axis=-1, keepdims=True) / 64.0 + EPS)
    r = _rope64pair(kr * inv * gr_ref[...], cos_ref[0], sin_ref[0], lane).astype(k_ref.dtype)
    for h in range(H):
        xn = x_ref[0, :, h * LANE:(h + 1) * LANE]
        k_ref[0, h, :, 0:LANE] = _rms(xn, gn_ref[...]).astype(k_ref.dtype)
        k_ref[0, h, :, LANE:2 * LANE] = r
        v_ref[0, h] = x_ref[0, :, (H + h) * LANE:(H + h + 1) * LANE].astype(v_ref.dtype)


def _mla(p, kr_block, pos, q_a_norm, kv_a_norm, w_uq, w_ukv, q_norm, k_norm):
    B, S, _ = p.shape
    H = MLA_HEADS
    N = B * S
    cqn = _norm(p, q_a_norm, width=MLA_Q_RANK, col_block=0)
    ckvn = _norm(p, kv_a_norm, width=MLA_KV_RANK, col_block=MLA_Q_RANK // MLA_KV_RANK)
    hh = np.arange(H)[:, None]
    q_perm = np.concatenate([(hh * 192 + np.arange(128)).ravel(), (hh * 192 + 128 + np.arange(64)).ravel()])
    kv_perm = np.concatenate([(hh * 256 + np.arange(128)).ravel(), (hh * 256 + 128 + np.arange(128)).ravel()])
    q_raw = _mm(cqn.reshape(N, -1), w_uq[:, q_perm].astype(CDT), tn=1024, tk=MLA_Q_RANK).reshape(B, S, -1)
    kv_raw = _mm(ckvn.reshape(N, -1), w_ukv[:, kv_perm].astype(CDT), tn=1024, tk=MLA_KV_RANK).reshape(B, S, -1)
    cos4, sin4 = _tables64pair(pos)
    ts = _tile(S, 256, 8)
    scale = (MLA_NOPE + MLA_ROPE) ** -0.5 * LOG2E
    gr = q_norm[MLA_NOPE:]
    tab = pl.BlockSpec((1, ts, LANE), lambda b, i: (b, i, 0))
    vec = pl.BlockSpec((1, LANE), lambda b, i: (0, 0))
    q = pl.pallas_call(
        functools.partial(_mla_qprep_kernel, H=H, scale=scale),
        out_shape=jax.ShapeDtypeStruct((B, H, S, 2 * LANE), CDT),
        grid=(B, S // ts),
        in_specs=[pl.BlockSpec((1, ts, H * 192), lambda b, i: (b, i, 0)), tab, tab, vec, vec],
        out_specs=pl.BlockSpec((1, H, ts, 2 * LANE), lambda b, i: (b, 0, i, 0)),
        compiler_params=_params(("parallel", "parallel")),
    )(q_raw, cos4, sin4, q_norm[:MLA_NOPE].reshape(1, -1), jnp.concatenate([gr, gr]).reshape(1, -1))
    gkr = jnp.concatenate([k_norm[MLA_NOPE:], jnp.zeros((64,), F32)])
    k, v = pl.pallas_call(
        functools.partial(_mla_kvprep_kernel, H=H),
        out_shape=(jax.ShapeDtypeStruct((B, H, S, 2 * LANE), CDT),
                   jax.ShapeDtypeStruct((B, H, S, LANE), CDT)),
        grid=(B, S // ts),
        in_specs=[pl.BlockSpec((1, ts, H * 256), lambda b, i: (b, i, 0)),
                  pl.BlockSpec((1, ts, LANE), lambda b, i: (b, i, kr_block)), tab, tab, vec, vec],
        out_specs=(pl.BlockSpec((1, H, ts, 2 * LANE), lambda b, i: (b, 0, i, 0)),
                   pl.BlockSpec((1, H, ts, LANE), lambda b, i: (b, 0, i, 0))),
        compiler_params=_params(("parallel", "parallel")),
    )(kv_raw, p, cos4, sin4, k_norm[:MLA_NOPE].reshape(1, -1), gkr.reshape(1, -1))
    return _flash(q, k, v, mode="causal", tq=1024, tk=1024, rb=256, heads_per_step=4)


def _nsa_prep_kernel(q_ref, kc_ref, vc_ref, ks_ref, vs_ref, kw_ref, vw_ref, cos_ref, sin_ref, g_ref,
                     qo, kso, vso, kwo, vwo, kco, vco, *, H, G, scale):
    cosf, sinf = cos_ref[0], sin_ref[0]
    for h in range(H):
        y = _rms(q_ref[0, :, h * LANE:(h + 1) * LANE], g_ref[0:1, :])
        qo[0, h] = (_rope128(y, cosf, sinf) * scale).astype(qo.dtype)
    for g in range(G):
        sl = slice(g * LANE, (g + 1) * LANE)
        kso[0, g] = _rope128(_rms(ks_ref[0, :, sl], g_ref[2:3, :]), cosf, sinf).astype(kso.dtype)
        kwo[0, g] = _rope128(_rms(kw_ref[0, :, sl], g_ref[3:4, :]), cosf, sinf).astype(kwo.dtype)
        vso[0, g] = vs_ref[0, :, sl].astype(vso.dtype)
        vwo[0, g] = vw_ref[0, :, sl].astype(vwo.dtype)
        kco[0, g] = kc_ref[0, :, sl].astype(kco.dtype)
        vco[0, g] = vc_ref[0, :, sl].astype(vco.dtype)


def _compress_kernel(xk_ref, xv_ref, wk_ref, wv_ref, pek_ref, pev_ref, g_ref, cos_ref, sin_ref,
                     kc_ref, vc_ref):
    nc = xk_ref.shape[2]

    def comp(x_ref, w_ref, pe_ref):
        y = jnp.dot(x_ref[0, 0], w_ref[...], preferred_element_type=F32)
        c = jnp.dot(pe_ref[...], w_ref[...], preferred_element_type=F32)
        const = c[0:1, 0:LANE] + c[1:2, LANE:2 * LANE]
        return y[:, 0:LANE] + pltpu.roll(y[:, LANE:2 * LANE], nc - 1, 0) + const

    kc = comp(xk_ref, wk_ref, pek_ref)
    kc_ref[0, 0] = _rope128(_rms(kc, g_ref[...]), cos_ref[0], sin_ref[0]).astype(kc_ref.dtype)
    vc_ref[0, 0] = comp(xv_ref, wv_ref, pev_ref).astype(vc_ref.dtype)


def _cmp_attn_kernel(q_ref, kc_ref, vc_ref, m_ref, o_ref, sel_ref, *, G, tq, n_cmp, n_sel, dv):
    i = pl.program_id(2)
    nc = kc_ref.shape[2]
    q = q_ref[0].reshape(G * tq, q_ref.shape[-1])
    s = lax.dot_general(q, kc_ref[0, 0], (((1,), (1,)), ((), ())), preferred_element_type=F32)
    s = s.reshape(G, tq, nc)
    t = i * tq + lax.broadcasted_iota(I32, (tq, nc), 0)
    n = lax.broadcasted_iota(I32, (tq, nc), 1)
    mask = (n * NSA_CMP_STRIDE + (NSA_CMP_LEN - 1) <= t) & (n < n_cmp)
    s = jnp.where(mask[None], s, NEG)
    mx = jnp.max(s, axis=-1, keepdims=True)
    e = jnp.where(mask[None], jnp.exp2(s - mx), 0.0)
    l = jnp.sum(e, axis=-1, keepdims=True)
    pc = e / jnp.where(l > 0.0, l, 1.0)
    o = jnp.dot(pc.reshape(G * tq, nc).astype(vc_ref.dtype), vc_ref[0, 0], preferred_element_type=F32)
    for g in range(G):
        o_ref[0, :, g * dv:(g + 1) * dv] = o[g * tq:(g + 1) * tq].astype(o_ref.dtype)

    ps = jnp.sum(pc, axis=0)
    hi = ps.astype(CDT)
    lo_part = (ps - hi.astype(F32)).astype(CDT)
    imp = (jnp.dot(hi, m_ref[...], preferred_element_type=F32)
           + jnp.dot(lo_part, m_ref[...], preferred_element_type=F32))
    blk = lax.broadcasted_iota(I32, (tq, LANE), 1)
    cur = (i * tq + lax.broadcasted_iota(I32, (tq, LANE), 0)) >> (NSA_SLC_LEN.bit_length() - 1)
    forced = (blk == 0) | (blk == cur) | (blk == cur - 1)
    imp = jnp.where(forced, FORCE, jnp.where(blk <= cur, imp, NEG))
    v = imp.T
    rowi = lax.broadcasted_iota(I32, (LANE, tq), 0)

    def take(_, carry):
        v, chosen = carry
        mval = jnp.max(v, axis=0, keepdims=True)
        first = jnp.min(jnp.where(v == mval, rowi, LANE), axis=0, keepdims=True)
        hit = rowi == first
        chosen = jnp.where(hit & (mval > NEG_HALF), 1.0, chosen)
        return jnp.where(hit, REMOVED, v), chosen

    _, chosen = lax.fori_loop(0, n_sel, take, (v, jnp.zeros((LANE, tq), F32)))
    sel_ref[0, 0] = chosen.T.astype(sel_ref.dtype)


def _nsa_combine_kernel(oc_ref, os_ref, ow_ref, g_ref, o_ref, *, H, dv):
    gate = _sigmoid(g_ref[0])
    for h in range(H):
        sl = slice(h * dv, (h + 1) * dv)
        o = (oc_ref[0, :, sl] * gate[:, h:h + 1] + os_ref[0, :, sl] * gate[:, H + h:H + h + 1]
             + ow_ref[0, :, sl] * gate[:, 2 * H + h:2 * H + h + 1])
        o_ref[0, :, sl] = o.astype(o_ref.dtype)


def _nsa(p, q_block, kv_block0, gate_block, pos, qk_norm, cmp_pos, cmp_w):
    B, S, _ = p.shape
    H, G, DH = NSA_HEADS, NSA_KV_GROUPS, NSA_HEAD_DIM
    HPG = H // G
    scale = DH ** -0.5 * LOG2E
    ts = _tile(S, 256, 8)
    cosf, sinf = _tables128(pos)
    tab = pl.BlockSpec((1, ts, LANE), lambda b, i: (b, i, 0))
    kvspec = [pl.BlockSpec((1, ts, G * DH), functools.partial(lambda b, i, m: (b, i, kv_block0 + m), m=m))
              for m in range(6)]
    head_out = lambda n: pl.BlockSpec((1, n, ts, DH), lambda b, i: (b, 0, i, 0))
    kv_shape = jax.ShapeDtypeStruct((B, G, S, DH), CDT)
    q, ks, vs, kw, vw, kcr, vcr = pl.pallas_call(
        functools.partial(_nsa_prep_kernel, H=H, G=G, scale=scale),
        out_shape=(jax.ShapeDtypeStruct((B, H, S, DH), CDT),) + (kv_shape,) * 6,
        grid=(B, S // ts),
        in_specs=[pl.BlockSpec((1, ts, H * DH), lambda b, i: (b, i, q_block))] + kvspec
                 + [tab, tab, pl.BlockSpec((4, DH), lambda b, i: (0, 0))],
        out_specs=(head_out(H),) + (head_out(G),) * 6,
        compiler_params=_params(("parallel", "parallel")),
    )(p, p, p, p, p, p, p, cosf, sinf, qk_norm)

    half = NSA_CMP_LEN // 2
    nc = S // NSA_CMP_STRIDE
    n_cmp = (S - NSA_CMP_LEN) // NSA_CMP_STRIDE + 1
    cmp_end = jnp.minimum(jnp.arange(nc) * NSA_CMP_STRIDE + NSA_CMP_LEN - 1, S - 1)
    ccos, csin = _tables128(pos[:, cmp_end])
    wcat = lambda w: jnp.concatenate([w[:half].reshape(half * DH, DH), w[half:].reshape(half * DH, DH)], 1).astype(CDT)
    pecat = lambda pe: jnp.zeros((8, half * DH), F32).at[0].set(pe[:half].reshape(-1)).at[1].set(
        pe[half:].reshape(-1)).astype(CDT)
    xspec = pl.BlockSpec((1, 1, nc, half * DH), lambda b, g: (b, g, 0, 0))
    wspec = pl.BlockSpec((half * DH, 2 * DH), lambda b, g: (0, 0))
    pespec = pl.BlockSpec((8, half * DH), lambda b, g: (0, 0))
    cspec = pl.BlockSpec((1, 1, nc, DH), lambda b, g: (b, g, 0, 0))
    ctab = pl.BlockSpec((1, nc, DH), lambda b, g: (b, 0, 0))
    kc, vc = pl.pallas_call(
        _compress_kernel,
        out_shape=(jax.ShapeDtypeStruct((B, G, nc, DH), CDT),) * 2,
        grid=(B, G),
        in_specs=[xspec, xspec, wspec, wspec, pespec, pespec,
                  pl.BlockSpec((1, DH), lambda b, g: (0, 0)), ctab, ctab],
        out_specs=(cspec, cspec),
        compiler_params=_params(("parallel", "parallel")),
    )(kcr.reshape(B, G, nc, half * DH), vcr.reshape(B, G, nc, half * DH), wcat(cmp_w[0]), wcat(cmp_w[1]),
      pecat(cmp_pos[0]), pecat(cmp_pos[1]), qk_norm[1].reshape(1, DH), ccos, csin)

    n_slc = S // NSA_SLC_LEN
    assert n_slc <= LANE
    r, cl = NSA_SLC_LEN // NSA_CMP_STRIDE, NSA_CMP_LEN // NSA_CMP_STRIDE
    m_np = np.zeros((nc, LANE), np.float32)
    for j in range(n_slc):
        for a in range(r):
            for c in range(cl):
                ci = j * r + a + c - (cl - 1)
                if 0 <= ci < n_cmp:
                    m_np[ci, j] += 1.0
    tq = _tile(S, 512, 8)
    o_c, sel = pl.pallas_call(
        functools.partial(_cmp_attn_kernel, G=HPG, tq=tq, n_cmp=n_cmp, n_sel=min(NSA_N_SEL, n_slc), dv=DH),
        out_shape=(jax.ShapeDtypeStruct((B, S, H * DH), CDT),
                   jax.ShapeDtypeStruct((B, G, S, LANE), CDT)),
        grid=(B, G, S // tq),
        in_specs=[pl.BlockSpec((1, HPG, tq, DH), lambda b, g, i: (b, g, i, 0)),
                  pl.BlockSpec((1, 1, nc, DH), lambda b, g, i: (b, g, 0, 0)),
                  pl.BlockSpec((1, 1, nc, DH), lambda b, g, i: (b, g, 0, 0)),
                  pl.BlockSpec((nc, LANE), lambda b, g, i: (0, 0))],
        out_specs=(pl.BlockSpec((1, tq, HPG * DH), lambda b, g, i: (b, i, g)),
                   pl.BlockSpec((1, 1, tq, LANE), lambda b, g, i: (b, g, i, 0))),
        compiler_params=_params(("parallel", "parallel", "parallel")),
    )(q, kc, vc, jnp.asarray(m_np, CDT))

    expand = jnp.asarray((np.arange(S)[None, :] // NSA_SLC_LEN) == np.arange(LANE)[:, None], CDT)
    o_s = _flash(q, ks, vs, mode="sel", sel=sel, expand=expand, tq=1024, tk=1024)
    o_w = _flash(q, kw, vw, mode="window", window=NSA_WINDOW, tq=512, tk=512)
    ospec = pl.BlockSpec((1, ts, H * DH), lambda b, i: (b, i, 0))
    return pl.pallas_call(
        functools.partial(_nsa_combine_kernel, H=H, dv=DH),
        out_shape=jax.ShapeDtypeStruct((B, S, H * DH), CDT),
        grid=(B, S // ts),
        in_specs=[ospec, ospec, ospec, pl.BlockSpec((1, ts, LANE), lambda b, i: (b, i, gate_block))],
        out_specs=ospec,
        compiler_params=_params(("parallel", "parallel")),
    )(o_c, o_s, o_w, p)


def _dsa_prep_kernel(q_ref, k_ref, v_ref, qi_ref, ki_ref, wi_ref, cos_ref, sin_ref, cos4_ref, sin4_ref,
                     g_ref, gi_ref, qo, ko, vo, qio, kilo, kihi, wio, *, H, HKV, scale, wscale):
    ts = q_ref.shape[1]
    lane = lax.broadcasted_iota(I32, (ts, LANE), 1)
    cosf, sinf, cos4, sin4 = cos_ref[0], sin_ref[0], cos4_ref[0], sin4_ref[0]
    for h in range(H):
        y = _rms(q_ref[0, :, h * LANE:(h + 1) * LANE], g_ref[0:1, :])
        qo[0, h] = (_rope128(y, cosf, sinf) * scale).astype(qo.dtype)
    for h in range(HKV):
        sl = slice(h * LANE, (h + 1) * LANE)
        ko[0, h] = _rope128(_rms(k_ref[0, :, sl], g_ref[1:2, :]), cosf, sinf).astype(ko.dtype)
        vo[0, h] = v_ref[0, :, sl].astype(vo.dtype)
    for j in range(qi_ref.shape[2] // LANE):
        qio[0, j] = _rope64pair(qi_ref[0, :, j * LANE:(j + 1) * LANE], cos4, sin4, lane).astype(qio.dtype)
    ki = ki_ref[0]
    inv = lax.rsqrt(jnp.sum(ki * ki, axis=-1, keepdims=True) / 64.0 + EPS)
    r = _rope64pair(ki * inv * gi_ref[...], cos4, sin4, lane)
    kilo[0] = r.astype(kilo.dtype)
    kihi[0] = pltpu.roll(r, 64, 1).astype(kihi.dtype)
    wio[0] = wi_ref[0] * wscale


def _indexer_kernel(qi_ref, kj_ref, fl_ref, q_ref, klo_ref, khi_ref, w_ref, o_ref, wb_sc, sc_sc, acc_sc,
                    *, tq, tk, topk, n_pairs, rg):
    p = pl.program_id(1)
    qi, kj, fl = qi_ref[p], kj_ref[p], fl_ref[p]
    n_tiles = o_ref.shape[1]

    @pl.when((fl & 1) != 0)
    def _():
        w = w_ref[0]
        for h in range(2 * n_pairs):
            wb_sc[h] = jnp.broadcast_to(w[:, h:h + 1], (tq, LANE))

    acc_sc[...] = jnp.zeros((tq, tk), F32)
    cw = min(tk, 2 * LANE)

    def pair(j, _):
        q = q_ref[0, j]
        wa = jnp.tile(wb_sc[2 * j], (1, cw // LANE))
        wb = jnp.tile(wb_sc[2 * j + 1], (1, cw // LANE))
        for c in range(tk // cw):
            cols = slice(c * cw, (c + 1) * cw)
            sa = lax.dot_general(q, klo_ref[0, cols, :], (((1,), (1,)), ((), ())), preferred_element_type=F32)
            sb = lax.dot_general(q, khi_ref[0, cols, :], (((1,), (1,)), ((), ())), preferred_element_type=F32)
            acc_sc[:, cols] += wa * jnp.maximum(sa, 0.0) + wb * jnp.maximum(sb, 0.0)
        return 0

    lax.fori_loop(0, n_pairs, pair, 0, unroll=8)
    score = acc_sc[...]
    row = qi * tq + lax.broadcasted_iota(I32, (tq, tk), 0)
    col = kj * tk + lax.broadcasted_iota(I32, (tq, tk), 1)
    score = jnp.where(col <= row, score, NEG)
    bits = pltpu.bitcast(score, I32)
    key = bits ^ ((bits >> 31) & 0x7FFFFFFF)
    sc_sc[kj] = key

    @pl.when((fl & 2) != 0)
    def _():
        n_chunks = kj + 1
        nh_bits = int(np.float32(NEG_HALF).view(np.int32))
        key_neg_half = nh_bits ^ 0x7FFFFFFF if nh_bits < 0 else nh_bits
        for g in range(tq // rg):
            rows = pl.ds(g * rg, rg)

            def bit_step(state):
                b, thr, n_ge, _ = state
                cand = thr + jnp.left_shift(jnp.int32(1), 31 - b)

                def count(c, cnt):
                    blk = sc_sc[c, rows, :]
                    for u in range(tk // LANE):
                        cnt = cnt + (blk[:, u * LANE:(u + 1) * LANE] >= cand).astype(I32)
                    return cnt

                cnt = lax.fori_loop(0, n_chunks, count, jnp.zeros((rg, LANE), I32))
                tot = jnp.sum(cnt, axis=1, keepdims=True)
                take = tot >= topk
                n_ge = jnp.where(take, tot, n_ge)
                return b + 1, jnp.where(take, cand, thr), n_ge, jnp.max(n_ge)

            start = (jnp.int32(0), jnp.full((rg, LANE), -2**31, I32),
                     jnp.full((rg, LANE), 2**30, I32), jnp.int32(2**30))
            _, thr, _, _ = lax.while_loop(lambda st: (st[0] < 32) & (st[3] > topk), bit_step, start)
            thr = jnp.maximum(thr, key_neg_half + 1)
            thr_t = jnp.tile(thr, (1, tk // LANE))

            def emit(c, _):
                o_ref[0, c, rows, :] = jnp.where(sc_sc[c, rows, :] >= thr_t, 0.0, NEG).astype(o_ref.dtype)
                return 0

            def emit_masked(c, _):
                o_ref[0, c, rows, :] = jnp.full((rg, tk), NEG, o_ref.dtype)
                return 0

            lax.fori_loop(0, n_chunks, emit, 0)
            lax.fori_loop(n_chunks, n_tiles, emit_masked, 0)


def _dsa(p, pos, qk_norm, idx_k_norm):
    B, S, _ = p.shape
    H, HKV, DH = DSA_HEADS, DSA_KV_HEADS, DSA_HEAD_DIM
    NP = IDX_HEADS // 2
    ts = _tile(S, 256, 8)
    cosf, sinf = _tables128(pos)
    cos4, sin4 = _tables64pair(pos)
    tab = pl.BlockSpec((1, ts, LANE), lambda b, i: (b, i, 0))
    kw = HKV * DH
    gi = jnp.concatenate([idx_k_norm, jnp.zeros((LANE - IDX_DIM,), F32)]).reshape(1, LANE)
    head_out = lambda n: pl.BlockSpec((1, n, ts, DH), lambda b, i: (b, 0, i, 0))
    q, k, v, qidx, kilo, kihi, wi = pl.pallas_call(
        functools.partial(_dsa_prep_kernel, H=H, HKV=HKV, scale=DH ** -0.5 * LOG2E,
                          wscale=IDX_HEADS ** -0.5 * IDX_DIM ** -0.5),
        out_shape=(jax.ShapeDtypeStruct((B, H, S, DH), CDT), jax.ShapeDtypeStruct((B, HKV, S, DH), CDT),
                   jax.ShapeDtypeStruct((B, HKV, S, DH), CDT), jax.ShapeDtypeStruct((B, NP, S, LANE), CDT),
                   jax.ShapeDtypeStruct((B, S, LANE), CDT), jax.ShapeDtypeStruct((B, S, LANE), CDT),
                   jax.ShapeDtypeStruct((B, S, LANE), F32)),
        grid=(B, S // ts),
        in_specs=[pl.BlockSpec((1, ts, H * DH), lambda b, i: (b, i, 0)),
                  pl.BlockSpec((1, ts, kw), lambda b, i: (b, i, H * DH // kw)),
                  pl.BlockSpec((1, ts, kw), lambda b, i: (b, i, H * DH // kw + 1)),
                  pl.BlockSpec((1, ts, NP * LANE), lambda b, i: (b, i, (H * DH + 2 * kw) // (NP * LANE))),
                  pl.BlockSpec((1, ts, LANE), lambda b, i: (b, i, (H * DH + 2 * kw + NP * LANE) // LANE)),
                  pl.BlockSpec((1, ts, LANE), lambda b, i: (b, i, (H * DH + 2 * kw + NP * LANE) // LANE + 1)),
                  tab, tab, tab, tab,
                  pl.BlockSpec((2, DH), lambda b, i: (0, 0)), pl.BlockSpec((1, LANE), lambda b, i: (0, 0))],
        out_specs=(head_out(H), head_out(HKV), head_out(HKV), head_out(NP), tab, tab, tab),
        compiler_params=_params(("parallel", "parallel")),
    )(p, p, p, p, p, p, cosf, sinf, cos4, sin4, qk_norm, gi)

    topk = min(DSA_TOPK_MAX, S // 4)
    tq, tk = _tile(S, 256, 8), _tile(S, 1024)
    qi_t, kj_t, fl_t = _pairs(S, tq, tk)
    bias = pl.pallas_call(
        functools.partial(_indexer_kernel, tq=tq, tk=tk, topk=topk, n_pairs=NP, rg=min(128, tq)),
        out_shape=jax.ShapeDtypeStruct((B, S // tk, S, tk), CDT),
        grid_spec=pltpu.PrefetchScalarGridSpec(
            num_scalar_prefetch=3,
            grid=(B, int(qi_t.shape[0])),
            in_specs=[pl.BlockSpec((1, NP, tq, LANE), lambda b, p, qi, kj, fl: (b, 0, qi[p], 0)),
                      pl.BlockSpec((1, tk, LANE), lambda b, p, qi, kj, fl: (b, kj[p], 0)),
                      pl.BlockSpec((1, tk, LANE), lambda b, p, qi, kj, fl: (b, kj[p], 0)),
                      pl.BlockSpec((1, tq, LANE), lambda b, p, qi, kj, fl: (b, qi[p], 0))],
            out_specs=pl.BlockSpec((1, S // tk, tq, tk), lambda b, p, qi, kj, fl: (b, 0, qi[p], 0)),
            scratch_shapes=[pltpu.VMEM((2 * NP, tq, LANE), F32), pltpu.VMEM((S // tk, tq, tk), I32),
                            pltpu.VMEM((tq, tk), F32)]),
        compiler_params=_params(("parallel", "arbitrary")),
    )(qi_t, kj_t, fl_t, qidx, kilo, kihi, wi)
    return _flash(q, k, v, mode="bias", bias=bias, tq=4 * tq, tk=tk)


def _router_kernel(l_ref, i_ref, p_ref, *, n_experts):
    x = l_ref[...]
    lane = lax.broadcasted_iota(I32, x.shape, 1)
    x = jnp.where(lane < n_experts, x, -jnp.inf)
    m1 = jnp.max(x, axis=1, keepdims=True)
    i1 = jnp.min(jnp.where(x == m1, lane, LANE), axis=1, keepdims=True)
    x2 = jnp.where(lane == i1, -jnp.inf, x)
    m2 = jnp.max(x2, axis=1, keepdims=True)
    i2 = jnp.min(jnp.where(x2 == m2, lane, LANE), axis=1, keepdims=True)
    e2 = jnp.exp(m2 - m1)
    p1 = 1.0 / (1.0 + e2)
    p2 = e2 / (1.0 + e2)
    i_ref[...] = jnp.where(lane == 0, i1, jnp.where(lane == 1, i2, 0))
    p_ref[...] = jnp.where(lane == 0, p1, jnp.where(lane == 1, p2, 0.0))


def _row_copy(src_hbm, row, dst_vmem, r, sem):
    return pltpu.make_async_copy(src_hbm.at[pl.ds(row, 1), :], dst_vmem.at[pl.ds(r, 1), :], sem)


def _gather_pipeline(n_steps, tm, copies):
    i = pl.program_id(0)
    slot = i % 2

    def start_tile(tile, slot):
        def body(r, _):
            for src, idx_ref, dst, sem in copies(slot):
                _row_copy(src, idx_ref[tile * tm + r], dst, r, sem).start()
            return 0
        lax.fori_loop(0, tm, body, 0)

    def wait_tile(slot):
        def body(r, _):
            for src, _, dst, sem in copies(slot):
                _row_copy(src, 0, dst, r, sem).wait()
            return 0
        lax.fori_loop(0, tm, body, 0)

    @pl.when(i == 0)
    def _():
        start_tile(0, 0)

    @pl.when(i + 1 < n_steps)
    def _():
        start_tile(i + 1, 1 - slot)

    wait_tile(slot)
    return slot


def _dispatch_kernel(tok_ref, h_hbm, o_ref, buf, sem, *, tm, n_steps):
    slot = _gather_pipeline(n_steps, tm, lambda s: [(h_hbm, tok_ref, buf.at[s], sem.at[s])])
    o_ref[...] = buf[slot].astype(o_ref.dtype)


def _combine_kernel(s1_ref, s2_ref, y_hbm, x_ref, g_ref, p_ref, o_ref, buf1, buf2, sem, *, tm, n_steps):
    slot = _gather_pipeline(n_steps, tm, lambda s: [(y_hbm, s1_ref, buf1.at[s], sem.at[s]),
                                                    (y_hbm, s2_ref, buf2.at[s], sem.at[s])])
    p = p_ref[...]
    o_ref[...] = x_ref[...] + g_ref[0] * (p[:, 0:1] * buf1[slot] + p[:, 1:2] * buf2[slot])


def _moe(h, h32, x, g_f, w_router, w_gate, w_up, w_down, S):
    N, D = h.shape
    E, _, DE = w_gate.shape
    wr = jnp.zeros((D, LANE), F32).at[:, :E].set(w_router).astype(CDT)
    logits = _mm(h, wr, tn=LANE, tk=D)
    tr = _tile(N, 1024, 8)
    spec = pl.BlockSpec((tr, LANE), lambda i: (i, 0))
    idx, prob = pl.pallas_call(
        functools.partial(_router_kernel, n_experts=E),
        out_shape=(jax.ShapeDtypeStruct((N, LANE), I32), jax.ShapeDtypeStruct((N, LANE), F32)),
        grid=(N // tr,),
        in_specs=[spec],
        out_specs=(spec, spec),
        compiler_params=_params(("parallel",)),
    )(logits)

    tm = min(512, N)
    n_rows = 2 * N + E * tm
    e_flat = jnp.concatenate([idx[:, 0], idx[:, 1]])
    onehot = (e_flat[:, None] == jnp.arange(E)[None, :]).astype(I32)
    csum = jnp.cumsum(onehot, axis=0)
    rank = jnp.take_along_axis(csum, e_flat[:, None], axis=1)[:, 0] - 1
    padded = (csum[-1] + tm - 1) // tm * tm
    ends = jnp.cumsum(padded)
    pos = (ends - padded)[e_flat] + rank
    tile_expert = jnp.minimum(jnp.sum(jnp.arange(n_rows // tm)[:, None] * tm >= ends[None, :], axis=1), E - 1)
    tok = jnp.tile(jnp.arange(N, dtype=I32), 2)
    row_token = jnp.zeros((n_rows,), I32).at[pos].set(tok)

    tg = min(256, N)
    xs = pl.pallas_call(
        functools.partial(_dispatch_kernel, tm=tg, n_steps=n_rows // tg),
        out_shape=jax.ShapeDtypeStruct((n_rows, D), CDT),
        grid_spec=pltpu.PrefetchScalarGridSpec(
            num_scalar_prefetch=1,
            grid=(n_rows // tg,),
            in_specs=[pl.BlockSpec(memory_space=pl.ANY)],
            out_specs=pl.BlockSpec((tg, D), lambda i, tok: (i, 0)),
            scratch_shapes=[pltpu.VMEM((2, tg, D), F32), pltpu.SemaphoreType.DMA((2,))]),
        compiler_params=_params(("arbitrary",)),
    )(row_token, h32)
    te = tile_expert.astype(I32)
    hid = _mm(xs, w_gate.astype(CDT), mode="swiglu", b2=w_up.astype(CDT), group=te, out_dtype=CDT,
              tm=tm, tn=DE, tk=2048)
    ys = _mm(hid, w_down.astype(CDT), group=te, tm=tm, tn=1024, tk=DE)
    return pl.pallas_call(
        functools.partial(_combine_kernel, tm=tg, n_steps=N // tg),
        out_shape=jax.ShapeDtypeStruct((N, D), F32),
        grid_spec=pltpu.PrefetchScalarGridSpec(
            num_scalar_prefetch=2,
            grid=(N // tg,),
            in_specs=[pl.BlockSpec(memory_space=pl.ANY),
                      pl.BlockSpec((tg, D), lambda i, s1, s2: (i, 0)),
                      pl.BlockSpec((1, 1, D), lambda i, s1, s2: ((i * tg) // S, 0, 0)),
                      pl.BlockSpec((tg, LANE), lambda i, s1, s2: (i, 0))],
            out_specs=pl.BlockSpec((tg, D), lambda i, s1, s2: (i, 0)),
            scratch_shapes=[pltpu.VMEM((2, tg, D), F32), pltpu.VMEM((2, tg, D), F32),
                            pltpu.SemaphoreType.DMA((2,))]),
        compiler_params=_params(("arbitrary",)),
    )(pos[:N].astype(I32), pos[N:].astype(I32), ys, x, g_f, prob)


def _pad_cols(blocks, total):
    cols = []
    for w, width in blocks:
        cols.append(w)
        if width > w.shape[1]:
            cols.append(jnp.zeros((w.shape[0], width - w.shape[1]), w.dtype))
    out = jnp.concatenate(cols, axis=1)
    if total > out.shape[1]:
        out = jnp.concatenate([out, jnp.zeros((out.shape[0], total - out.shape[1]), out.dtype)], axis=1)
    return out.astype(CDT)


def _round_up(n, m):
    return (n + m - 1) // m * m


def kernel(x, c, positions, ada_w, ada_b, ada_table, norm_g, ev_w_in, ev_w_out, mla_q_a_norm, mla_kv_a_norm, mla_w_uq, mla_w_ukv, mla_q_norm, mla_k_norm, nsa_qk_norm, nsa_cmp_pos, nsa_cmp_w, ffn_w_gate, ffn_w_up, ffn_w_down, od_w_in, od_w_out, dsa_qk_norm, idx_k_norm, moe_router, moe_w_gate, moe_w_up, moe_w_down):
    B, S, D = x.shape
    N = B * S
    depth = ada_table.shape[0]
    cond = _cond(c, ada_w, ada_b).reshape(B, 6, D)

    mla_in = MLA_Q_RANK + MLA_KV_RANK + MLA_ROPE
    nq = NSA_HEADS * NSA_HEAD_DIM
    nkv = 6 * NSA_KV_GROUPS * NSA_HEAD_DIM
    hn = np.arange(NSA_HEADS)
    gate_perm = np.concatenate([hn * 3 + r for r in range(3)])

    x2 = x.reshape(N, D)
    for l in range(depth):
        i = l // 2
        mod = cond + ada_table[l]
        sh_a, sc_a, g_a, sh_f, sc_f, g_f = [mod[:, j, None, :] for j in range(6)]
        h = _norm(x2.reshape(B, S, D), norm_g[l, 0], sc_a, sh_a).reshape(N, D)
        if l % 2 == 0:
            w = ev_w_in[i]
            nsa = w[:, mla_in:]
            blocks = [(w[:, :MLA_Q_RANK + MLA_KV_RANK], MLA_Q_RANK + MLA_KV_RANK),
                      (nsa[:, :nq + nkv], nq + nkv),
                      (w[:, MLA_Q_RANK + MLA_KV_RANK:mla_in], LANE),
                      (nsa[:, nq + nkv:][:, gate_perm], LANE)]
            width = MLA_Q_RANK + MLA_KV_RANK + nq + nkv + 2 * LANE
            w_in = _pad_cols(blocks, _round_up(width, 512))
            p = _mm(h, w_in).reshape(B, S, -1)
            off = MLA_Q_RANK + MLA_KV_RANK
            a_out = _mla(p, (off + nq + nkv) // LANE, positions, mla_q_a_norm[i], mla_kv_a_norm[i],
                         mla_w_uq[i], mla_w_ukv[i], mla_q_norm[i], mla_k_norm[i])
            b_out = _nsa(p, off // nq, (off + nq) // (NSA_KV_GROUPS * NSA_HEAD_DIM),
                         (off + nq + nkv) // LANE + 1, positions, nsa_qk_norm[i], nsa_cmp_pos[i], nsa_cmp_w[i])
            mix = (a_out.reshape(N, -1), b_out.reshape(N, -1))
            w_out = ev_w_out[i]
        else:
            w = od_w_in[i]
            main = DSA_HEADS * DSA_HEAD_DIM + 2 * DSA_KV_HEADS * DSA_HEAD_DIM + IDX_HEADS * IDX_DIM
            blocks = [(w[:, :main], main), (w[:, main:main + IDX_DIM], LANE), (w[:, main + IDX_DIM:], LANE)]
            w_in = _pad_cols(blocks, _round_up(main + 2 * LANE, 512))
            p = _mm(h, w_in).reshape(B, S, -1)
            mix = _dsa(p, positions, dsa_qk_norm[i], idx_k_norm[i]).reshape(N, -1)
            w_out = od_w_out[i]
        x2 = _mm(mix, w_out.astype(CDT), mode="res", x=x2, g=g_a, rows_per_batch=S)
        if l % 2 == 0:
            h = _norm(x2.reshape(B, S, D), norm_g[l, 1], sc_f, sh_f).reshape(N, D)
            hid = _mm(h, ffn_w_gate[i].astype(CDT), mode="swiglu", b2=ffn_w_up[i].astype(CDT), out_dtype=CDT)
            x2 = _mm(hid, ffn_w_down[i].astype(CDT), mode="res", x=x2, g=g_f, tn=1024, tk=2048,
                     rows_per_batch=S)
        else:
            h, h32 = _norm(x2.reshape(B, S, D), norm_g[l, 1], sc_f, sh_f, also_f32=True)
            x2 = _moe(h.reshape(N, D), h32.reshape(N, D), x2, g_f, moe_router[i], moe_w_gate[i], moe_w_up[i],
                      moe_w_down[i], S)
    return x2.reshape(B, S, D)
```

```python
import functools

import numpy as np
import jax
import jax.numpy as jnp
from jax import lax
from jax.experimental import pallas as pl
from jax.experimental.pallas import tpu as pltpu

F32 = jnp.float32
I32 = jnp.int32
CDT = jnp.bfloat16

ROPE_THETA = 10000.0
EPS = 1e-6
NEG = -1e30
NEG_HALF = -5e29
FORCE = 1e9
REMOVED = -3e38
LOG2E = 1.4426950408889634

MLA_HEADS, MLA_Q_RANK, MLA_KV_RANK, MLA_NOPE, MLA_ROPE, MLA_V = 16, 1536, 512, 128, 64, 128
NSA_HEADS, NSA_KV_GROUPS, NSA_HEAD_DIM = 16, 4, 128
NSA_CMP_LEN, NSA_CMP_STRIDE, NSA_SLC_LEN, NSA_N_SEL, NSA_WINDOW = 32, 16, 64, 16, 512
DSA_HEADS, DSA_KV_HEADS, DSA_HEAD_DIM, IDX_HEADS, IDX_DIM, DSA_TOPK_MAX = 32, 8, 128, 32, 64, 256
N_EXPERTS = 8

LANE = 128
VMEM_LIMIT = 56 * 2**20


def _tile(n, pref, mult=LANE):
    if n <= pref:
        return n
    t = (pref // mult) * mult
    while t >= mult:
        if n % t == 0:
            return t
        t -= mult
    return n


def _params(sem):
    return pltpu.CompilerParams(dimension_semantics=sem, vmem_limit_bytes=VMEM_LIMIT)


def _silu(x):
    return x / (1.0 + jnp.exp(-x))


def _sigmoid(x):
    return 1.0 / (1.0 + jnp.exp(-x))


def _rms(x, g):
    return x * lax.rsqrt(jnp.mean(x * x, axis=-1, keepdims=True) + EPS) * g


def _sumsq128(x, halves=False):
    sq = x * x
    hi = sq.astype(CDT)
    lo = (sq - hi.astype(F32)).astype(CDT)
    if halves:
        r = lax.broadcasted_iota(I32, (LANE, LANE), 0)
        c = lax.broadcasted_iota(I32, (LANE, LANE), 1)
        ones = ((r < 64) == (c < 64)).astype(CDT)
    else:
        ones = jnp.ones((LANE, LANE), CDT)
    return (jnp.dot(hi, ones, preferred_element_type=F32) + jnp.dot(lo, ones, preferred_element_type=F32))


def _rms128(x, g):
    return x * lax.rsqrt(_sumsq128(x) * (1.0 / LANE) + EPS) * g


def _rope128(y, cosf, sinf):
    return y * cosf + pltpu.roll(y, 64, 1) * sinf


def _rope64pair(y, cos4, sin4, lane):
    rot = jnp.where((lane & 63) < 32, pltpu.roll(y, 96, 1), pltpu.roll(y, 32, 1))
    return y * cos4 + rot * sin4


def _mm_kernel(*refs, nk, mode, grouped, split_a):
    if grouped:
        refs = refs[1:]
    k = pl.program_id(2)
    a_refs = refs[:2] if split_a else refs[:1]
    if split_a:
        refs = refs[1:]
    n_in = {"plain": 2, "res": 4, "swiglu": 3}[mode]
    o_ref = refs[n_in]
    acc, acc2 = (tuple(refs[n_in + 1:]) + (None, None))[:2]
    if mode == "plain":
        b_ref, = refs[1:n_in]
    elif mode == "res":
        b_ref, x_ref, g_ref = refs[1:n_in]
    else:
        b_ref, b2_ref = refs[1:n_in]

    def product(w_ref):
        ka = a_refs[0].shape[1]
        return sum(jnp.dot(r[...], w_ref[n * ka:(n + 1) * ka, :].astype(r.dtype), preferred_element_type=F32)
                   for n, r in enumerate(a_refs))

    part = product(b_ref)
    if mode == "swiglu":
        part2 = product(b2_ref)

    if nk > 1:
        @pl.when(k == 0)
        def _():
            acc[...] = part
            if mode == "swiglu":
                acc2[...] = part2

        @pl.when(k > 0)
        def _():
            acc[...] += part
            if mode == "swiglu":
                acc2[...] += part2

    def finish():
        r = acc[...] if nk > 1 else part
        if mode == "plain":
            o_ref[...] = r.astype(o_ref.dtype)
        elif mode == "res":
            o_ref[...] = (x_ref[...] + g_ref[0] * r).astype(o_ref.dtype)
        else:
            r2 = acc2[...] if nk > 1 else part2
            o_ref[...] = (_silu(r) * r2).astype(o_ref.dtype)

    if nk > 1:
        pl.when(k == nk - 1)(finish)
    else:
        finish()


def _mm(a, b, *, mode="plain", b2=None, x=None, g=None, group=None, out_dtype=F32,
        tm=1024, tn=512, tk=4096, rows_per_batch=None):
    split_a = isinstance(a, tuple)
    a_list = list(a) if split_a else [a]
    M, Ka = a_list[0].shape
    K = Ka * len(a_list)
    N = b.shape[-1]
    tm, tn, tk = _tile(M, tm, 8), _tile(N, tn), (K if split_a else _tile(K, tk))
    nk = K // tk
    grid = (M // tm, N // tn, nk)
    grouped = group is not None
    if grouped:
        assert group.shape == (M // tm,)
        b_spec = pl.BlockSpec((None, tk, tn), lambda i, j, k, ge: (ge[i], k, j))
    else:
        b_spec = pl.BlockSpec((tk, tn), lambda i, j, k, *_: (k, j))
    if split_a:
        in_specs = [pl.BlockSpec((tm, Ka), lambda i, j, k, *_: (i, 0))] * 2 + [b_spec]
    else:
        in_specs = [pl.BlockSpec((tm, tk), lambda i, j, k, *_: (i, k)), b_spec]
    args = a_list + [b]
    scratch = [pltpu.VMEM((tm, tn), F32)] if nk > 1 else []
    if mode == "swiglu":
        in_specs.append(b_spec)
        args.append(b2)
        scratch = scratch * 2
    if mode == "res":
        rpb = rows_per_batch
        assert rpb % tm == 0
        in_specs += [pl.BlockSpec((tm, tn), lambda i, j, k, *_: (i, j)),
                     pl.BlockSpec((1, 1, tn), lambda i, j, k, *_: ((i * tm) // rpb, 0, j))]
        args += [x, g]
    return pl.pallas_call(
        functools.partial(_mm_kernel, nk=nk, mode=mode, grouped=grouped, split_a=split_a),
        out_shape=jax.ShapeDtypeStruct((M, N), out_dtype),
        grid_spec=pltpu.PrefetchScalarGridSpec(
            num_scalar_prefetch=1 if grouped else 0,
            grid=grid,
            in_specs=in_specs,
            out_specs=pl.BlockSpec((tm, tn), lambda i, j, k, *_: (i, j)),
            scratch_shapes=scratch),
        compiler_params=_params(("parallel", "parallel", "arbitrary")),
    )(*(([group] if grouped else []) + args))


def _cond_kernel(c_ref, w_ref, b_ref, o_ref):
    a = _silu(c_ref[...]).astype(CDT)
    o_ref[...] = jnp.dot(a, w_ref[...].astype(CDT), preferred_element_type=F32) + b_ref[...]


def _cond(c, ada_w, ada_b):
    B, D = c.shape
    N = ada_w.shape[1]
    cp = jnp.zeros((8, D), F32).at[:B].set(c)
    tn = _tile(N, 512)
    out = pl.pallas_call(
        _cond_kernel,
        out_shape=jax.ShapeDtypeStruct((8, N), F32),
        grid=(N // tn,),
        in_specs=[pl.BlockSpec((8, D), lambda j: (0, 0)),
                  pl.BlockSpec((D, tn), lambda j: (0, j)),
                  pl.BlockSpec((1, tn), lambda j: (0, j))],
        out_specs=pl.BlockSpec((8, tn), lambda j: (0, j)),
        compiler_params=_params(("parallel",)),
    )(cp, ada_w, ada_b.reshape(1, N))
    return out[:B]


def _norm_kernel(*refs, modulate, n_out):
    outs = refs[len(refs) - n_out:]
    if modulate:
        x_ref, g_ref, sc_ref, sh_ref = refs[:4]
    else:
        x_ref, g_ref = refs[:2]
    y = _rms(x_ref[0], g_ref[...])
    if modulate:
        y = y * (1.0 + sc_ref[0]) + sh_ref[0]
    for o_ref in outs:
        o_ref[0] = y.astype(o_ref.dtype)


def _norm(x, g, sc=None, sh=None, *, width=None, col_block=0, ts=512, also_f32=False):
    B, S, W = x.shape
    width = W if width is None else width
    ts = _tile(S, ts, 8)
    modulate = sc is not None
    in_specs = [pl.BlockSpec((1, ts, width), lambda b, i: (b, i, col_block)),
                pl.BlockSpec((1, width), lambda b, i: (0, 0))]
    args = [x, g.reshape(1, width)]
    if modulate:
        in_specs += [pl.BlockSpec((1, 1, width), lambda b, i: (b, 0, 0))] * 2
        args += [sc, sh]
    dtypes = (CDT, F32) if also_f32 else (CDT,)
    out = pl.pallas_call(
        functools.partial(_norm_kernel, modulate=modulate, n_out=len(dtypes)),
        out_shape=tuple(jax.ShapeDtypeStruct((B, S, width), d) for d in dtypes),
        grid=(B, S // ts),
        in_specs=in_specs,
        out_specs=tuple(pl.BlockSpec((1, ts, width), lambda b, i: (b, i, 0)) for _ in dtypes),
        compiler_params=_params(("parallel", "parallel")),
    )(*args)
    return out if also_f32 else out[0]


def _rope_angles(pos, dim):
    inv = ROPE_THETA ** (-jnp.arange(0, dim, 2, dtype=F32) / dim)
    ang = pos.astype(F32)[..., None] * inv
    return jnp.cos(ang), jnp.sin(ang)


def _tables128(pos):
    c, s = _rope_angles(pos, 128)
    return jnp.concatenate([c, c], -1), jnp.concatenate([-s, s], -1)


def _tables64pair(pos):
    c, s = _rope_angles(pos, 64)
    return jnp.concatenate([c, c, c, c], -1), jnp.concatenate([-s, s, -s, s], -1)


def _pairs(S, tq, tk, window=None):
    qi, kj, fl = [], [], []
    for i in range(S // tq):
        lo = 0 if window is None else max(0, i * tq - window + 1)
        js = list(range(lo // tk, (i * tq + tq - 1) // tk + 1))
        for n, j in enumerate(js):
            diag = (j + 1) * tk - 1 > i * tq
            f = (1 if n == 0 else 0) | (2 if n == len(js) - 1 else 0)
            f |= 4 if (diag or window is not None) else 0
            qi.append(i), kj.append(j), fl.append(f)
    return (jnp.asarray(qi, I32), jnp.asarray(kj, I32), jnp.asarray(fl, I32))


def _flash_kernel(qi_ref, kj_ref, fl_ref, q_ref, k_ref, v_ref, *rest, G, tq, tk, rb, mode, window):
    if mode == "bias":
        bias_ref, o_ref, m_sc, acc_sc = rest
    elif mode == "sel":
        sel_ref, e_ref, o_ref, m_sc, acc_sc = rest
    else:
        o_ref, m_sc, acc_sc = rest
    p = pl.program_id(2)
    qi, kj, fl = qi_ref[p], kj_ref[p], fl_ref[p]
    dk = q_ref.shape[-1]
    dv = v_ref.shape[-1]
    R = G * tq

    @pl.when((fl & 1) != 0)
    def _():
        m_sc[...] = jnp.full(m_sc.shape, NEG, F32)
        acc_sc[...] = jnp.zeros(acc_sc.shape, F32)

    def compute(position_mask):
        mask = None
        if position_mask:
            row = qi * tq + lax.broadcasted_iota(I32, (tq, tk), 0)
            col = kj * tk + lax.broadcasted_iota(I32, (tq, tk), 1)
            mask = col <= row
            if window is not None:
                mask = mask & (col > row - window)
        if mode == "sel":
            hit = jnp.dot(sel_ref[0, 0], e_ref[...], preferred_element_type=F32) > 0.5
            mask = hit if mask is None else (mask & hit)
        ones = jnp.ones((tk, LANE), v_ref.dtype)
        v_exts = [jnp.concatenate([v_ref[0, h], ones], axis=1) for h in range(v_ref.shape[1])]
        for r in range(R // rb):
            g, t0 = divmod(r * rb, tq)
            kv = g % k_ref.shape[1]
            v_ext = v_exts[kv]
            rows = slice(r * rb, (r + 1) * rb)
            s = lax.dot_general(q_ref[0, g, t0:t0 + rb, :], k_ref[0, kv], (((1,), (1,)), ((), ())),
                                preferred_element_type=F32)
            if mode == "bias":
                s = s + bias_ref[0, 0, t0:t0 + rb, :].astype(F32)
            if mask is not None:
                s = jnp.where(mask[t0:t0 + rb], s, NEG)
            m_prev = m_sc[rows, :]
            m_new = jnp.maximum(m_prev, jnp.max(s, axis=-1, keepdims=True))
            alpha = jnp.exp2(m_prev - m_new)
            pr = jnp.concatenate([jnp.exp2(s[:, c * LANE:(c + 1) * LANE] - m_new).astype(v_ref.dtype)
                                  for c in range(tk // LANE)], axis=1)
            pv = jnp.dot(pr, v_ext, preferred_element_type=F32)
            acc_sc[rows, :] = jnp.concatenate([alpha] * (dv // LANE + 1), axis=1) * acc_sc[rows, :] + pv
            m_sc[rows, :] = m_new

    if mode == "bias":
        compute(False)
    else:
        pl.when((fl & 4) != 0)(functools.partial(compute, True))
        pl.when((fl & 4) == 0)(functools.partial(compute, False))

    @pl.when((fl & 2) != 0)
    def _():
        for g in range(G):
            rows = slice(g * tq, (g + 1) * tq)
            o = acc_sc[rows, 0:dv] / acc_sc[rows, dv:dv + LANE]
            o_ref[0, :, g * dv:(g + 1) * dv] = o.astype(o_ref.dtype)


def _flash(q, k, v, *, mode="causal", window=None, bias=None, sel=None, expand=None,
           tq=512, tk=512, rb=128, heads_per_step=1, out_dtype=None):
    B, Hq, S, dk = q.shape
    Hkv, dv = k.shape[1], v.shape[-1]
    G = Hq // Hkv
    kvh = 1
    if heads_per_step > 1:
        assert G == 1 and Hq % heads_per_step == 0
        G = kvh = heads_per_step
    tq, tk = _tile(S, tq, 8), _tile(S, tk)
    qi, kj, fl = _pairs(S, tq, tk, window)
    P = int(qi.shape[0])
    in_specs = [pl.BlockSpec((1, G, tq, dk), lambda b, h, p, qi, kj, fl: (b, h, qi[p], 0)),
                pl.BlockSpec((1, kvh, tk, dk), lambda b, h, p, qi, kj, fl: (b, h, kj[p], 0)),
                pl.BlockSpec((1, kvh, tk, dv), lambda b, h, p, qi, kj, fl: (b, h, kj[p], 0))]
    args = [q, k, v]
    if mode == "bias":
        in_specs.append(pl.BlockSpec((1, 1, tq, tk), lambda b, h, p, qi, kj, fl: (b, kj[p], qi[p], 0)))
        args.append(bias)
    elif mode == "sel":
        in_specs += [pl.BlockSpec((1, 1, tq, LANE), lambda b, h, p, qi, kj, fl: (b, h, qi[p], 0)),
                     pl.BlockSpec((LANE, tk), lambda b, h, p, qi, kj, fl: (0, kj[p]))]
        args += [sel, expand]
    kern = functools.partial(_flash_kernel, G=G, tq=tq, tk=tk, rb=min(rb, tq), mode=mode, window=window)
    return pl.pallas_call(
        kern,
        out_shape=jax.ShapeDtypeStruct((B, S, Hq * dv), CDT if out_dtype is None else out_dtype),
        grid_spec=pltpu.PrefetchScalarGridSpec(
            num_scalar_prefetch=3,
            grid=(B, Hkv // kvh, P),
            in_specs=in_specs,
            out_specs=pl.BlockSpec((1, tq, G * dv), lambda b, h, p, qi, kj, fl: (b, qi[p], h)),
            scratch_shapes=[pltpu.VMEM((G * tq, LANE), F32), pltpu.VMEM((G * tq, dv + LANE), F32)]),
        compiler_params=_params(("parallel", "parallel", "arbitrary")),
    )(qi, kj, fl, *args)


def _mla_qprep_kernel(x_ref, cos_ref, sin_ref, gn_ref, gr_ref, o_ref, *, H, scale):
    ts = x_ref.shape[1]
    lane = lax.broadcasted_iota(I32, (ts, LANE), 1)
    lo = lane < 64
    cos4, sin4 = cos_ref[0], sin_ref[0]
    for h in range(H):
        xn = x_ref[0, :, h * LANE:(h + 1) * LANE]
        o_ref[0, h, :, 0:LANE] = (_rms128(xn, gn_ref[...]) * scale).astype(o_ref.dtype)
    for j in range(H // 2):
        xr = x_ref[0, :, (H + j) * LANE:(H + j + 1) * LANE]
        inv = lax.rsqrt(_sumsq128(xr, halves=True) * (1.0 / 64.0) + EPS)
        r = _rope64pair(xr * inv * gr_ref[...], cos4, sin4, lane) * scale
        o_ref[0, 2 * j, :, LANE:2 * LANE] = jnp.where(lo, r, 0.0).astype(o_ref.dtype)
        o_ref[0, 2 * j + 1, :, LANE:2 * LANE] = jnp.where(lo, pltpu.roll(r, 64, 1), 0.0).astype(o_ref.dtype)


def _mla_kvprep_kernel(x_ref, kr_ref, cos_ref, sin_ref, gn_ref, gr_ref, k_ref, v_ref, *, H):
    ts = x_ref.shape[1]
    lane = lax.broadcasted_iota(I32, (ts, LANE), 1)
    kr = kr_ref[0]
    inv = lax.rsqrt(_sumsq128(kr) * (1.0 / 64.0) + EPS)
    r = _rope64pair(kr * inv * gr_ref[...], cos_ref[0], sin_ref[0], lane).astype(k_ref.dtype)
    for h in range(H):
        xn = x_ref[0, :, h * LANE:(h + 1) * LANE]
        k_ref[0, h, :, 0:LANE] = _rms128(xn, gn_ref[...]).astype(k_ref.dtype)
        k_ref[0, h, :, LANE:2 * LANE] = r
        v_ref[0, h] = x_ref[0, :, (H + h) * LANE:(H + h + 1) * LANE].astype(v_ref.dtype)


def _mla(p, kr_block, pos, q_a_norm, kv_a_norm, w_uq, w_ukv, q_norm, k_norm):
    B, S, _ = p.shape
    H = MLA_HEADS
    N = B * S
    cqn = _norm(p, q_a_norm, width=MLA_Q_RANK, col_block=0)
    ckvn = _norm(p, kv_a_norm, width=MLA_KV_RANK, col_block=MLA_Q_RANK // MLA_KV_RANK)
    hh = np.arange(H)[:, None]
    q_perm = np.concatenate([(hh * 192 + np.arange(128)).ravel(), (hh * 192 + 128 + np.arange(64)).ravel()])
    kv_perm = np.concatenate([(hh * 256 + np.arange(128)).ravel(), (hh * 256 + 128 + np.arange(128)).ravel()])
    q_raw = _mm(cqn.reshape(N, -1), w_uq[:, q_perm].astype(CDT), tn=1024, tk=MLA_Q_RANK).reshape(B, S, -1)
    kv_raw = _mm(ckvn.reshape(N, -1), w_ukv[:, kv_perm].astype(CDT), tn=1024, tk=MLA_KV_RANK).reshape(B, S, -1)
    cos4, sin4 = _tables64pair(pos)
    ts = _tile(S, 256, 8)
    scale = (MLA_NOPE + MLA_ROPE) ** -0.5 * LOG2E
    gr = q_norm[MLA_NOPE:]
    tab = pl.BlockSpec((1, ts, LANE), lambda b, i: (b, i, 0))
    vec = pl.BlockSpec((1, LANE), lambda b, i: (0, 0))
    q = pl.pallas_call(
        functools.partial(_mla_qprep_kernel, H=H, scale=scale),
        out_shape=jax.ShapeDtypeStruct((B, H, S, 2 * LANE), CDT),
        grid=(B, S // ts),
        in_specs=[pl.BlockSpec((1, ts, H * 192), lambda b, i: (b, i, 0)), tab, tab, vec, vec],
        out_specs=pl.BlockSpec((1, H, ts, 2 * LANE), lambda b, i: (b, 0, i, 0)),
        compiler_params=_params(("parallel", "parallel")),
    )(q_raw, cos4, sin4, q_norm[:MLA_NOPE].reshape(1, -1), jnp.concatenate([gr, gr]).reshape(1, -1))
    gkr = jnp.concatenate([k_norm[MLA_NOPE:], jnp.zeros((64,), F32)])
    k, v = pl.pallas_call(
        functools.partial(_mla_kvprep_kernel, H=H),
        out_shape=(jax.ShapeDtypeStruct((B, H, S, 2 * LANE), CDT),
                   jax.ShapeDtypeStruct((B, H, S, LANE), CDT)),
        grid=(B, S // ts),
        in_specs=[pl.BlockSpec((1, ts, H * 256), lambda b, i: (b, i, 0)),
                  pl.BlockSpec((1, ts, LANE), lambda b, i: (b, i, kr_block)), tab, tab, vec, vec],
        out_specs=(pl.BlockSpec((1, H, ts, 2 * LANE), lambda b, i: (b, 0, i, 0)),
                   pl.BlockSpec((1, H, ts, LANE), lambda b, i: (b, 0, i, 0))),
        compiler_params=_params(("parallel", "parallel")),
    )(kv_raw, p, cos4, sin4, k_norm[:MLA_NOPE].reshape(1, -1), gkr.reshape(1, -1))
    return _flash(q, k, v, mode="causal", tq=1024, tk=1024, rb=256, heads_per_step=4)


def _nsa_prep_kernel(q_ref, kc_ref, vc_ref, ks_ref, vs_ref, kw_ref, vw_ref, cos_ref, sin_ref, g_ref,
                     qo, kso, vso, kwo, vwo, kco, vco, *, H, G, scale):
    cosf, sinf = cos_ref[0], sin_ref[0]
    for h in range(H):
        y = _rms128(q_ref[0, :, h * LANE:(h + 1) * LANE], g_ref[0:1, :])
        qo[0, h] = (_rope128(y, cosf, sinf) * scale).astype(qo.dtype)
    for g in range(G):
        sl = slice(g * LANE, (g + 1) * LANE)
        kso[0, g] = _rope128(_rms128(ks_ref[0, :, sl], g_ref[2:3, :]), cosf, sinf).astype(kso.dtype)
        kwo[0, g] = _rope128(_rms128(kw_ref[0, :, sl], g_ref[3:4, :]), cosf, sinf).astype(kwo.dtype)
        vso[0, g] = vs_ref[0, :, sl].astype(vso.dtype)
        vwo[0, g] = vw_ref[0, :, sl].astype(vwo.dtype)
        kco[0, g] = kc_ref[0, :, sl].astype(kco.dtype)
        vco[0, g] = vc_ref[0, :, sl].astype(vco.dtype)


def _compress_kernel(xk_ref, xv_ref, wk_ref, wv_ref, pek_ref, pev_ref, g_ref, cos_ref, sin_ref,
                     kc_ref, vc_ref):
    nc = xk_ref.shape[2]

    def comp(x_ref, w_ref, pe_ref):
        y = jnp.dot(x_ref[0, 0], w_ref[...], preferred_element_type=F32)
        c = jnp.dot(pe_ref[...], w_ref[...], preferred_element_type=F32)
        const = c[0:1, 0:LANE] + c[1:2, LANE:2 * LANE]
        return y[:, 0:LANE] + pltpu.roll(y[:, LANE:2 * LANE], nc - 1, 0) + const

    kc = comp(xk_ref, wk_ref, pek_ref)
    kc_ref[0, 0] = _rope128(_rms128(kc, g_ref[...]), cos_ref[0], sin_ref[0]).astype(kc_ref.dtype)
    vc_ref[0, 0] = comp(xv_ref, wv_ref, pev_ref).astype(vc_ref.dtype)


def _cmp_attn_kernel(q_ref, kc_ref, vc_ref, m_ref, o_ref, sel_ref, *, G, tq, n_cmp, n_sel, dv):
    i = pl.program_id(2)
    nc = kc_ref.shape[2]
    q = q_ref[0].reshape(G * tq, q_ref.shape[-1])
    s = lax.dot_general(q, kc_ref[0, 0], (((1,), (1,)), ((), ())), preferred_element_type=F32)
    s = s.reshape(G, tq, nc)
    t = i * tq + lax.broadcasted_iota(I32, (tq, nc), 0)
    n = lax.broadcasted_iota(I32, (tq, nc), 1)
    mask = (n * NSA_CMP_STRIDE + (NSA_CMP_LEN - 1) <= t) & (n < n_cmp)
    s = jnp.where(mask[None], s, NEG)
    mx = jnp.max(s, axis=-1, keepdims=True)
    e = jnp.where(mask[None], jnp.exp2(s - mx), 0.0)
    l = jnp.sum(e, axis=-1, keepdims=True)
    pc = e / jnp.where(l > 0.0, l, 1.0)
    o = jnp.dot(pc.reshape(G * tq, nc).astype(vc_ref.dtype), vc_ref[0, 0], preferred_element_type=F32)
    for g in range(G):
        o_ref[0, :, g * dv:(g + 1) * dv] = o[g * tq:(g + 1) * tq].astype(o_ref.dtype)

    ps = jnp.sum(pc, axis=0)
    hi = ps.astype(CDT)
    lo_part = (ps - hi.astype(F32)).astype(CDT)
    imp = (jnp.dot(hi, m_ref[...], preferred_element_type=F32)
           + jnp.dot(lo_part, m_ref[...], preferred_element_type=F32))
    blk = lax.broadcasted_iota(I32, (tq, LANE), 1)
    cur = (i * tq + lax.broadcasted_iota(I32, (tq, LANE), 0)) >> (NSA_SLC_LEN.bit_length() - 1)
    forced = (blk == 0) | (blk == cur) | (blk == cur - 1)
    imp = jnp.where(forced, FORCE, jnp.where(blk <= cur, imp, NEG))
    v = imp.T
    rowi = lax.broadcasted_iota(I32, (LANE, tq), 0)

    def take(_, carry):
        v, chosen = carry
        mval = jnp.max(v, axis=0, keepdims=True)
        first = jnp.min(jnp.where(v == mval, rowi, LANE), axis=0, keepdims=True)
        hit = rowi == first
        chosen = jnp.where(hit & (mval > NEG_HALF), 1.0, chosen)
        return jnp.where(hit, REMOVED, v), chosen

    _, chosen = lax.fori_loop(0, n_sel, take, (v, jnp.zeros((LANE, tq), F32)))
    sel_ref[0, 0] = chosen.T.astype(sel_ref.dtype)


def _nsa_combine_kernel(oc_ref, os_ref, ow_ref, g_ref, o_ref, *, H, dv):
    gate = _sigmoid(g_ref[0])
    for h in range(H):
        sl = slice(h * dv, (h + 1) * dv)
        o = (oc_ref[0, :, sl] * gate[:, h:h + 1] + os_ref[0, :, sl] * gate[:, H + h:H + h + 1]
             + ow_ref[0, :, sl] * gate[:, 2 * H + h:2 * H + h + 1])
        o_ref[0, :, sl] = o.astype(o_ref.dtype)


def _nsa(p, q_block, kv_block0, gate_block, pos, qk_norm, cmp_pos, cmp_w):
    B, S, _ = p.shape
    H, G, DH = NSA_HEADS, NSA_KV_GROUPS, NSA_HEAD_DIM
    HPG = H // G
    scale = DH ** -0.5 * LOG2E
    ts = _tile(S, 256, 8)
    cosf, sinf = _tables128(pos)
    tab = pl.BlockSpec((1, ts, LANE), lambda b, i: (b, i, 0))
    kvspec = [pl.BlockSpec((1, ts, G * DH), functools.partial(lambda b, i, m: (b, i, kv_block0 + m), m=m))
              for m in range(6)]
    head_out = lambda n: pl.BlockSpec((1, n, ts, DH), lambda b, i: (b, 0, i, 0))
    kv_shape = jax.ShapeDtypeStruct((B, G, S, DH), CDT)
    q, ks, vs, kw, vw, kcr, vcr = pl.pallas_call(
        functools.partial(_nsa_prep_kernel, H=H, G=G, scale=scale),
        out_shape=(jax.ShapeDtypeStruct((B, H, S, DH), CDT),) + (kv_shape,) * 6,
        grid=(B, S // ts),
        in_specs=[pl.BlockSpec((1, ts, H * DH), lambda b, i: (b, i, q_block))] + kvspec
                 + [tab, tab, pl.BlockSpec((4, DH), lambda b, i: (0, 0))],
        out_specs=(head_out(H),) + (head_out(G),) * 6,
        compiler_params=_params(("parallel", "parallel")),
    )(p, p, p, p, p, p, p, cosf, sinf, qk_norm)

    half = NSA_CMP_LEN // 2
    nc = S // NSA_CMP_STRIDE
    n_cmp = (S - NSA_CMP_LEN) // NSA_CMP_STRIDE + 1
    cmp_end = jnp.minimum(jnp.arange(nc) * NSA_CMP_STRIDE + NSA_CMP_LEN - 1, S - 1)
    ccos, csin = _tables128(pos[:, cmp_end])
    wcat = lambda w: jnp.concatenate([w[:half].reshape(half * DH, DH), w[half:].reshape(half * DH, DH)], 1).astype(CDT)
    pecat = lambda pe: jnp.zeros((8, half * DH), F32).at[0].set(pe[:half].reshape(-1)).at[1].set(
        pe[half:].reshape(-1)).astype(CDT)
    xspec = pl.BlockSpec((1, 1, nc, half * DH), lambda b, g: (b, g, 0, 0))
    wspec = pl.BlockSpec((half * DH, 2 * DH), lambda b, g: (0, 0))
    pespec = pl.BlockSpec((8, half * DH), lambda b, g: (0, 0))
    cspec = pl.BlockSpec((1, 1, nc, DH), lambda b, g: (b, g, 0, 0))
    ctab = pl.BlockSpec((1, nc, DH), lambda b, g: (b, 0, 0))
    kc, vc = pl.pallas_call(
        _compress_kernel,
        out_shape=(jax.ShapeDtypeStruct((B, G, nc, DH), CDT),) * 2,
        grid=(B, G),
        in_specs=[xspec, xspec, wspec, wspec, pespec, pespec,
                  pl.BlockSpec((1, DH), lambda b, g: (0, 0)), ctab, ctab],
        out_specs=(cspec, cspec),
        compiler_params=_params(("parallel", "parallel")),
    )(kcr.reshape(B, G, nc, half * DH), vcr.reshape(B, G, nc, half * DH), wcat(cmp_w[0]), wcat(cmp_w[1]),
      pecat(cmp_pos[0]), pecat(cmp_pos[1]), qk_norm[1].reshape(1, DH), ccos, csin)

    n_slc = S // NSA_SLC_LEN
    assert n_slc <= LANE
    r, cl = NSA_SLC_LEN // NSA_CMP_STRIDE, NSA_CMP_LEN // NSA_CMP_STRIDE
    m_np = np.zeros((nc, LANE), np.float32)
    for j in range(n_slc):
        for a in range(r):
            for c in range(cl):
                ci = j * r + a + c - (cl - 1)
                if 0 <= ci < n_cmp:
                    m_np[ci, j] += 1.0
    tq = _tile(S, 512, 8)
    o_c, sel = pl.pallas_call(
        functools.partial(_cmp_attn_kernel, G=HPG, tq=tq, n_cmp=n_cmp, n_sel=min(NSA_N_SEL, n_slc), dv=DH),
        out_shape=(jax.ShapeDtypeStruct((B, S, H * DH), CDT),
                   jax.ShapeDtypeStruct((B, G, S, LANE), CDT)),
        grid=(B, G, S // tq),
        in_specs=[pl.BlockSpec((1, HPG, tq, DH), lambda b, g, i: (b, g, i, 0)),
                  pl.BlockSpec((1, 1, nc, DH), lambda b, g, i: (b, g, 0, 0)),
                  pl.BlockSpec((1, 1, nc, DH), lambda b, g, i: (b, g, 0, 0)),
                  pl.BlockSpec((nc, LANE), lambda b, g, i: (0, 0))],
        out_specs=(pl.BlockSpec((1, tq, HPG * DH), lambda b, g, i: (b, i, g)),
                   pl.BlockSpec((1, 1, tq, LANE), lambda b, g, i: (b, g, i, 0))),
        compiler_params=_params(("parallel", "parallel", "parallel")),
    )(q, kc, vc, jnp.asarray(m_np, CDT))

    expand = jnp.asarray((np.arange(S)[None, :] // NSA_SLC_LEN) == np.arange(LANE)[:, None], CDT)
    o_s = _flash(q, ks, vs, mode="sel", sel=sel, expand=expand, tq=1024, tk=1024)
    o_w = _flash(q, kw, vw, mode="window", window=NSA_WINDOW, tq=512, tk=512)
    ospec = pl.BlockSpec((1, ts, H * DH), lambda b, i: (b, i, 0))
    return pl.pallas_call(
        functools.partial(_nsa_combine_kernel, H=H, dv=DH),
        out_shape=jax.ShapeDtypeStruct((B, S, H * DH), CDT),
        grid=(B, S // ts),
        in_specs=[ospec, ospec, ospec, pl.BlockSpec((1, ts, LANE), lambda b, i: (b, i, gate_block))],
        out_specs=ospec,
        compiler_params=_params(("parallel", "parallel")),
    )(o_c, o_s, o_w, p)


def _dsa_prep_kernel(q_ref, k_ref, v_ref, qi_ref, ki_ref, wi_ref, cos_ref, sin_ref, cos4_ref, sin4_ref,
                     g_ref, gi_ref, qo, ko, vo, qio, kilo, kihi, wio, *, H, HKV, scale, wscale):
    ts = q_ref.shape[1]
    lane = lax.broadcasted_iota(I32, (ts, LANE), 1)
    cosf, sinf, cos4, sin4 = cos_ref[0], sin_ref[0], cos4_ref[0], sin4_ref[0]
    for h in range(H):
        y = _rms128(q_ref[0, :, h * LANE:(h + 1) * LANE], g_ref[0:1, :])
        qo[0, h] = (_rope128(y, cosf, sinf) * scale).astype(qo.dtype)
    for h in range(HKV):
        sl = slice(h * LANE, (h + 1) * LANE)
        ko[0, h] = _rope128(_rms128(k_ref[0, :, sl], g_ref[1:2, :]), cosf, sinf).astype(ko.dtype)
        vo[0, h] = v_ref[0, :, sl].astype(vo.dtype)
    for j in range(qi_ref.shape[2] // LANE):
        qio[0, j] = _rope64pair(qi_ref[0, :, j * LANE:(j + 1) * LANE], cos4, sin4, lane).astype(qio.dtype)
    ki = ki_ref[0]
    inv = lax.rsqrt(_sumsq128(ki) * (1.0 / 64.0) + EPS)
    r = _rope64pair(ki * inv * gi_ref[...], cos4, sin4, lane)
    kilo[0] = r.astype(kilo.dtype)
    kihi[0] = pltpu.roll(r, 64, 1).astype(kihi.dtype)
    wio[0] = wi_ref[0] * wscale


def _indexer_kernel(qi_ref, kj_ref, fl_ref, q_ref, klo_ref, khi_ref, w_ref, o_ref, wb_sc, sc_sc, acc_sc,
                    *, tq, tk, topk, n_pairs, rg):
    p = pl.program_id(1)
    qi, kj, fl = qi_ref[p], kj_ref[p], fl_ref[p]
    n_tiles = o_ref.shape[1]

    @pl.when((fl & 1) != 0)
    def _():
        w = w_ref[0]
        for h in range(2 * n_pairs):
            wb_sc[h] = jnp.broadcast_to(w[:, h:h + 1], (tq, LANE))

    acc_sc[...] = jnp.zeros((tq, tk), F32)
    cw = min(tk, 2 * LANE)

    def pair(j, _):
        q = q_ref[0, j]
        wa = jnp.tile(wb_sc[2 * j], (1, cw // LANE))
        wb = jnp.tile(wb_sc[2 * j + 1], (1, cw // LANE))
        for c in range(tk // cw):
            cols = slice(c * cw, (c + 1) * cw)
            sa = lax.dot_general(q, klo_ref[0, cols, :], (((1,), (1,)), ((), ())), preferred_element_type=F32)
            sb = lax.dot_general(q, khi_ref[0, cols, :], (((1,), (1,)), ((), ())), preferred_element_type=F32)
            acc_sc[:, cols] += wa * jnp.maximum(sa, 0.0) + wb * jnp.maximum(sb, 0.0)
        return 0

    lax.fori_loop(0, n_pairs, pair, 0, unroll=8)
    score = acc_sc[...]
    row = qi * tq + lax.broadcasted_iota(I32, (tq, tk), 0)
    col = kj * tk + lax.broadcasted_iota(I32, (tq, tk), 1)
    score = jnp.where(col <= row, score, NEG)
    bits = pltpu.bitcast(score, I32)
    key = bits ^ ((bits >> 31) & 0x7FFFFFFF)
    sc_sc[kj] = key

    @pl.when((fl & 2) != 0)
    def _():
        n_chunks = kj + 1
        nh_bits = int(np.float32(NEG_HALF).view(np.int32))
        key_neg_half = nh_bits ^ 0x7FFFFFFF if nh_bits < 0 else nh_bits
        for g in range(tq // rg):
            rows = pl.ds(g * rg, rg)

            def bit_step(state):
                b, thr, n_ge, _ = state
                cand = thr + jnp.left_shift(jnp.int32(1), 31 - b)

                def count(c, cnt):
                    blk = sc_sc[c, rows, :]
                    for u in range(tk // LANE):
                        cnt = cnt + (blk[:, u * LANE:(u + 1) * LANE] >= cand).astype(I32)
                    return cnt

                cnt = lax.fori_loop(0, n_chunks, count, jnp.zeros((rg, LANE), I32))
                tot = jnp.sum(cnt, axis=1, keepdims=True)
                take = tot >= topk
                n_ge = jnp.where(take, tot, n_ge)
                return b + 1, jnp.where(take, cand, thr), n_ge, jnp.max(n_ge)

            start = (jnp.int32(0), jnp.full((rg, LANE), -2**31, I32),
                     jnp.full((rg, LANE), 2**30, I32), jnp.int32(2**30))
            _, thr, _, _ = lax.while_loop(lambda st: (st[0] < 32) & (st[3] > topk), bit_step, start)
            thr = jnp.maximum(thr, key_neg_half + 1)
            thr_t = jnp.tile(thr, (1, tk // LANE))

            def emit(c, _):
                o_ref[0, c, rows, :] = jnp.where(sc_sc[c, rows, :] >= thr_t, 0.0, NEG).astype(o_ref.dtype)
                return 0

            def emit_masked(c, _):
                o_ref[0, c, rows, :] = jnp.full((rg, tk), NEG, o_ref.dtype)
                return 0

            lax.fori_loop(0, n_chunks, emit, 0)
            lax.fori_loop(n_chunks, n_tiles, emit_masked, 0)


def _dsa(p, pos, qk_norm, idx_k_norm):
    B, S, _ = p.shape
    H, HKV, DH = DSA_HEADS, DSA_KV_HEADS, DSA_HEAD_DIM
    NP = IDX_HEADS // 2
    ts = _tile(S, 256, 8)
    cosf, sinf = _tables128(pos)
    cos4, sin4 = _tables64pair(pos)
    tab = pl.BlockSpec((1, ts, LANE), lambda b, i: (b, i, 0))
    kw = HKV * DH
    gi = jnp.concatenate([idx_k_norm, jnp.zeros((LANE - IDX_DIM,), F32)]).reshape(1, LANE)
    head_out = lambda n: pl.BlockSpec((1, n, ts, DH), lambda b, i: (b, 0, i, 0))
    q, k, v, qidx, kilo, kihi, wi = pl.pallas_call(
        functools.partial(_dsa_prep_kernel, H=H, HKV=HKV, scale=DH ** -0.5 * LOG2E,
                          wscale=IDX_HEADS ** -0.5 * IDX_DIM ** -0.5),
        out_shape=(jax.ShapeDtypeStruct((B, H, S, DH), CDT), jax.ShapeDtypeStruct((B, HKV, S, DH), CDT),
                   jax.ShapeDtypeStruct((B, HKV, S, DH), CDT), jax.ShapeDtypeStruct((B, NP, S, LANE), CDT),
                   jax.ShapeDtypeStruct((B, S, LANE), CDT), jax.ShapeDtypeStruct((B, S, LANE), CDT),
                   jax.ShapeDtypeStruct((B, S, LANE), F32)),
        grid=(B, S // ts),
        in_specs=[pl.BlockSpec((1, ts, H * DH), lambda b, i: (b, i, 0)),
                  pl.BlockSpec((1, ts, kw), lambda b, i: (b, i, H * DH // kw)),
                  pl.BlockSpec((1, ts, kw), lambda b, i: (b, i, H * DH // kw + 1)),
                  pl.BlockSpec((1, ts, NP * LANE), lambda b, i: (b, i, (H * DH + 2 * kw) // (NP * LANE))),
                  pl.BlockSpec((1, ts, LANE), lambda b, i: (b, i, (H * DH + 2 * kw + NP * LANE) // LANE)),
                  pl.BlockSpec((1, ts, LANE), lambda b, i: (b, i, (H * DH + 2 * kw + NP * LANE) // LANE + 1)),
                  tab, tab, tab, tab,
                  pl.BlockSpec((2, DH), lambda b, i: (0, 0)), pl.BlockSpec((1, LANE), lambda b, i: (0, 0))],
        out_specs=(head_out(H), head_out(HKV), head_out(HKV), head_out(NP), tab, tab, tab),
        compiler_params=_params(("parallel", "parallel")),
    )(p, p, p, p, p, p, cosf, sinf, cos4, sin4, qk_norm, gi)

    topk = min(DSA_TOPK_MAX, S // 4)
    tq, tk = _tile(S, 256, 8), _tile(S, 1024)
    qi_t, kj_t, fl_t = _pairs(S, tq, tk)
    bias = pl.pallas_call(
        functools.partial(_indexer_kernel, tq=tq, tk=tk, topk=topk, n_pairs=NP, rg=min(128, tq)),
        out_shape=jax.ShapeDtypeStruct((B, S // tk, S, tk), CDT),
        grid_spec=pltpu.PrefetchScalarGridSpec(
            num_scalar_prefetch=3,
            grid=(B, int(qi_t.shape[0])),
            in_specs=[pl.BlockSpec((1, NP, tq, LANE), lambda b, p, qi, kj, fl: (b, 0, qi[p], 0)),
                      pl.BlockSpec((1, tk, LANE), lambda b, p, qi, kj, fl: (b, kj[p], 0)),
                      pl.BlockSpec((1, tk, LANE), lambda b, p, qi, kj, fl: (b, kj[p], 0)),
                      pl.BlockSpec((1, tq, LANE), lambda b, p, qi, kj, fl: (b, qi[p], 0))],
            out_specs=pl.BlockSpec((1, S // tk, tq, tk), lambda b, p, qi, kj, fl: (b, 0, qi[p], 0)),
            scratch_shapes=[pltpu.VMEM((2 * NP, tq, LANE), F32), pltpu.VMEM((S // tk, tq, tk), I32),
                            pltpu.VMEM((tq, tk), F32)]),
        compiler_params=_params(("parallel", "arbitrary")),
    )(qi_t, kj_t, fl_t, qidx, kilo, kihi, wi)
    return _flash(q, k, v, mode="bias", bias=bias, tq=4 * tq, tk=tk)


def _router_kernel(l_ref, i_ref, p_ref, *, n_experts):
    x = l_ref[...]
    lane = lax.broadcasted_iota(I32, x.shape, 1)
    x = jnp.where(lane < n_experts, x, -jnp.inf)
    m1 = jnp.max(x, axis=1, keepdims=True)
    i1 = jnp.min(jnp.where(x == m1, lane, LANE), axis=1, keepdims=True)
    x2 = jnp.where(lane == i1, -jnp.inf, x)
    m2 = jnp.max(x2, axis=1, keepdims=True)
    i2 = jnp.min(jnp.where(x2 == m2, lane, LANE), axis=1, keepdims=True)
    e2 = jnp.exp(m2 - m1)
    p1 = 1.0 / (1.0 + e2)
    p2 = e2 / (1.0 + e2)
    i_ref[...] = jnp.where(lane == 0, i1, jnp.where(lane == 1, i2, 0))
    p_ref[...] = jnp.where(lane == 0, p1, jnp.where(lane == 1, p2, 0.0))


def _row_copy(src_hbm, row, dst_vmem, r, sem):
    return pltpu.make_async_copy(src_hbm.at[pl.ds(row, 1), :], dst_vmem.at[pl.ds(r, 1), :], sem)


def _gather_pipeline(n_steps, tm, copies):
    i = pl.program_id(0)
    slot = i % 2

    def start_tile(tile, slot):
        def body(r, _):
            for src, idx_ref, dst, sem in copies(slot):
                _row_copy(src, idx_ref[tile * tm + r], dst, r, sem).start()
            return 0
        lax.fori_loop(0, tm, body, 0)

    def wait_tile(slot):
        def body(r, _):
            for src, _, dst, sem in copies(slot):
                _row_copy(src, 0, dst, r, sem).wait()
            return 0
        lax.fori_loop(0, tm, body, 0)

    @pl.when(i == 0)
    def _():
        start_tile(0, 0)

    @pl.when(i + 1 < n_steps)
    def _():
        start_tile(i + 1, 1 - slot)

    wait_tile(slot)
    return slot


def _dispatch_kernel(tok_ref, h_hbm, o_ref, buf, sem, *, tm, n_steps):
    slot = _gather_pipeline(n_steps, tm, lambda s: [(h_hbm, tok_ref, buf.at[s], sem.at[s])])
    o_ref[...] = buf[slot].astype(o_ref.dtype)


def _combine_kernel(s1_ref, s2_ref, y_hbm, x_ref, g_ref, p_ref, o_ref, buf1, buf2, sem, *, tm, n_steps):
    slot = _gather_pipeline(n_steps, tm, lambda s: [(y_hbm, s1_ref, buf1.at[s], sem.at[s]),
                                                    (y_hbm, s2_ref, buf2.at[s], sem.at[s])])
    p = p_ref[...]
    o_ref[...] = x_ref[...] + g_ref[0] * (p[:, 0:1] * buf1[slot] + p[:, 1:2] * buf2[slot])


def _moe(h, h32, x, g_f, w_router, w_gate, w_up, w_down, S):
    N, D = h.shape
    E, _, DE = w_gate.shape
    wr = jnp.zeros((D, LANE), F32).at[:, :E].set(w_router).astype(CDT)
    logits = _mm(h, wr, tn=LANE, tk=D)
    tr = _tile(N, 1024, 8)
    spec = pl.BlockSpec((tr, LANE), lambda i: (i, 0))
    idx, prob = pl.pallas_call(
        functools.partial(_router_kernel, n_experts=E),
        out_shape=(jax.ShapeDtypeStruct((N, LANE), I32), jax.ShapeDtypeStruct((N, LANE), F32)),
        grid=(N // tr,),
        in_specs=[spec],
        out_specs=(spec, spec),
        compiler_params=_params(("parallel",)),
    )(logits)

    tm = min(512, N)
    n_rows = 2 * N + E * tm
    e_flat = jnp.concatenate([idx[:, 0], idx[:, 1]])
    onehot = (e_flat[:, None] == jnp.arange(E)[None, :]).astype(I32)
    csum = jnp.cumsum(onehot, axis=0)
    rank = jnp.take_along_axis(csum, e_flat[:, None], axis=1)[:, 0] - 1
    padded = (csum[-1] + tm - 1) // tm * tm
    ends = jnp.cumsum(padded)
    pos = (ends - padded)[e_flat] + rank
    tile_expert = jnp.minimum(jnp.sum(jnp.arange(n_rows // tm)[:, None] * tm >= ends[None, :], axis=1), E - 1)
    tok = jnp.tile(jnp.arange(N, dtype=I32), 2)
    row_token = jnp.zeros((n_rows,), I32).at[pos].set(tok)

    tg = min(256, N)
    xs = pl.pallas_call(
        functools.partial(_dispatch_kernel, tm=tg, n_steps=n_rows // tg),
        out_shape=jax.ShapeDtypeStruct((n_rows, D), CDT),
        grid_spec=pltpu.PrefetchScalarGridSpec(
            num_scalar_prefetch=1,
            grid=(n_rows // tg,),
            in_specs=[pl.BlockSpec(memory_space=pl.ANY)],
            out_specs=pl.BlockSpec((tg, D), lambda i, tok: (i, 0)),
            scratch_shapes=[pltpu.VMEM((2, tg, D), F32), pltpu.SemaphoreType.DMA((2,))]),
        compiler_params=_params(("arbitrary",)),
    )(row_token, h32)
    te = tile_expert.astype(I32)
    hid = _mm(xs, w_gate.astype(CDT), mode="swiglu", b2=w_up.astype(CDT), group=te, out_dtype=CDT,
              tm=tm, tn=DE, tk=2048)
    ys = _mm(hid, w_down.astype(CDT), group=te, tm=tm, tn=2048, tk=DE)
    return pl.pallas_call(
        functools.partial(_combine_kernel, tm=tg, n_steps=N // tg),
        out_shape=jax.ShapeDtypeStruct((N, D), F32),
        grid_spec=pltpu.PrefetchScalarGridSpec(
            num_scalar_prefetch=2,
            grid=(N // tg,),
            in_specs=[pl.BlockSpec(memory_space=pl.ANY),
                      pl.BlockSpec((tg, D), lambda i, s1, s2: (i, 0)),
                      pl.BlockSpec((1, 1, D), lambda i, s1, s2: ((i * tg) // S, 0, 0)),
                      pl.BlockSpec((tg, LANE), lambda i, s1, s2: (i, 0))],
            out_specs=pl.BlockSpec((tg, D), lambda i, s1, s2: (i, 0)),
            scratch_shapes=[pltpu.VMEM((2, tg, D), F32), pltpu.VMEM((2, tg, D), F32),
                            pltpu.SemaphoreType.DMA((2,))]),
        compiler_params=_params(("arbitrary",)),
    )(pos[:N].astype(I32), pos[N:].astype(I32), ys, x, g_f, prob)


def _pad_cols(blocks, total):
    cols = []
    for w, width in blocks:
        cols.append(w)
        if width > w.shape[1]:
            cols.append(jnp.zeros((w.shape[0], width - w.shape[1]), w.dtype))
    out = jnp.concatenate(cols, axis=1)
    if total > out.shape[1]:
        out = jnp.concatenate([out, jnp.zeros((out.shape[0], total - out.shape[1]), out.dtype)], axis=1)
    return out.astype(CDT)


def _round_up(n, m):
    return (n + m - 1) // m * m


def kernel(x, c, positions, ada_w, ada_b, ada_table, norm_g, ev_w_in, ev_w_out, mla_q_a_norm, mla_kv_a_norm, mla_w_uq, mla_w_ukv, mla_q_norm, mla_k_norm, nsa_qk_norm, nsa_cmp_pos, nsa_cmp_w, ffn_w_gate, ffn_w_up, ffn_w_down, od_w_in, od_w_out, dsa_qk_norm, idx_k_norm, moe_router, moe_w_gate, moe_w_up, moe_w_down):
    B, S, D = x.shape
    N = B * S
    depth = ada_table.shape[0]
    cond = _cond(c, ada_w, ada_b).reshape(B, 6, D)

    mla_in = MLA_Q_RANK + MLA_KV_RANK + MLA_ROPE
    nq = NSA_HEADS * NSA_HEAD_DIM
    nkv = 6 * NSA_KV_GROUPS * NSA_HEAD_DIM
    hn = np.arange(NSA_HEADS)
    gate_perm = np.concatenate([hn * 3 + r for r in range(3)])

    x2 = x.reshape(N, D)
    for l in range(depth):
        i = l // 2
        mod = cond + ada_table[l]
        sh_a, sc_a, g_a, sh_f, sc_f, g_f = [mod[:, j, None, :] for j in range(6)]
        h = _norm(x2.reshape(B, S, D), norm_g[l, 0], sc_a, sh_a).reshape(N, D)
        if l % 2 == 0:
            w = ev_w_in[i]
            nsa = w[:, mla_in:]
            blocks = [(w[:, :MLA_Q_RANK + MLA_KV_RANK], MLA_Q_RANK + MLA_KV_RANK),
                      (nsa[:, :nq + nkv], nq + nkv),
                      (w[:, MLA_Q_RANK + MLA_KV_RANK:mla_in], LANE),
                      (nsa[:, nq + nkv:][:, gate_perm], LANE)]
            width = MLA_Q_RANK + MLA_KV_RANK + nq + nkv + 2 * LANE
            w_in = _pad_cols(blocks, _round_up(width, 512))
            p = _mm(h, w_in).reshape(B, S, -1)
            off = MLA_Q_RANK + MLA_KV_RANK
            a_out = _mla(p, (off + nq + nkv) // LANE, positions, mla_q_a_norm[i], mla_kv_a_norm[i],
                         mla_w_uq[i], mla_w_ukv[i], mla_q_norm[i], mla_k_norm[i])
            b_out = _nsa(p, off // nq, (off + nq) // (NSA_KV_GROUPS * NSA_HEAD_DIM),
                         (off + nq + nkv) // LANE + 1, positions, nsa_qk_norm[i], nsa_cmp_pos[i], nsa_cmp_w[i])
            mix = (a_out.reshape(N, -1), b_out.reshape(N, -1))
            w_out = ev_w_out[i]
        else:
            w = od_w_in[i]
            main = DSA_HEADS * DSA_HEAD_DIM + 2 * DSA_KV_HEADS * DSA_HEAD_DIM + IDX_HEADS * IDX_DIM
            blocks = [(w[:, :main], main), (w[:, main:main + IDX_DIM], LANE), (w[:, main + IDX_DIM:], LANE)]
            w_in = _pad_cols(blocks, _round_up(main + 2 * LANE, 512))
            p = _mm(h, w_in).reshape(B, S, -1)
            mix = _dsa(p, positions, dsa_qk_norm[i], idx_k_norm[i]).reshape(N, -1)
            w_out = od_w_out[i]
        x2 = _mm(mix, w_out.astype(CDT), mode="res", x=x2, g=g_a, rows_per_batch=S)
        if l % 2 == 0:
            h = _norm(x2.reshape(B, S, D), norm_g[l, 1], sc_f, sh_f).reshape(N, D)
            hid = _mm(h, ffn_w_gate[i].astype(CDT), mode="swiglu", b2=ffn_w_up[i].astype(CDT), out_dtype=CDT)
            x2 = _mm(hid, ffn_w_down[i].astype(CDT), mode="res", x=x2, g=g_f, tn=1024, tk=2048,
                     rows_per_batch=S)
        else:
            h, h32 = _norm(x2.reshape(B, S, D), norm_g[l, 1], sc_f, sh_f, also_f32=True)
            x2 = _moe(h.reshape(N, D), h32.reshape(N, D), x2, g_f, moe_router[i], moe_w_gate[i], moe_w_up[i],
                      moe_w_down[i], S)
    return x2.reshape(B, S, D)
```

```python
import functools

import numpy as np
import jax
import jax.numpy as jnp
from jax import lax
from jax.experimental import pallas as pl
from jax.experimental.pallas import tpu as pltpu

F32 = jnp.float32
I32 = jnp.int32
CDT = jnp.bfloat16

ROPE_THETA = 10000.0
EPS = 1e-6
NEG = -1e30
NEG_HALF = -5e29
FORCE = 1e9
REMOVED = -3e38
LOG2E = 1.4426950408889634

MLA_HEADS, MLA_Q_RANK, MLA_KV_RANK, MLA_NOPE, MLA_ROPE, MLA_V = 16, 1536, 512, 128, 64, 128
NSA_HEADS, NSA_KV_GROUPS, NSA_HEAD_DIM = 16, 4, 128
NSA_CMP_LEN, NSA_CMP_STRIDE, NSA_SLC_LEN, NSA_N_SEL, NSA_WINDOW = 32, 16, 64, 16, 512
DSA_HEADS, DSA_KV_HEADS, DSA_HEAD_DIM, IDX_HEADS, IDX_DIM, DSA_TOPK_MAX = 32, 8, 128, 32, 64, 256
N_EXPERTS = 8

LANE = 128
VMEM_LIMIT = 56 * 2**20


def _tile(n, pref, mult=LANE):
    if n <= pref:
        return n
    t = (pref // mult) * mult
    while t >= mult:
        if n % t == 0:
            return t
        t -= mult
    return n


def _params(sem):
    return pltpu.CompilerParams(dimension_semantics=sem, vmem_limit_bytes=VMEM_LIMIT)


def _silu(x):
    return x / (1.0 + jnp.exp(-x))


def _sigmoid(x):
    return 1.0 / (1.0 + jnp.exp(-x))


def _rms(x, g):
    return x * lax.rsqrt(jnp.mean(x * x, axis=-1, keepdims=True) + EPS) * g


def _sumsq128(x, halves=False):
    sq = x * x
    hi = sq.astype(CDT)
    lo = (sq - hi.astype(F32)).astype(CDT)
    if halves:
        r = lax.broadcasted_iota(I32, (LANE, LANE), 0)
        c = lax.broadcasted_iota(I32, (LANE, LANE), 1)
        ones = ((r < 64) == (c < 64)).astype(CDT)
    else:
        ones = jnp.ones((LANE, LANE), CDT)
    return (jnp.dot(hi, ones, preferred_element_type=F32) + jnp.dot(lo, ones, preferred_element_type=F32))


def _rms128(x, g):
    return x * lax.rsqrt(_sumsq128(x) * (1.0 / LANE) + EPS) * g


def _rope128(y, cosf, sinf):
    return y * cosf + pltpu.roll(y, 64, 1) * sinf


def _rope64pair(y, cos4, sin4, lane):
    rot = jnp.where((lane & 63) < 32, pltpu.roll(y, 96, 1), pltpu.roll(y, 32, 1))
    return y * cos4 + rot * sin4


def _mm_kernel(*refs, nk, mode, grouped, split_a):
    if grouped:
        refs = refs[1:]
    k = pl.program_id(2)
    a_refs = refs[:2] if split_a else refs[:1]
    if split_a:
        refs = refs[1:]
    n_in = {"plain": 2, "res": 4, "swiglu": 3}[mode]
    o_ref = refs[n_in]
    acc, acc2 = (tuple(refs[n_in + 1:]) + (None, None))[:2]
    if mode == "plain":
        b_ref, = refs[1:n_in]
    elif mode == "res":
        b_ref, x_ref, g_ref = refs[1:n_in]
    else:
        b_ref, b2_ref = refs[1:n_in]

    def product(w_ref):
        ka = a_refs[0].shape[1]
        return sum(jnp.dot(r[...], w_ref[n * ka:(n + 1) * ka, :].astype(r.dtype), preferred_element_type=F32)
                   for n, r in enumerate(a_refs))

    part = product(b_ref)
    if mode == "swiglu":
        part2 = product(b2_ref)

    if nk > 1:
        @pl.when(k == 0)
        def _():
            acc[...] = part
            if mode == "swiglu":
                acc2[...] = part2

        @pl.when(k > 0)
        def _():
            acc[...] += part
            if mode == "swiglu":
                acc2[...] += part2

    def finish():
        r = acc[...] if nk > 1 else part
        if mode == "plain":
            o_ref[...] = r.astype(o_ref.dtype)
        elif mode == "res":
            o_ref[...] = (x_ref[...] + g_ref[0] * r).astype(o_ref.dtype)
        else:
            r2 = acc2[...] if nk > 1 else part2
            o_ref[...] = (_silu(r) * r2).astype(o_ref.dtype)

    if nk > 1:
        pl.when(k == nk - 1)(finish)
    else:
        finish()


def _mm(a, b, *, mode="plain", b2=None, x=None, g=None, group=None, out_dtype=F32,
        tm=1024, tn=512, tk=4096, rows_per_batch=None):
    split_a = isinstance(a, tuple)
    a_list = list(a) if split_a else [a]
    M, Ka = a_list[0].shape
    K = Ka * len(a_list)
    N = b.shape[-1]
    tm, tn, tk = _tile(M, tm, 8), _tile(N, tn), (K if split_a else _tile(K, tk))
    nk = K // tk
    grid = (M // tm, N // tn, nk)
    grouped = group is not None
    if grouped:
        assert group.shape == (M // tm,)
        b_spec = pl.BlockSpec((None, tk, tn), lambda i, j, k, ge: (ge[i], k, j))
    else:
        b_spec = pl.BlockSpec((tk, tn), lambda i, j, k, *_: (k, j))
    if split_a:
        in_specs = [pl.BlockSpec((tm, Ka), lambda i, j, k, *_: (i, 0))] * 2 + [b_spec]
    else:
        in_specs = [pl.BlockSpec((tm, tk), lambda i, j, k, *_: (i, k)), b_spec]
    args = a_list + [b]
    scratch = [pltpu.VMEM((tm, tn), F32)] if nk > 1 else []
    if mode == "swiglu":
        in_specs.append(b_spec)
        args.append(b2)
        scratch = scratch * 2
    if mode == "res":
        rpb = rows_per_batch
        assert rpb % tm == 0
        in_specs += [pl.BlockSpec((tm, tn), lambda i, j, k, *_: (i, j)),
                     pl.BlockSpec((1, 1, tn), lambda i, j, k, *_: ((i * tm) // rpb, 0, j))]
        args += [x, g]
    return pl.pallas_call(
        functools.partial(_mm_kernel, nk=nk, mode=mode, grouped=grouped, split_a=split_a),
        out_shape=jax.ShapeDtypeStruct((M, N), out_dtype),
        grid_spec=pltpu.PrefetchScalarGridSpec(
            num_scalar_prefetch=1 if grouped else 0,
            grid=grid,
            in_specs=in_specs,
            out_specs=pl.BlockSpec((tm, tn), lambda i, j, k, *_: (i, j)),
            scratch_shapes=scratch),
        compiler_params=_params(("parallel", "parallel", "arbitrary")),
    )(*(([group] if grouped else []) + args))


def _cond_kernel(c_ref, w_ref, b_ref, o_ref):
    a = _silu(c_ref[...]).astype(CDT)
    o_ref[...] = jnp.dot(a, w_ref[...].astype(CDT), preferred_element_type=F32) + b_ref[...]


def _cond(c, ada_w, ada_b):
    B, D = c.shape
    N = ada_w.shape[1]
    cp = jnp.zeros((8, D), F32).at[:B].set(c)
    tn = _tile(N, 512)
    out = pl.pallas_call(
        _cond_kernel,
        out_shape=jax.ShapeDtypeStruct((8, N), F32),
        grid=(N // tn,),
        in_specs=[pl.BlockSpec((8, D), lambda j: (0, 0)),
                  pl.BlockSpec((D, tn), lambda j: (0, j)),
                  pl.BlockSpec((1, tn), lambda j: (0, j))],
        out_specs=pl.BlockSpec((8, tn), lambda j: (0, j)),
        compiler_params=_params(("parallel",)),
    )(cp, ada_w, ada_b.reshape(1, N))
    return out[:B]


def _norm_kernel(*refs, modulate, n_out):
    outs = refs[len(refs) - n_out:]
    if modulate:
        x_ref, g_ref, sc_ref, sh_ref = refs[:4]
    else:
        x_ref, g_ref = refs[:2]
    y = _rms(x_ref[0], g_ref[...])
    if modulate:
        y = y * (1.0 + sc_ref[0]) + sh_ref[0]
    for o_ref in outs:
        o_ref[0] = y.astype(o_ref.dtype)


def _norm(x, g, sc=None, sh=None, *, width=None, col_block=0, ts=512, also_f32=False):
    B, S, W = x.shape
    width = W if width is None else width
    ts = _tile(S, ts, 8)
    modulate = sc is not None
    in_specs = [pl.BlockSpec((1, ts, width), lambda b, i: (b, i, col_block)),
                pl.BlockSpec((1, width), lambda b, i: (0, 0))]
    args = [x, g.reshape(1, width)]
    if modulate:
        in_specs += [pl.BlockSpec((1, 1, width), lambda b, i: (b, 0, 0))] * 2
        args += [sc, sh]
    dtypes = (CDT, F32) if also_f32 else (CDT,)
    out = pl.pallas_call(
        functools.partial(_norm_kernel, modulate=modulate, n_out=len(dtypes)),
        out_shape=tuple(jax.ShapeDtypeStruct((B, S, width), d) for d in dtypes),
        grid=(B, S // ts),
        in_specs=in_specs,
        out_specs=tuple(pl.BlockSpec((1, ts, width), lambda b, i: (b, i, 0)) for _ in dtypes),
        compiler_params=_params(("parallel", "parallel")),
    )(*args)
    return out if also_f32 else out[0]


def _rope_angles(pos, dim):
    inv = ROPE_THETA ** (-jnp.arange(0, dim, 2, dtype=F32) / dim)
    ang = pos.astype(F32)[..., None] * inv
    return jnp.cos(ang), jnp.sin(ang)


def _tables128(pos):
    c, s = _rope_angles(pos, 128)
    return jnp.concatenate([c, c], -1), jnp.concatenate([-s, s], -1)


def _tables64pair(pos):
    c, s = _rope_angles(pos, 64)
    return jnp.concatenate([c, c, c, c], -1), jnp.concatenate([-s, s, -s, s], -1)


def _pairs(S, tq, tk, window=None):
    qi, kj, fl = [], [], []
    for i in range(S // tq):
        lo = 0 if window is None else max(0, i * tq - window + 1)
        js = list(range(lo // tk, (i * tq + tq - 1) // tk + 1))
        for n, j in enumerate(js):
            diag = (j + 1) * tk - 1 > i * tq
            f = (1 if n == 0 else 0) | (2 if n == len(js) - 1 else 0)
            f |= 4 if (diag or window is not None) else 0
            qi.append(i), kj.append(j), fl.append(f)
    return (jnp.asarray(qi, I32), jnp.asarray(kj, I32), jnp.asarray(fl, I32))


def _flash_kernel(qi_ref, kj_ref, fl_ref, q_ref, k_ref, v_ref, *rest, G, tq, tk, rb, mode, window):
    if mode == "bias":
        bias_ref, o_ref, m_sc, acc_sc = rest
    elif mode == "sel":
        sel_ref, e_ref, o_ref, m_sc, acc_sc = rest
    else:
        o_ref, m_sc, acc_sc = rest
    p = pl.program_id(2)
    qi, kj, fl = qi_ref[p], kj_ref[p], fl_ref[p]
    dk = q_ref.shape[-1]
    dv = v_ref.shape[-1]
    R = G * tq

    @pl.when((fl & 1) != 0)
    def _():
        m_sc[...] = jnp.full(m_sc.shape, NEG, F32)
        acc_sc[...] = jnp.zeros(acc_sc.shape, F32)

    square = window is None and tq == tk

    def compute(position_mask, diagonal):
        mask = None
        if position_mask:
            row = qi * tq + lax.broadcasted_iota(I32, (tq, tk), 0)
            col = kj * tk + lax.broadcasted_iota(I32, (tq, tk), 1)
            mask = col <= row
            if window is not None:
                mask = mask & (col > row - window)
        if mode == "sel":
            hit = jnp.dot(sel_ref[0, 0], e_ref[...], preferred_element_type=F32) > 0.5
            mask = hit if mask is None else (mask & hit)
        ones = jnp.ones((tk, LANE), v_ref.dtype)
        v_exts = [jnp.concatenate([v_ref[0, h], ones], axis=1) for h in range(v_ref.shape[1])]
        for r in range(R // rb):
            g, t0 = divmod(r * rb, tq)
            kv = g % k_ref.shape[1]
            v_ext = v_exts[kv]
            rows = slice(r * rb, (r + 1) * rb)
            w = min(tk, -(-(t0 + rb) // 256) * 256) if (diagonal and square) else tk
            s = lax.dot_general(q_ref[0, g, t0:t0 + rb, :], k_ref[0, kv, 0:w, :], (((1,), (1,)), ((), ())),
                                preferred_element_type=F32)
            if mode == "bias":
                s = s + bias_ref[0, 0, t0:t0 + rb, 0:w].astype(F32)
            if mask is not None:
                s = jnp.where(mask[t0:t0 + rb, 0:w], s, NEG)
            m_prev = m_sc[rows, :]
            m_new = jnp.maximum(m_prev, jnp.max(s, axis=-1, keepdims=True))
            alpha = jnp.exp2(m_prev - m_new)
            pr = jnp.concatenate([jnp.exp2(s[:, c * LANE:(c + 1) * LANE] - m_new).astype(v_ref.dtype)
                                  for c in range(w // LANE)], axis=1)
            pv = jnp.dot(pr, v_ext[0:w], preferred_element_type=F32)
            acc_sc[rows, :] = jnp.concatenate([alpha] * (dv // LANE + 1), axis=1) * acc_sc[rows, :] + pv
            m_sc[rows, :] = m_new

    flagged = (fl & 4) != 0
    pl.when(flagged)(functools.partial(compute, mode != "bias", True))
    pl.when(jnp.logical_not(flagged))(functools.partial(compute, False, False))

    @pl.when((fl & 2) != 0)
    def _():
        for g in range(G):
            rows = slice(g * tq, (g + 1) * tq)
            o = acc_sc[rows, 0:dv] / acc_sc[rows, dv:dv + LANE]
            o_ref[0, :, g * dv:(g + 1) * dv] = o.astype(o_ref.dtype)


def _flash(q, k, v, *, mode="causal", window=None, bias=None, sel=None, expand=None,
           tq=512, tk=512, rb=128, heads_per_step=1, out_dtype=None):
    B, Hq, S, dk = q.shape
    Hkv, dv = k.shape[1], v.shape[-1]
    G = Hq // Hkv
    kvh = 1
    if heads_per_step > 1:
        assert G == 1 and Hq % heads_per_step == 0
        G = kvh = heads_per_step
    tq, tk = _tile(S, tq, 8), _tile(S, tk)
    qi, kj, fl = _pairs(S, tq, tk, window)
    P = int(qi.shape[0])
    in_specs = [pl.BlockSpec((1, G, tq, dk), lambda b, h, p, qi, kj, fl: (b, h, qi[p], 0)),
                pl.BlockSpec((1, kvh, tk, dk), lambda b, h, p, qi, kj, fl: (b, h, kj[p], 0)),
                pl.BlockSpec((1, kvh, tk, dv), lambda b, h, p, qi, kj, fl: (b, h, kj[p], 0))]
    args = [q, k, v]
    if mode == "bias":
        in_specs.append(pl.BlockSpec((1, 1, tq, tk), lambda b, h, p, qi, kj, fl: (b, kj[p], qi[p], 0)))
        args.append(bias)
    elif mode == "sel":
        in_specs += [pl.BlockSpec((1, 1, tq, LANE), lambda b, h, p, qi, kj, fl: (b, h, qi[p], 0)),
                     pl.BlockSpec((LANE, tk), lambda b, h, p, qi, kj, fl: (0, kj[p]))]
        args += [sel, expand]
    kern = functools.partial(_flash_kernel, G=G, tq=tq, tk=tk, rb=min(rb, tq), mode=mode, window=window)
    return pl.pallas_call(
        kern,
        out_shape=jax.ShapeDtypeStruct((B, S, Hq * dv), CDT if out_dtype is None else out_dtype),
        grid_spec=pltpu.PrefetchScalarGridSpec(
            num_scalar_prefetch=3,
            grid=(B, Hkv // kvh, P),
            in_specs=in_specs,
            out_specs=pl.BlockSpec((1, tq, G * dv), lambda b, h, p, qi, kj, fl: (b, qi[p], h)),
            scratch_shapes=[pltpu.VMEM((G * tq, LANE), F32), pltpu.VMEM((G * tq, dv + LANE), F32)]),
        compiler_params=_params(("parallel", "parallel", "arbitrary")),
    )(qi, kj, fl, *args)


def _mla_qprep_kernel(x_ref, cos_ref, sin_ref, gn_ref, gr_ref, o_ref, *, H, scale):
    ts = x_ref.shape[1]
    lane = lax.broadcasted_iota(I32, (ts, LANE), 1)
    lo = lane < 64
    cos4, sin4 = cos_ref[0], sin_ref[0]
    for h in range(H):
        xn = x_ref[0, :, h * LANE:(h + 1) * LANE]
        o_ref[0, h, :, 0:LANE] = (_rms128(xn, gn_ref[...]) * scale).astype(o_ref.dtype)
    for j in range(H // 2):
        xr = x_ref[0, :, (H + j) * LANE:(H + j + 1) * LANE]
        inv = lax.rsqrt(_sumsq128(xr, halves=True) * (1.0 / 64.0) + EPS)
        r = _rope64pair(xr * inv * gr_ref[...], cos4, sin4, lane) * scale
        o_ref[0, 2 * j, :, LANE:2 * LANE] = jnp.where(lo, r, 0.0).astype(o_ref.dtype)
        o_ref[0, 2 * j + 1, :, LANE:2 * LANE] = jnp.where(lo, pltpu.roll(r, 64, 1), 0.0).astype(o_ref.dtype)


def _mla_kvprep_kernel(x_ref, kr_ref, cos_ref, sin_ref, gn_ref, gr_ref, k_ref, v_ref, *, H):
    ts = x_ref.shape[1]
    lane = lax.broadcasted_iota(I32, (ts, LANE), 1)
    kr = kr_ref[0]
    inv = lax.rsqrt(_sumsq128(kr) * (1.0 / 64.0) + EPS)
    r = _rope64pair(kr * inv * gr_ref[...], cos_ref[0], sin_ref[0], lane).astype(k_ref.dtype)
    for h in range(H):
        xn = x_ref[0, :, h * LANE:(h + 1) * LANE]
        k_ref[0, h, :, 0:LANE] = _rms128(xn, gn_ref[...]).astype(k_ref.dtype)
        k_ref[0, h, :, LANE:2 * LANE] = r
        v_ref[0, h] = x_ref[0, :, (H + h) * LANE:(H + h + 1) * LANE].astype(v_ref.dtype)


def _mla(p, kr_block, pos, q_a_norm, kv_a_norm, w_uq, w_ukv, q_norm, k_norm):
    B, S, _ = p.shape
    H = MLA_HEADS
    N = B * S
    cqn = _norm(p, q_a_norm, width=MLA_Q_RANK, col_block=0)
    ckvn = _norm(p, kv_a_norm, width=MLA_KV_RANK, col_block=MLA_Q_RANK // MLA_KV_RANK)
    hh = np.arange(H)[:, None]
    q_perm = np.concatenate([(hh * 192 + np.arange(128)).ravel(), (hh * 192 + 128 + np.arange(64)).ravel()])
    kv_perm = np.concatenate([(hh * 256 + np.arange(128)).ravel(), (hh * 256 + 128 + np.arange(128)).ravel()])
    q_raw = _mm(cqn.reshape(N, -1), w_uq[:, q_perm].astype(CDT), tn=1024, tk=MLA_Q_RANK).reshape(B, S, -1)
    kv_raw = _mm(ckvn.reshape(N, -1), w_ukv[:, kv_perm].astype(CDT), tn=1024, tk=MLA_KV_RANK).reshape(B, S, -1)
    cos4, sin4 = _tables64pair(pos)
    ts = _tile(S, 256, 8)
    scale = (MLA_NOPE + MLA_ROPE) ** -0.5 * LOG2E
    gr = q_norm[MLA_NOPE:]
    tab = pl.BlockSpec((1, ts, LANE), lambda b, i: (b, i, 0))
    vec = pl.BlockSpec((1, LANE), lambda b, i: (0, 0))
    q = pl.pallas_call(
        functools.partial(_mla_qprep_kernel, H=H, scale=scale),
        out_shape=jax.ShapeDtypeStruct((B, H, S, 2 * LANE), CDT),
        grid=(B, S // ts),
        in_specs=[pl.BlockSpec((1, ts, H * 192), lambda b, i: (b, i, 0)), tab, tab, vec, vec],
        out_specs=pl.BlockSpec((1, H, ts, 2 * LANE), lambda b, i: (b, 0, i, 0)),
        compiler_params=_params(("parallel", "parallel")),
    )(q_raw, cos4, sin4, q_norm[:MLA_NOPE].reshape(1, -1), jnp.concatenate([gr, gr]).reshape(1, -1))
    gkr = jnp.concatenate([k_norm[MLA_NOPE:], jnp.zeros((64,), F32)])
    k, v = pl.pallas_call(
        functools.partial(_mla_kvprep_kernel, H=H),
        out_shape=(jax.ShapeDtypeStruct((B, H, S, 2 * LANE), CDT),
                   jax.ShapeDtypeStruct((B, H, S, LANE), CDT)),
        grid=(B, S // ts),
        in_specs=[pl.BlockSpec((1, ts, H * 256), lambda b, i: (b, i, 0)),
                  pl.BlockSpec((1, ts, LANE), lambda b, i: (b, i, kr_block)), tab, tab, vec, vec],
        out_specs=(pl.BlockSpec((1, H, ts, 2 * LANE), lambda b, i: (b, 0, i, 0)),
                   pl.BlockSpec((1, H, ts, LANE), lambda b, i: (b, 0, i, 0))),
        compiler_params=_params(("parallel", "parallel")),
    )(kv_raw, p, cos4, sin4, k_norm[:MLA_NOPE].reshape(1, -1), gkr.reshape(1, -1))
    return _flash(q, k, v, mode="causal", tq=1024, tk=1024, rb=256, heads_per_step=4)


def _nsa_prep_kernel(q_ref, kc_ref, vc_ref, ks_ref, vs_ref, kw_ref, vw_ref, cos_ref, sin_ref, g_ref,
                     qo, kso, vso, kwo, vwo, kco, vco, *, H, G, scale):
    cosf, sinf = cos_ref[0], sin_ref[0]
    for h in range(H):
        y = _rms128(q_ref[0, :, h * LANE:(h + 1) * LANE], g_ref[0:1, :])
        qo[0, h] = (_rope128(y, cosf, sinf) * scale).astype(qo.dtype)
    for g in range(G):
        sl = slice(g * LANE, (g + 1) * LANE)
        kso[0, g] = _rope128(_rms128(ks_ref[0, :, sl], g_ref[2:3, :]), cosf, sinf).astype(kso.dtype)
        kwo[0, g] = _rope128(_rms128(kw_ref[0, :, sl], g_ref[3:4, :]), cosf, sinf).astype(kwo.dtype)
        vso[0, g] = vs_ref[0, :, sl].astype(vso.dtype)
        vwo[0, g] = vw_ref[0, :, sl].astype(vwo.dtype)
        kco[0, g] = kc_ref[0, :, sl].astype(kco.dtype)
        vco[0, g] = vc_ref[0, :, sl].astype(vco.dtype)


def _compress_kernel(xk_ref, xv_ref, wk_ref, wv_ref, pek_ref, pev_ref, g_ref, cos_ref, sin_ref,
                     kc_ref, vc_ref):
    nc = xk_ref.shape[2]

    def comp(x_ref, w_ref, pe_ref):
        y = jnp.dot(x_ref[0, 0], w_ref[...], preferred_element_type=F32)
        c = jnp.dot(pe_ref[...], w_ref[...], preferred_element_type=F32)
        const = c[0:1, 0:LANE] + c[1:2, LANE:2 * LANE]
        return y[:, 0:LANE] + pltpu.roll(y[:, LANE:2 * LANE], nc - 1, 0) + const

    kc = comp(xk_ref, wk_ref, pek_ref)
    kc_ref[0, 0] = _rope128(_rms128(kc, g_ref[...]), cos_ref[0], sin_ref[0]).astype(kc_ref.dtype)
    vc_ref[0, 0] = comp(xv_ref, wv_ref, pev_ref).astype(vc_ref.dtype)


def _cmp_attn_kernel(q_ref, kc_ref, vc_ref, m_ref, o_ref, sel_ref, *, G, tq, n_cmp, n_sel, dv):
    i = pl.program_id(2)
    nc = kc_ref.shape[2]
    q = q_ref[0].reshape(G * tq, q_ref.shape[-1])
    s = lax.dot_general(q, kc_ref[0, 0], (((1,), (1,)), ((), ())), preferred_element_type=F32)
    s = s.reshape(G, tq, nc)
    t = i * tq + lax.broadcasted_iota(I32, (tq, nc), 0)
    n = lax.broadcasted_iota(I32, (tq, nc), 1)
    mask = (n * NSA_CMP_STRIDE + (NSA_CMP_LEN - 1) <= t) & (n < n_cmp)
    s = jnp.where(mask[None], s, NEG)
    mx = jnp.max(s, axis=-1, keepdims=True)
    e = jnp.where(mask[None], jnp.exp2(s - mx), 0.0)
    l = jnp.sum(e, axis=-1, keepdims=True)
    pc = e / jnp.where(l > 0.0, l, 1.0)
    o = jnp.dot(pc.reshape(G * tq, nc).astype(vc_ref.dtype), vc_ref[0, 0], preferred_element_type=F32)
    for g in range(G):
        o_ref[0, :, g * dv:(g + 1) * dv] = o[g * tq:(g + 1) * tq].astype(o_ref.dtype)

    ps = jnp.sum(pc, axis=0)
    hi = ps.astype(CDT)
    lo_part = (ps - hi.astype(F32)).astype(CDT)
    imp = (jnp.dot(hi, m_ref[...], preferred_element_type=F32)
           + jnp.dot(lo_part, m_ref[...], preferred_element_type=F32))
    blk = lax.broadcasted_iota(I32, (tq, LANE), 1)
    cur = (i * tq + lax.broadcasted_iota(I32, (tq, LANE), 0)) >> (NSA_SLC_LEN.bit_length() - 1)
    forced = (blk == 0) | (blk == cur) | (blk == cur - 1)
    imp = jnp.where(forced, FORCE, jnp.where(blk <= cur, imp, NEG))
    v = imp.T
    rowi = lax.broadcasted_iota(I32, (LANE, tq), 0)

    def take(_, carry):
        v, chosen = carry
        mval = jnp.max(v, axis=0, keepdims=True)
        first = jnp.min(jnp.where(v == mval, rowi, LANE), axis=0, keepdims=True)
        hit = rowi == first
        chosen = jnp.where(hit & (mval > NEG_HALF), 1.0, chosen)
        return jnp.where(hit, REMOVED, v), chosen

    _, chosen = lax.fori_loop(0, n_sel, take, (v, jnp.zeros((LANE, tq), F32)))
    sel_ref[0, 0] = chosen.T.astype(sel_ref.dtype)


def _nsa_combine_kernel(oc_ref, os_ref, ow_ref, g_ref, o_ref, *, H, dv):
    gate = _sigmoid(g_ref[0])
    for h in range(H):
        sl = slice(h * dv, (h + 1) * dv)
        o = (oc_ref[0, :, sl] * gate[:, h:h + 1] + os_ref[0, :, sl] * gate[:, H + h:H + h + 1]
             + ow_ref[0, :, sl] * gate[:, 2 * H + h:2 * H + h + 1])
        o_ref[0, :, sl] = o.astype(o_ref.dtype)


def _nsa(p, q_block, kv_block0, gate_block, pos, qk_norm, cmp_pos, cmp_w):
    B, S, _ = p.shape
    H, G, DH = NSA_HEADS, NSA_KV_GROUPS, NSA_HEAD_DIM
    HPG = H // G
    scale = DH ** -0.5 * LOG2E
    ts = _tile(S, 256, 8)
    cosf, sinf = _tables128(pos)
    tab = pl.BlockSpec((1, ts, LANE), lambda b, i: (b, i, 0))
    kvspec = [pl.BlockSpec((1, ts, G * DH), functools.partial(lambda b, i, m: (b, i, kv_block0 + m), m=m))
              for m in range(6)]
    head_out = lambda n: pl.BlockSpec((1, n, ts, DH), lambda b, i: (b, 0, i, 0))
    kv_shape = jax.ShapeDtypeStruct((B, G, S, DH), CDT)
    q, ks, vs, kw, vw, kcr, vcr = pl.pallas_call(
        functools.partial(_nsa_prep_kernel, H=H, G=G, scale=scale),
        out_shape=(jax.ShapeDtypeStruct((B, H, S, DH), CDT),) + (kv_shape,) * 6,
        grid=(B, S // ts),
        in_specs=[pl.BlockSpec((1, ts, H * DH), lambda b, i: (b, i, q_block))] + kvspec
                 + [tab, tab, pl.BlockSpec((4, DH), lambda b, i: (0, 0))],
        out_specs=(head_out(H),) + (head_out(G),) * 6,
        compiler_params=_params(("parallel", "parallel")),
    )(p, p, p, p, p, p, p, cosf, sinf, qk_norm)

    half = NSA_CMP_LEN // 2
    nc = S // NSA_CMP_STRIDE
    n_cmp = (S - NSA_CMP_LEN) // NSA_CMP_STRIDE + 1
    cmp_end = jnp.minimum(jnp.arange(nc) * NSA_CMP_STRIDE + NSA_CMP_LEN - 1, S - 1)
    ccos, csin = _tables128(pos[:, cmp_end])
    wcat = lambda w: jnp.concatenate([w[:half].reshape(half * DH, DH), w[half:].reshape(half * DH, DH)], 1).astype(CDT)
    pecat = lambda pe: jnp.zeros((8, half * DH), F32).at[0].set(pe[:half].reshape(-1)).at[1].set(
        pe[half:].reshape(-1)).astype(CDT)
    xspec = pl.BlockSpec((1, 1, nc, half * DH), lambda b, g: (b, g, 0, 0))
    wspec = pl.BlockSpec((half * DH, 2 * DH), lambda b, g: (0, 0))
    pespec = pl.BlockSpec((8, half * DH), lambda b, g: (0, 0))
    cspec = pl.BlockSpec((1, 1, nc, DH), lambda b, g: (b, g, 0, 0))
    ctab = pl.BlockSpec((1, nc, DH), lambda b, g: (b, 0, 0))
    kc, vc = pl.pallas_call(
        _compress_kernel,
        out_shape=(jax.ShapeDtypeStruct((B, G, nc, DH), CDT),) * 2,
        grid=(B, G),
        in_specs=[xspec, xspec, wspec, wspec, pespec, pespec,
                  pl.BlockSpec((1, DH), lambda b, g: (0, 0)), ctab, ctab],
        out_specs=(cspec, cspec),
        compiler_params=_params(("parallel", "parallel")),
    )(kcr.reshape(B, G, nc, half * DH), vcr.reshape(B, G, nc, half * DH), wcat(cmp_w[0]), wcat(cmp_w[1]),
      pecat(cmp_pos[0]), pecat(cmp_pos[1]), qk_norm[1].reshape(1, DH), ccos, csin)

    n_slc = S // NSA_SLC_LEN
    assert n_slc <= LANE
    r, cl = NSA_SLC_LEN // NSA_CMP_STRIDE, NSA_CMP_LEN // NSA_CMP_STRIDE
    m_np = np.zeros((nc, LANE), np.float32)
    for j in range(n_slc):
        for a in range(r):
            for c in range(cl):
                ci = j * r + a + c - (cl - 1)
                if 0 <= ci < n_cmp:
                    m_np[ci, j] += 1.0
    tq = _tile(S, 512, 8)
    o_c, sel = pl.pallas_call(
        functools.partial(_cmp_attn_kernel, G=HPG, tq=tq, n_cmp=n_cmp, n_sel=min(NSA_N_SEL, n_slc), dv=DH),
        out_shape=(jax.ShapeDtypeStruct((B, S, H * DH), CDT),
                   jax.ShapeDtypeStruct((B, G, S, LANE), CDT)),
        grid=(B, G, S // tq),
        in_specs=[pl.BlockSpec((1, HPG, tq, DH), lambda b, g, i: (b, g, i, 0)),
                  pl.BlockSpec((1, 1, nc, DH), lambda b, g, i: (b, g, 0, 0)),
                  pl.BlockSpec((1, 1, nc, DH), lambda b, g, i: (b, g, 0, 0)),
                  pl.BlockSpec((nc, LANE), lambda b, g, i: (0, 0))],
        out_specs=(pl.BlockSpec((1, tq, HPG * DH), lambda b, g, i: (b, i, g)),
                   pl.BlockSpec((1, 1, tq, LANE), lambda b, g, i: (b, g, i, 0))),
        compiler_params=_params(("parallel", "parallel", "parallel")),
    )(q, kc, vc, jnp.asarray(m_np, CDT))

    expand = jnp.asarray((np.arange(S)[None, :] // NSA_SLC_LEN) == np.arange(LANE)[:, None], CDT)
    o_s = _flash(q, ks, vs, mode="sel", sel=sel, expand=expand, tq=1024, tk=1024)
    o_w = _flash(q, kw, vw, mode="window", window=NSA_WINDOW, tq=512, tk=512)
    ospec = pl.BlockSpec((1, ts, H * DH), lambda b, i: (b, i, 0))
    return pl.pallas_call(
        functools.partial(_nsa_combine_kernel, H=H, dv=DH),
        out_shape=jax.ShapeDtypeStruct((B, S, H * DH), CDT),
        grid=(B, S // ts),
        in_specs=[ospec, ospec, ospec, pl.BlockSpec((1, ts, LANE), lambda b, i: (b, i, gate_block))],
        out_specs=ospec,
        compiler_params=_params(("parallel", "parallel")),
    )(o_c, o_s, o_w, p)


def _dsa_prep_kernel(q_ref, k_ref, v_ref, qi_ref, ki_ref, wi_ref, cos_ref, sin_ref, cos4_ref, sin4_ref,
                     g_ref, gi_ref, qo, ko, vo, qio, kilo, kihi, wio, *, H, HKV, scale, wscale):
    ts = q_ref.shape[1]
    lane = lax.broadcasted_iota(I32, (ts, LANE), 1)
    cosf, sinf, cos4, sin4 = cos_ref[0], sin_ref[0], cos4_ref[0], sin4_ref[0]
    for h in range(H):
        y = _rms128(q_ref[0, :, h * LANE:(h + 1) * LANE], g_ref[0:1, :])
        qo[0, h] = (_rope128(y, cosf, sinf) * scale).astype(qo.dtype)
    for h in range(HKV):
        sl = slice(h * LANE, (h + 1) * LANE)
        ko[0, h] = _rope128(_rms128(k_ref[0, :, sl], g_ref[1:2, :]), cosf, sinf).astype(ko.dtype)
        vo[0, h] = v_ref[0, :, sl].astype(vo.dtype)
    for j in range(qi_ref.shape[2] // LANE):
        qio[0, j] = _rope64pair(qi_ref[0, :, j * LANE:(j + 1) * LANE], cos4, sin4, lane).astype(qio.dtype)
    ki = ki_ref[0]
    inv = lax.rsqrt(_sumsq128(ki) * (1.0 / 64.0) + EPS)
    r = _rope64pair(ki * inv * gi_ref[...], cos4, sin4, lane)
    kilo[0] = r.astype(kilo.dtype)
    kihi[0] = pltpu.roll(r, 64, 1).astype(kihi.dtype)
    wio[0] = wi_ref[0] * wscale


def _indexer_kernel(qi_ref, kj_ref, fl_ref, q_ref, klo_ref, khi_ref, w_ref, o_ref, wb_sc, sc_sc, acc_sc,
                    *, tq, tk, topk, n_pairs, rg):
    p = pl.program_id(1)
    qi, kj, fl = qi_ref[p], kj_ref[p], fl_ref[p]
    n_tiles = o_ref.shape[1]

    @pl.when((fl & 1) != 0)
    def _():
        w = w_ref[0]
        for h in range(2 * n_pairs):
            wb_sc[h] = jnp.broadcast_to(w[:, h:h + 1], (tq, LANE))

    acc_sc[...] = jnp.zeros((tq, tk), F32)
    cw = min(tk, 2 * LANE)

    def pair(j, _):
        q = q_ref[0, j]
        wa = jnp.tile(wb_sc[2 * j], (1, cw // LANE))
        wb = jnp.tile(wb_sc[2 * j + 1], (1, cw // LANE))
        for c in range(tk // cw):
            cols = slice(c * cw, (c + 1) * cw)
            sa = lax.dot_general(q, klo_ref[0, cols, :], (((1,), (1,)), ((), ())), preferred_element_type=F32)
            sb = lax.dot_general(q, khi_ref[0, cols, :], (((1,), (1,)), ((), ())), preferred_element_type=F32)
            acc_sc[:, cols] += wa * jnp.maximum(sa, 0.0) + wb * jnp.maximum(sb, 0.0)
        return 0

    lax.fori_loop(0, n_pairs, pair, 0, unroll=8)
    score = acc_sc[...]
    row = qi * tq + lax.broadcasted_iota(I32, (tq, tk), 0)
    col = kj * tk + lax.broadcasted_iota(I32, (tq, tk), 1)
    score = jnp.where(col <= row, score, NEG)
    bits = pltpu.bitcast(score, I32)
    key = bits ^ ((bits >> 31) & 0x7FFFFFFF)
    sc_sc[kj] = key

    @pl.when((fl & 2) != 0)
    def _():
        n_chunks = kj + 1
        nh_bits = int(np.float32(NEG_HALF).view(np.int32))
        key_neg_half = nh_bits ^ 0x7FFFFFFF if nh_bits < 0 else nh_bits
        for g in range(tq // rg):
            rows = pl.ds(g * rg, rg)

            def bit_step(state):
                b, thr, n_ge, _ = state
                cand = thr + jnp.left_shift(jnp.int32(1), 31 - b)

                def count(c, cnt):
                    blk = sc_sc[c, rows, :]
                    for u in range(tk // LANE):
                        cnt = cnt + (blk[:, u * LANE:(u + 1) * LANE] >= cand).astype(I32)
                    return cnt

                cnt = lax.fori_loop(0, n_chunks, count, jnp.zeros((rg, LANE), I32))
                tot = jnp.sum(cnt, axis=1, keepdims=True)
                take = tot >= topk
                n_ge = jnp.where(take, tot, n_ge)
                return b + 1, jnp.where(take, cand, thr), n_ge, jnp.max(n_ge)

            start = (jnp.int32(0), jnp.full((rg, LANE), -2**31, I32),
                     jnp.full((rg, LANE), 2**30, I32), jnp.int32(2**30))
            _, thr, _, _ = lax.while_loop(lambda st: (st[0] < 32) & (st[3] > topk), bit_step, start)
            thr = jnp.maximum(thr, key_neg_half + 1)
            thr_t = jnp.tile(thr, (1, tk // LANE))

            def emit(c, _):
                o_ref[0, c, rows, :] = jnp.where(sc_sc[c, rows, :] >= thr_t, 0.0, NEG).astype(o_ref.dtype)
                return 0

            def emit_masked(c, _):
                o_ref[0, c, rows, :] = jnp.full((rg, tk), NEG, o_ref.dtype)
                return 0

            lax.fori_loop(0, n_chunks, emit, 0)
            lax.fori_loop(n_chunks, n_tiles, emit_masked, 0)


def _dsa(p, pos, qk_norm, idx_k_norm):
    B, S, _ = p.shape
    H, HKV, DH = DSA_HEADS, DSA_KV_HEADS, DSA_HEAD_DIM
    NP = IDX_HEADS // 2
    ts = _tile(S, 256, 8)
    cosf, sinf = _tables128(pos)
    cos4, sin4 = _tables64pair(pos)
    tab = pl.BlockSpec((1, ts, LANE), lambda b, i: (b, i, 0))
    kw = HKV * DH
    gi = jnp.concatenate([idx_k_norm, jnp.zeros((LANE - IDX_DIM,), F32)]).reshape(1, LANE)
    head_out = lambda n: pl.BlockSpec((1, n, ts, DH), lambda b, i: (b, 0, i, 0))
    q, k, v, qidx, kilo, kihi, wi = pl.pallas_call(
        functools.partial(_dsa_prep_kernel, H=H, HKV=HKV, scale=DH ** -0.5 * LOG2E,
                          wscale=IDX_HEADS ** -0.5 * IDX_DIM ** -0.5),
        out_shape=(jax.ShapeDtypeStruct((B, H, S, DH), CDT), jax.ShapeDtypeStruct((B, HKV, S, DH), CDT),
                   jax.ShapeDtypeStruct((B, HKV, S, DH), CDT), jax.ShapeDtypeStruct((B, NP, S, LANE), CDT),
                   jax.ShapeDtypeStruct((B, S, LANE), CDT), jax.ShapeDtypeStruct((B, S, LANE), CDT),
                   jax.ShapeDtypeStruct((B, S, LANE), F32)),
        grid=(B, S // ts),
        in_specs=[pl.BlockSpec((1, ts, H * DH), lambda b, i: (b, i, 0)),
                  pl.BlockSpec((1, ts, kw), lambda b, i: (b, i, H * DH // kw)),
                  pl.BlockSpec((1, ts, kw), lambda b, i: (b, i, H * DH // kw + 1)),
                  pl.BlockSpec((1, ts, NP * LANE), lambda b, i: (b, i, (H * DH + 2 * kw) // (NP * LANE))),
                  pl.BlockSpec((1, ts, LANE), lambda b, i: (b, i, (H * DH + 2 * kw + NP * LANE) // LANE)),
                  pl.BlockSpec((1, ts, LANE), lambda b, i: (b, i, (H * DH + 2 * kw + NP * LANE) // LANE + 1)),
                  tab, tab, tab, tab,
                  pl.BlockSpec((2, DH), lambda b, i: (0, 0)), pl.BlockSpec((1, LANE), lambda b, i: (0, 0))],
        out_specs=(head_out(H), head_out(HKV), head_out(HKV), head_out(NP), tab, tab, tab),
        compiler_params=_params(("parallel", "parallel")),
    )(p, p, p, p, p, p, cosf, sinf, cos4, sin4, qk_norm, gi)

    topk = min(DSA_TOPK_MAX, S // 4)
    tq, tk = _tile(S, 256, 8), _tile(S, 1024)
    qi_t, kj_t, fl_t = _pairs(S, tq, tk)
    bias = pl.pallas_call(
        functools.partial(_indexer_kernel, tq=tq, tk=tk, topk=topk, n_pairs=NP, rg=min(128, tq)),
        out_shape=jax.ShapeDtypeStruct((B, S // tk, S, tk), CDT),
        grid_spec=pltpu.PrefetchScalarGridSpec(
            num_scalar_prefetch=3,
            grid=(B, int(qi_t.shape[0])),
            in_specs=[pl.BlockSpec((1, NP, tq, LANE), lambda b, p, qi, kj, fl: (b, 0, qi[p], 0)),
                      pl.BlockSpec((1, tk, LANE), lambda b, p, qi, kj, fl: (b, kj[p], 0)),
                      pl.BlockSpec((1, tk, LANE), lambda b, p, qi, kj, fl: (b, kj[p], 0)),
                      pl.BlockSpec((1, tq, LANE), lambda b, p, qi, kj, fl: (b, qi[p], 0))],
            out_specs=pl.BlockSpec((1, S // tk, tq, tk), lambda b, p, qi, kj, fl: (b, 0, qi[p], 0)),
            scratch_shapes=[pltpu.VMEM((2 * NP, tq, LANE), F32), pltpu.VMEM((S // tk, tq, tk), I32),
                            pltpu.VMEM((tq, tk), F32)]),
        compiler_params=_params(("parallel", "arbitrary")),
    )(qi_t, kj_t, fl_t, qidx, kilo, kihi, wi)
    return _flash(q, k, v, mode="bias", bias=bias, tq=4 * tq, tk=tk)


def _router_kernel(l_ref, i_ref, p_ref, *, n_experts):
    x = l_ref[...]
    lane = lax.broadcasted_iota(I32, x.shape, 1)
    x = jnp.where(lane < n_experts, x, -jnp.inf)
    m1 = jnp.max(x, axis=1, keepdims=True)
    i1 = jnp.min(jnp.where(x == m1, lane, LANE), axis=1, keepdims=True)
    x2 = jnp.where(lane == i1, -jnp.inf, x)
    m2 = jnp.max(x2, axis=1, keepdims=True)
    i2 = jnp.min(jnp.where(x2 == m2, lane, LANE), axis=1, keepdims=True)
    e2 = jnp.exp(m2 - m1)
    p1 = 1.0 / (1.0 + e2)
    p2 = e2 / (1.0 + e2)
    i_ref[...] = jnp.where(lane == 0, i1, jnp.where(lane == 1, i2, 0))
    p_ref[...] = jnp.where(lane == 0, p1, jnp.where(lane == 1, p2, 0.0))


def _row_copy(src_hbm, row, dst_vmem, r, sem):
    return pltpu.make_async_copy(src_hbm.at[pl.ds(row, 1), :], dst_vmem.at[pl.ds(r, 1), :], sem)


def _gather_pipeline(n_steps, tm, copies):
    i = pl.program_id(0)
    slot = i % 2

    def start_tile(tile, slot):
        def body(r, _):
            for src, idx_ref, dst, sem in copies(slot):
                _row_copy(src, idx_ref[tile * tm + r], dst, r, sem).start()
            return 0
        lax.fori_loop(0, tm, body, 0)

    def wait_tile(slot):
        def body(r, _):
            for src, _, dst, sem in copies(slot):
                _row_copy(src, 0, dst, r, sem).wait()
            return 0
        lax.fori_loop(0, tm, body, 0)

    @pl.when(i == 0)
    def _():
        start_tile(0, 0)

    @pl.when(i + 1 < n_steps)
    def _():
        start_tile(i + 1, 1 - slot)

    wait_tile(slot)
    return slot


def _dispatch_kernel(tok_ref, h_hbm, o_ref, buf, sem, *, tm, n_steps):
    slot = _gather_pipeline(n_steps, tm, lambda s: [(h_hbm, tok_ref, buf.at[s], sem.at[s])])
    o_ref[...] = buf[slot].astype(o_ref.dtype)


def _combine_kernel(s1_ref, s2_ref, y_hbm, x_ref, g_ref, p_ref, o_ref, buf1, buf2, sem, *, tm, n_steps):
    slot = _gather_pipeline(n_steps, tm, lambda s: [(y_hbm, s1_ref, buf1.at[s], sem.at[s]),
                                                    (y_hbm, s2_ref, buf2.at[s], sem.at[s])])
    p = p_ref[...]
    o_ref[...] = x_ref[...] + g_ref[0] * (p[:, 0:1] * buf1[slot] + p[:, 1:2] * buf2[slot])


def _moe(h, h32, x, g_f, w_router, w_gate, w_up, w_down, S):
    N, D = h.shape
    E, _, DE = w_gate.shape
    wr = jnp.zeros((D, LANE), F32).at[:, :E].set(w_router).astype(CDT)
    logits = _mm(h, wr, tn=LANE, tk=D)
    tr = _tile(N, 1024, 8)
    spec = pl.BlockSpec((tr, LANE), lambda i: (i, 0))
    idx, prob = pl.pallas_call(
        functools.partial(_router_kernel, n_experts=E),
        out_shape=(jax.ShapeDtypeStruct((N, LANE), I32), jax.ShapeDtypeStruct((N, LANE), F32)),
        grid=(N // tr,),
        in_specs=[spec],
        out_specs=(spec, spec),
        compiler_params=_params(("parallel",)),
    )(logits)

    tm = min(512, N)
    n_rows = 2 * N + E * tm
    e_flat = jnp.concatenate([idx[:, 0], idx[:, 1]])
    onehot = (e_flat[:, None] == jnp.arange(E)[None, :]).astype(I32)
    csum = jnp.cumsum(onehot, axis=0)
    rank = jnp.take_along_axis(csum, e_flat[:, None], axis=1)[:, 0] - 1
    padded = (csum[-1] + tm - 1) // tm * tm
    ends = jnp.cumsum(padded)
    pos = (ends - padded)[e_flat] + rank
    tile_expert = jnp.minimum(jnp.sum(jnp.arange(n_rows // tm)[:, None] * tm >= ends[None, :], axis=1), E - 1)
    tok = jnp.tile(jnp.arange(N, dtype=I32), 2)
    row_token = jnp.zeros((n_rows,), I32).at[pos].set(tok)

    tg = min(256, N)
    xs = pl.pallas_call(
        functools.partial(_dispatch_kernel, tm=tg, n_steps=n_rows // tg),
        out_shape=jax.ShapeDtypeStruct((n_rows, D), CDT),
        grid_spec=pltpu.PrefetchScalarGridSpec(
            num_scalar_prefetch=1,
            grid=(n_rows // tg,),
            in_specs=[pl.BlockSpec(memory_space=pl.ANY)],
            out_specs=pl.BlockSpec((tg, D), lambda i, tok: (i, 0)),
            scratch_shapes=[pltpu.VMEM((2, tg, D), F32), pltpu.SemaphoreType.DMA((2,))]),
        compiler_params=_params(("arbitrary",)),
    )(row_token, h32)
    te = tile_expert.astype(I32)
    hid = _mm(xs, w_gate.astype(CDT), mode="swiglu", b2=w_up.astype(CDT), group=te, out_dtype=CDT,
              tm=tm, tn=DE, tk=2048)
    ys = _mm(hid, w_down.astype(CDT), group=te, tm=tm, tn=2048, tk=DE)
    return pl.pallas_call(
        functools.partial(_combine_kernel, tm=tg, n_steps=N // tg),
        out_shape=jax.ShapeDtypeStruct((N, D), F32),
        grid_spec=pltpu.PrefetchScalarGridSpec(
            num_scalar_prefetch=2,
            grid=(N // tg,),
            in_specs=[pl.BlockSpec(memory_space=pl.ANY),
                      pl.BlockSpec((tg, D), lambda i, s1, s2: (i, 0)),
                      pl.BlockSpec((1, 1, D), lambda i, s1, s2: ((i * tg) // S, 0, 0)),
                      pl.BlockSpec((tg, LANE), lambda i, s1, s2: (i, 0))],
            out_specs=pl.BlockSpec((tg, D), lambda i, s1, s2: (i, 0)),
            scratch_shapes=[pltpu.VMEM((2, tg, D), F32), pltpu.VMEM((2, tg, D), F32),
                            pltpu.SemaphoreType.DMA((2,))]),
        compiler_params=_params(("arbitrary",)),
    )(pos[:N].astype(I32), pos[N:].astype(I32), ys, x, g_f, prob)


def _pad_cols(blocks, total):
    cols = []
    for w, width in blocks:
        cols.append(w)
        if width > w.shape[1]:
            cols.append(jnp.zeros((w.shape[0], width - w.shape[1]), w.dtype))
    out = jnp.concatenate(cols, axis=1)
    if total > out.shape[1]:
        out = jnp.concatenate([out, jnp.zeros((out.shape[0], total - out.shape[1]), out.dtype)], axis=1)
    return out.astype(CDT)


def _round_up(n, m):
    return (n + m - 1) // m * m


def kernel(x, c, positions, ada_w, ada_b, ada_table, norm_g, ev_w_in, ev_w_out, mla_q_a_norm, mla_kv_a_norm, mla_w_uq, mla_w_ukv, mla_q_norm, mla_k_norm, nsa_qk_norm, nsa_cmp_pos, nsa_cmp_w, ffn_w_gate, ffn_w_up, ffn_w_down, od_w_in, od_w_out, dsa_qk_norm, idx_k_norm, moe_router, moe_w_gate, moe_w_up, moe_w_down):
    B, S, D = x.shape
    N = B * S
    depth = ada_table.shape[0]
    cond = _cond(c, ada_w, ada_b).reshape(B, 6, D)

    mla_in = MLA_Q_RANK + MLA_KV_RANK + MLA_ROPE
    nq = NSA_HEADS * NSA_HEAD_DIM
    nkv = 6 * NSA_KV_GROUPS * NSA_HEAD_DIM
    hn = np.arange(NSA_HEADS)
    gate_perm = np.concatenate([hn * 3 + r for r in range(3)])

    x2 = x.reshape(N, D)
    for l in range(depth):
        i = l // 2
        mod = cond + ada_table[l]
        sh_a, sc_a, g_a, sh_f, sc_f, g_f = [mod[:, j, None, :] for j in range(6)]
        h = _norm(x2.reshape(B, S, D), norm_g[l, 0], sc_a, sh_a).reshape(N, D)
        if l % 2 == 0:
            w = ev_w_in[i]
            nsa = w[:, mla_in:]
            blocks = [(w[:, :MLA_Q_RANK + MLA_KV_RANK], MLA_Q_RANK + MLA_KV_RANK),
                      (nsa[:, :nq + nkv], nq + nkv),
                      (w[:, MLA_Q_RANK + MLA_KV_RANK:mla_in], LANE),
                      (nsa[:, nq + nkv:][:, gate_perm], LANE)]
            width = MLA_Q_RANK + MLA_KV_RANK + nq + nkv + 2 * LANE
            w_in = _pad_cols(blocks, _round_up(width, 512))
            p = _mm(h, w_in).reshape(B, S, -1)
            off = MLA_Q_RANK + MLA_KV_RANK
            a_out = _mla(p, (off + nq + nkv) // LANE, positions, mla_q_a_norm[i], mla_kv_a_norm[i],
                         mla_w_uq[i], mla_w_ukv[i], mla_q_norm[i], mla_k_norm[i])
            b_out = _nsa(p, off // nq, (off + nq) // (NSA_KV_GROUPS * NSA_HEAD_DIM),
                         (off + nq + nkv) // LANE + 1, positions, nsa_qk_norm[i], nsa_cmp_pos[i], nsa_cmp_w[i])
            mix = (a_out.reshape(N, -1), b_out.reshape(N, -1))
            w_out = ev_w_out[i]
        else:
            w = od_w_in[i]
            main = DSA_HEADS * DSA_HEAD_DIM + 2 * DSA_KV_HEADS * DSA_HEAD_DIM + IDX_HEADS * IDX_DIM
            blocks = [(w[:, :main], main), (w[:, main:main + IDX_DIM], LANE), (w[:, main + IDX_DIM:], LANE)]
            w_in = _pad_cols(blocks, _round_up(main + 2 * LANE, 512))
            p = _mm(h, w_in).reshape(B, S, -1)
            mix = _dsa(p, positions, dsa_qk_norm[i], idx_k_norm[i]).reshape(N, -1)
            w_out = od_w_out[i]
        x2 = _mm(mix, w_out.astype(CDT), mode="res", x=x2, g=g_a, rows_per_batch=S)
        if l % 2 == 0:
            h = _norm(x2.reshape(B, S, D), norm_g[l, 1], sc_f, sh_f).reshape(N, D)
            hid = _mm(h, ffn_w_gate[i].astype(CDT), mode="swiglu", b2=ffn_w_up[i].astype(CDT), out_dtype=CDT)
            x2 = _mm(hid, ffn_w_down[i].astype(CDT), mode="res", x=x2, g=g_f, tn=1024, tk=2048,
                     rows_per_batch=S)
        else:
            h, h32 = _norm(x2.reshape(B, S, D), norm_g[l, 1], sc_f, sh_f, also_f32=True)
            x2 = _moe(h.reshape(N, D), h32.reshape(N, D), x2, g_f, moe_router[i], moe_w_gate[i], moe_w_up[i],
                      moe_w_down[i], S)
    return x2.reshape(B, S, D)
```

```python
import functools

import numpy as np
import jax
import jax.numpy as jnp
from jax import lax
from jax.experimental import pallas as pl
from jax.experimental.pallas import tpu as pltpu

F32 = jnp.float32
I32 = jnp.int32
CDT = jnp.bfloat16

ROPE_THETA = 10000.0
EPS = 1e-6
NEG = -1e30
NEG_HALF = -5e29
FORCE = 1e9
REMOVED = -3e38
LOG2E = 1.4426950408889634

MLA_HEADS, MLA_Q_RANK, MLA_KV_RANK, MLA_NOPE, MLA_ROPE, MLA_V = 16, 1536, 512, 128, 64, 128
NSA_HEADS, NSA_KV_GROUPS, NSA_HEAD_DIM = 16, 4, 128
NSA_CMP_LEN, NSA_CMP_STRIDE, NSA_SLC_LEN, NSA_N_SEL, NSA_WINDOW = 32, 16, 64, 16, 512
DSA_HEADS, DSA_KV_HEADS, DSA_HEAD_DIM, IDX_HEADS, IDX_DIM, DSA_TOPK_MAX = 32, 8, 128, 32, 64, 256
N_EXPERTS = 8

LANE = 128
VMEM_LIMIT = 56 * 2**20


def _tile(n, pref, mult=LANE):
    if n <= pref:
        return n
    t = (pref // mult) * mult
    while t >= mult:
        if n % t == 0:
            return t
        t -= mult
    return n


def _params(sem):
    return pltpu.CompilerParams(dimension_semantics=sem, vmem_limit_bytes=VMEM_LIMIT)


def _silu(x):
    return x / (1.0 + jnp.exp(-x))


def _sigmoid(x):
    return 1.0 / (1.0 + jnp.exp(-x))


def _rms(x, g):
    return x * lax.rsqrt(jnp.mean(x * x, axis=-1, keepdims=True) + EPS) * g


def _sumsq128(x, halves=False):
    sq = x * x
    hi = sq.astype(CDT)
    lo = (sq - hi.astype(F32)).astype(CDT)
    if halves:
        r = lax.broadcasted_iota(I32, (LANE, LANE), 0)
        c = lax.broadcasted_iota(I32, (LANE, LANE), 1)
        ones = ((r < 64) == (c < 64)).astype(CDT)
    else:
        ones = jnp.ones((LANE, LANE), CDT)
    return (jnp.dot(hi, ones, preferred_element_type=F32) + jnp.dot(lo, ones, preferred_element_type=F32))


def _rms128(x, g):
    return x * lax.rsqrt(_sumsq128(x) * (1.0 / LANE) + EPS) * g


def _rope128(y, cosf, sinf):
    return y * cosf + pltpu.roll(y, 64, 1) * sinf


def _rope64pair(y, cos4, sin4, lane):
    rot = jnp.where((lane & 63) < 32, pltpu.roll(y, 96, 1), pltpu.roll(y, 32, 1))
    return y * cos4 + rot * sin4


def _mm_kernel(*refs, nk, mode, grouped, split_a):
    if grouped:
        refs = refs[1:]
    k = pl.program_id(2)
    a_refs = refs[:2] if split_a else refs[:1]
    if split_a:
        refs = refs[1:]
    n_in = {"plain": 2, "res": 4, "swiglu": 3}[mode]
    o_ref = refs[n_in]
    acc, acc2 = (tuple(refs[n_in + 1:]) + (None, None))[:2]
    if mode == "plain":
        b_ref, = refs[1:n_in]
    elif mode == "res":
        b_ref, x_ref, g_ref = refs[1:n_in]
    else:
        b_ref, b2_ref = refs[1:n_in]

    def product(w_ref):
        ka = a_refs[0].shape[1]
        return sum(jnp.dot(r[...], w_ref[n * ka:(n + 1) * ka, :].astype(r.dtype), preferred_element_type=F32)
                   for n, r in enumerate(a_refs))

    part = product(b_ref)
    if mode == "swiglu":
        part2 = product(b2_ref)

    if nk > 1:
        @pl.when(k == 0)
        def _():
            acc[...] = part
            if mode == "swiglu":
                acc2[...] = part2

        @pl.when(k > 0)
        def _():
            acc[...] += part
            if mode == "swiglu":
                acc2[...] += part2

    def finish():
        r = acc[...] if nk > 1 else part
        if mode == "plain":
            o_ref[...] = r.astype(o_ref.dtype)
        elif mode == "res":
            o_ref[...] = (x_ref[...] + g_ref[0] * r).astype(o_ref.dtype)
        else:
            r2 = acc2[...] if nk > 1 else part2
            o_ref[...] = (_silu(r) * r2).astype(o_ref.dtype)

    if nk > 1:
        pl.when(k == nk - 1)(finish)
    else:
        finish()


def _mm(a, b, *, mode="plain", b2=None, x=None, g=None, group=None, out_dtype=F32,
        tm=1024, tn=512, tk=4096, rows_per_batch=None):
    split_a = isinstance(a, tuple)
    a_list = list(a) if split_a else [a]
    M, Ka = a_list[0].shape
    K = Ka * len(a_list)
    N = b.shape[-1]
    tm, tn, tk = _tile(M, tm, 8), _tile(N, tn), (K if split_a else _tile(K, tk))
    nk = K // tk
    grid = (M // tm, N // tn, nk)
    grouped = group is not None
    if grouped:
        assert group.shape == (M // tm,)
        b_spec = pl.BlockSpec((None, tk, tn), lambda i, j, k, ge: (ge[i], k, j))
    else:
        b_spec = pl.BlockSpec((tk, tn), lambda i, j, k, *_: (k, j))
    if split_a:
        in_specs = [pl.BlockSpec((tm, Ka), lambda i, j, k, *_: (i, 0))] * 2 + [b_spec]
    else:
        in_specs = [pl.BlockSpec((tm, tk), lambda i, j, k, *_: (i, k)), b_spec]
    args = a_list + [b]
    scratch = [pltpu.VMEM((tm, tn), F32)] if nk > 1 else []
    if mode == "swiglu":
        in_specs.append(b_spec)
        args.append(b2)
        scratch = scratch * 2
    if mode == "res":
        rpb = rows_per_batch
        assert rpb % tm == 0
        in_specs += [pl.BlockSpec((tm, tn), lambda i, j, k, *_: (i, j)),
                     pl.BlockSpec((1, 1, tn), lambda i, j, k, *_: ((i * tm) // rpb, 0, j))]
        args += [x, g]
    return pl.pallas_call(
        functools.partial(_mm_kernel, nk=nk, mode=mode, grouped=grouped, split_a=split_a),
        out_shape=jax.ShapeDtypeStruct((M, N), out_dtype),
        grid_spec=pltpu.PrefetchScalarGridSpec(
            num_scalar_prefetch=1 if grouped else 0,
            grid=grid,
            in_specs=in_specs,
            out_specs=pl.BlockSpec((tm, tn), lambda i, j, k, *_: (i, j)),
            scratch_shapes=scratch),
        compiler_params=_params(("parallel", "parallel", "arbitrary")),
    )(*(([group] if grouped else []) + args))


def _cond_kernel(c_ref, w_ref, b_ref, o_ref):
    a = _silu(c_ref[...]).astype(CDT)
    o_ref[...] = jnp.dot(a, w_ref[...].astype(CDT), preferred_element_type=F32) + b_ref[...]


def _cond(c, ada_w, ada_b):
    B, D = c.shape
    N = ada_w.shape[1]
    cp = jnp.zeros((8, D), F32).at[:B].set(c)
    tn = _tile(N, 512)
    out = pl.pallas_call(
        _cond_kernel,
        out_shape=jax.ShapeDtypeStruct((8, N), F32),
        grid=(N // tn,),
        in_specs=[pl.BlockSpec((8, D), lambda j: (0, 0)),
                  pl.BlockSpec((D, tn), lambda j: (0, j)),
                  pl.BlockSpec((1, tn), lambda j: (0, j))],
        out_specs=pl.BlockSpec((8, tn), lambda j: (0, j)),
        compiler_params=_params(("parallel",)),
    )(cp, ada_w, ada_b.reshape(1, N))
    return out[:B]


def _norm_kernel(*refs, modulate, n_out):
    outs = refs[len(refs) - n_out:]
    if modulate:
        x_ref, g_ref, sc_ref, sh_ref = refs[:4]
    else:
        x_ref, g_ref = refs[:2]
    y = _rms(x_ref[0], g_ref[...])
    if modulate:
        y = y * (1.0 + sc_ref[0]) + sh_ref[0]
    for o_ref in outs:
        o_ref[0] = y.astype(o_ref.dtype)


def _norm(x, g, sc=None, sh=None, *, width=None, col_block=0, ts=512, also_f32=False):
    B, S, W = x.shape
    width = W if width is None else width
    ts = _tile(S, ts, 8)
    modulate = sc is not None
    in_specs = [pl.BlockSpec((1, ts, width), lambda b, i: (b, i, col_block)),
                pl.BlockSpec((1, width), lambda b, i: (0, 0))]
    args = [x, g.reshape(1, width)]
    if modulate:
        in_specs += [pl.BlockSpec((1, 1, width), lambda b, i: (b, 0, 0))] * 2
        args += [sc, sh]
    dtypes = (CDT, F32) if also_f32 else (CDT,)
    out = pl.pallas_call(
        functools.partial(_norm_kernel, modulate=modulate, n_out=len(dtypes)),
        out_shape=tuple(jax.ShapeDtypeStruct((B, S, width), d) for d in dtypes),
        grid=(B, S // ts),
        in_specs=in_specs,
        out_specs=tuple(pl.BlockSpec((1, ts, width), lambda b, i: (b, i, 0)) for _ in dtypes),
        compiler_params=_params(("parallel", "parallel")),
    )(*args)
    return out if also_f32 else out[0]


def _rope_angles(pos, dim):
    inv = ROPE_THETA ** (-jnp.arange(0, dim, 2, dtype=F32) / dim)
    ang = pos.astype(F32)[..., None] * inv
    return jnp.cos(ang), jnp.sin(ang)


def _tables128(pos):
    c, s = _rope_angles(pos, 128)
    return jnp.concatenate([c, c], -1), jnp.concatenate([-s, s], -1)


def _tables64pair(pos):
    c, s = _rope_angles(pos, 64)
    return jnp.concatenate([c, c, c, c], -1), jnp.concatenate([-s, s, -s, s], -1)


def _pairs(S, tq, tk, window=None):
    qi, kj, fl = [], [], []
    for i in range(S // tq):
        lo = 0 if window is None else max(0, i * tq - window + 1)
        js = list(range(lo // tk, (i * tq + tq - 1) // tk + 1))
        for n, j in enumerate(js):
            diag = (j + 1) * tk - 1 > i * tq
            f = (1 if n == 0 else 0) | (2 if n == len(js) - 1 else 0)
            f |= 4 if (diag or window is not None) else 0
            qi.append(i), kj.append(j), fl.append(f)
    return (jnp.asarray(qi, I32), jnp.asarray(kj, I32), jnp.asarray(fl, I32))


def _flash_kernel(qi_ref, kj_ref, fl_ref, q_ref, k_ref, v_ref, *rest, G, tq, tk, rb, mode, window):
    if mode == "bias":
        bias_ref, o_ref, m_sc, acc_sc = rest
    elif mode == "sel":
        sel_ref, e_ref, o_ref, m_sc, acc_sc = rest
    else:
        o_ref, m_sc, acc_sc = rest
    p = pl.program_id(2)
    qi, kj, fl = qi_ref[p], kj_ref[p], fl_ref[p]
    dk = q_ref.shape[-1]
    dv = v_ref.shape[-1]
    R = G * tq

    @pl.when((fl & 1) != 0)
    def _():
        m_sc[...] = jnp.full(m_sc.shape, NEG, F32)
        acc_sc[...] = jnp.zeros(acc_sc.shape, F32)

    def compute(position_mask):
        mask = None
        if position_mask:
            row = qi * tq + lax.broadcasted_iota(I32, (tq, tk), 0)
            col = kj * tk + lax.broadcasted_iota(I32, (tq, tk), 1)
            mask = col <= row
            if window is not None:
                mask = mask & (col > row - window)
        if mode == "sel":
            hit = jnp.dot(sel_ref[0, 0], e_ref[...], preferred_element_type=F32) > 0.5
            mask = hit if mask is None else (mask & hit)
        ones = jnp.ones((tk, LANE), v_ref.dtype)
        v_exts = [jnp.concatenate([v_ref[0, h], ones], axis=1) for h in range(v_ref.shape[1])]
        for r in range(R // rb):
            g, t0 = divmod(r * rb, tq)
            kv = g % k_ref.shape[1]
            v_ext = v_exts[kv]
            rows = slice(r * rb, (r + 1) * rb)
            s = lax.dot_general(q_ref[0, g, t0:t0 + rb, :], k_ref[0, kv], (((1,), (1,)), ((), ())),
                                preferred_element_type=F32)
            if mode == "bias":
                s = s + bias_ref[0, 0, t0:t0 + rb, :].astype(F32)
            if mask is not None:
                s = jnp.where(mask[t0:t0 + rb], s, NEG)
            m_prev = m_sc[rows, :]
            m_new = jnp.maximum(m_prev, jnp.max(s, axis=-1, keepdims=True))
            alpha = jnp.exp2(m_prev - m_new)
            pr = jnp.concatenate([jnp.exp2(s[:, c * LANE:(c + 1) * LANE] - m_new).astype(v_ref.dtype)
                                  for c in range(tk // LANE)], axis=1)
            pv = jnp.dot(pr, v_ext, preferred_element_type=F32)
            acc_sc[rows, :] = jnp.concatenate([alpha] * (dv // LANE + 1), axis=1) * acc_sc[rows, :] + pv
            m_sc[rows, :] = m_new

    if mode == "bias":
        compute(False)
    else:
        pl.when((fl & 4) != 0)(functools.partial(compute, True))
        pl.when((fl & 4) == 0)(functools.partial(compute, False))

    @pl.when((fl & 2) != 0)
    def _():
        for g in range(G):
            rows = slice(g * tq, (g + 1) * tq)
            o = acc_sc[rows, 0:dv] / acc_sc[rows, dv:dv + LANE]
            o_ref[0, :, g * dv:(g + 1) * dv] = o.astype(o_ref.dtype)


def _flash(q, k, v, *, mode="causal", window=None, bias=None, sel=None, expand=None,
           tq=512, tk=512, rb=128, heads_per_step=1, out_dtype=None):
    B, Hq, S, dk = q.shape
    Hkv, dv = k.shape[1], v.shape[-1]
    G = Hq // Hkv
    kvh = 1
    if heads_per_step > 1:
        assert G == 1 and Hq % heads_per_step == 0
        G = kvh = heads_per_step
    tq, tk = _tile(S, tq, 8), _tile(S, tk)
    qi, kj, fl = _pairs(S, tq, tk, window)
    P = int(qi.shape[0])
    in_specs = [pl.BlockSpec((1, G, tq, dk), lambda b, h, p, qi, kj, fl: (b, h, qi[p], 0)),
                pl.BlockSpec((1, kvh, tk, dk), lambda b, h, p, qi, kj, fl: (b, h, kj[p], 0)),
                pl.BlockSpec((1, kvh, tk, dv), lambda b, h, p, qi, kj, fl: (b, h, kj[p], 0))]
    args = [q, k, v]
    if mode == "bias":
        in_specs.append(pl.BlockSpec((1, 1, tq, tk), lambda b, h, p, qi, kj, fl: (b, kj[p], qi[p], 0)))
        args.append(bias)
    elif mode == "sel":
        in_specs += [pl.BlockSpec((1, 1, tq, LANE), lambda b, h, p, qi, kj, fl: (b, h, qi[p], 0)),
                     pl.BlockSpec((LANE, tk), lambda b, h, p, qi, kj, fl: (0, kj[p]))]
        args += [sel, expand]
    kern = functools.partial(_flash_kernel, G=G, tq=tq, tk=tk, rb=min(rb, tq), mode=mode, window=window)
    return pl.pallas_call(
        kern,
        out_shape=jax.ShapeDtypeStruct((B, S, Hq * dv), CDT if out_dtype is None else out_dtype),
        grid_spec=pltpu.PrefetchScalarGridSpec(
            num_scalar_prefetch=3,
            grid=(B, Hkv // kvh, P),
            in_specs=in_specs,
            out_specs=pl.BlockSpec((1, tq, G * dv), lambda b, h, p, qi, kj, fl: (b, qi[p], h)),
            scratch_shapes=[pltpu.VMEM((G * tq, LANE), F32), pltpu.VMEM((G * tq, dv + LANE), F32)]),
        compiler_params=_params(("parallel", "parallel", "arbitrary")),
    )(qi, kj, fl, *args)


def _mla_qprep_kernel(x_ref, cos_ref, sin_ref, gn_ref, gr_ref, o_ref, *, H, scale):
    ts = x_ref.shape[1]
    lane = lax.broadcasted_iota(I32, (ts, LANE), 1)
    lo = lane < 64
    cos4, sin4 = cos_ref[0], sin_ref[0]
    for h in range(H):
        xn = x_ref[0, :, h * LANE:(h + 1) * LANE]
        o_ref[0, h, :, 0:LANE] = (_rms128(xn, gn_ref[...]) * scale).astype(o_ref.dtype)
    for j in range(H // 2):
        xr = x_ref[0, :, (H + j) * LANE:(H + j + 1) * LANE]
        inv = lax.rsqrt(_sumsq128(xr, halves=True) * (1.0 / 64.0) + EPS)
        r = _rope64pair(xr * inv * gr_ref[...], cos4, sin4, lane) * scale
        o_ref[0, 2 * j, :, LANE:2 * LANE] = jnp.where(lo, r, 0.0).astype(o_ref.dtype)
        o_ref[0, 2 * j + 1, :, LANE:2 * LANE] = jnp.where(lo, pltpu.roll(r, 64, 1), 0.0).astype(o_ref.dtype)


def _mla_kvprep_kernel(x_ref, kr_ref, cos_ref, sin_ref, gn_ref, gr_ref, k_ref, v_ref, *, H):
    ts = x_ref.shape[1]
    lane = lax.broadcasted_iota(I32, (ts, LANE), 1)
    kr = kr_ref[0]
    inv = lax.rsqrt(_sumsq128(kr) * (1.0 / 64.0) + EPS)
    r = _rope64pair(kr * inv * gr_ref[...], cos_ref[0], sin_ref[0], lane).astype(k_ref.dtype)
    for h in range(H):
        xn = x_ref[0, :, h * LANE:(h + 1) * LANE]
        k_ref[0, h, :, 0:LANE] = _rms128(xn, gn_ref[...]).astype(k_ref.dtype)
        k_ref[0, h, :, LANE:2 * LANE] = r
        v_ref[0, h] = x_ref[0, :, (H + h) * LANE:(H + h + 1) * LANE].astype(v_ref.dtype)


def _mla(p, kr_block, pos, q_a_norm, kv_a_norm, w_uq, w_ukv, q_norm, k_norm):
    B, S, _ = p.shape
    H = MLA_HEADS
    N = B * S
    cqn = _norm(p, q_a_norm, width=MLA_Q_RANK, col_block=0)
    ckvn = _norm(p, kv_a_norm, width=MLA_KV_RANK, col_block=MLA_Q_RANK // MLA_KV_RANK)
    hh = np.arange(H)[:, None]
    q_perm = np.concatenate([(hh * 192 + np.arange(128)).ravel(), (hh * 192 + 128 + np.arange(64)).ravel()])
    kv_perm = np.concatenate([(hh * 256 + np.arange(128)).ravel(), (hh * 256 + 128 + np.arange(128)).ravel()])
    q_raw = _mm(cqn.reshape(N, -1), w_uq[:, q_perm].astype(CDT), tn=1024, tk=MLA_Q_RANK).reshape(B, S, -1)
    kv_raw = _mm(ckvn.reshape(N, -1), w_ukv[:, kv_perm].astype(CDT), tn=1024, tk=MLA_KV_RANK).reshape(B, S, -1)
    cos4, sin4 = _tables64pair(pos)
    ts = _tile(S, 256, 8)
    scale = (MLA_NOPE + MLA_ROPE) ** -0.5 * LOG2E
    gr = q_norm[MLA_NOPE:]
    tab = pl.BlockSpec((1, ts, LANE), lambda b, i: (b, i, 0))
    vec = pl.BlockSpec((1, LANE), lambda b, i: (0, 0))
    q = pl.pallas_call(
        functools.partial(_mla_qprep_kernel, H=H, scale=scale),
        out_shape=jax.ShapeDtypeStruct((B, H, S, 2 * LANE), CDT),
        grid=(B, S // ts),
        in_specs=[pl.BlockSpec((1, ts, H * 192), lambda b, i: (b, i, 0)), tab, tab, vec, vec],
        out_specs=pl.BlockSpec((1, H, ts, 2 * LANE), lambda b, i: (b, 0, i, 0)),
        compiler_params=_params(("parallel", "parallel")),
    )(q_raw, cos4, sin4, q_norm[:MLA_NOPE].reshape(1, -1), jnp.concatenate([gr, gr]).reshape(1, -1))
    gkr = jnp.concatenate([k_norm[MLA_NOPE:], jnp.zeros((64,), F32)])
    k, v = pl.pallas_call(
        functools.partial(_mla_kvprep_kernel, H=H),
        out_shape=(jax.ShapeDtypeStruct((B, H, S, 2 * LANE), CDT),
                   jax.ShapeDtypeStruct((B, H, S, LANE), CDT)),
        grid=(B, S // ts),
        in_specs=[pl.BlockSpec((1, ts, H * 256), lambda b, i: (b, i, 0)),
                  pl.BlockSpec((1, ts, LANE), lambda b, i: (b, i, kr_block)), tab, tab, vec, vec],
        out_specs=(pl.BlockSpec((1, H, ts, 2 * LANE), lambda b, i: (b, 0, i, 0)),
                   pl.BlockSpec((1, H, ts, LANE), lambda b, i: (b, 0, i, 0))),
        compiler_params=_params(("parallel", "parallel")),
    )(kv_raw, p, cos4, sin4, k_norm[:MLA_NOPE].reshape(1, -1), gkr.reshape(1, -1))
    return _flash(q, k, v, mode="causal", tq=1024, tk=1024, rb=256, heads_per_step=4)


def _nsa_prep_kernel(q_ref, kc_ref, vc_ref, ks_ref, vs_ref, kw_ref, vw_ref, cos_ref, sin_ref, g_ref,
                     qo, kso, vso, kwo, vwo, kco, vco, *, H, G, scale):
    cosf, sinf = cos_ref[0], sin_ref[0]
    for h in range(H):
        y = _rms128(q_ref[0, :, h * LANE:(h + 1) * LANE], g_ref[0:1, :])
        qo[0, h] = (_rope128(y, cosf, sinf) * scale).astype(qo.dtype)
    for g in range(G):
        sl = slice(g * LANE, (g + 1) * LANE)
        kso[0, g] = _rope128(_rms128(ks_ref[0, :, sl], g_ref[2:3, :]), cosf, sinf).astype(kso.dtype)
        kwo[0, g] = _rope128(_rms128(kw_ref[0, :, sl], g_ref[3:4, :]), cosf, sinf).astype(kwo.dtype)
        vso[0, g] = vs_ref[0, :, sl].astype(vso.dtype)
        vwo[0, g] = vw_ref[0, :, sl].astype(vwo.dtype)
        kco[0, g] = kc_ref[0, :, sl].astype(kco.dtype)
        vco[0, g] = vc_ref[0, :, sl].astype(vco.dtype)


def _compress_kernel(xk_ref, xv_ref, wk_ref, wv_ref, pek_ref, pev_ref, g_ref, cos_ref, sin_ref,
                     kc_ref, vc_ref):
    nc = xk_ref.shape[2]

    def comp(x_ref, w_ref, pe_ref):
        y = jnp.dot(x_ref[0, 0], w_ref[...], preferred_element_type=F32)
        c = jnp.dot(pe_ref[...], w_ref[...], preferred_element_type=F32)
        const = c[0:1, 0:LANE] + c[1:2, LANE:2 * LANE]
        return y[:, 0:LANE] + pltpu.roll(y[:, LANE:2 * LANE], nc - 1, 0) + const

    kc = comp(xk_ref, wk_ref, pek_ref)
    kc_ref[0, 0] = _rope128(_rms128(kc, g_ref[...]), cos_ref[0], sin_ref[0]).astype(kc_ref.dtype)
    vc_ref[0, 0] = comp(xv_ref, wv_ref, pev_ref).astype(vc_ref.dtype)


def _cmp_attn_kernel(q_ref, kc_ref, vc_ref, m_ref, o_ref, sel_ref, *, G, tq, n_cmp, n_sel, dv):
    i = pl.program_id(2)
    nc = kc_ref.shape[2]
    q = q_ref[0].reshape(G * tq, q_ref.shape[-1])
    s = lax.dot_general(q, kc_ref[0, 0], (((1,), (1,)), ((), ())), preferred_element_type=F32)
    s = s.reshape(G, tq, nc)
    t = i * tq + lax.broadcasted_iota(I32, (tq, nc), 0)
    n = lax.broadcasted_iota(I32, (tq, nc), 1)
    mask = (n * NSA_CMP_STRIDE + (NSA_CMP_LEN - 1) <= t) & (n < n_cmp)
    s = jnp.where(mask[None], s, NEG)
    mx = jnp.max(s, axis=-1, keepdims=True)
    e = jnp.where(mask[None], jnp.exp2(s - mx), 0.0)
    l = jnp.sum(e, axis=-1, keepdims=True)
    pc = e / jnp.where(l > 0.0, l, 1.0)
    o = jnp.dot(pc.reshape(G * tq, nc).astype(vc_ref.dtype), vc_ref[0, 0], preferred_element_type=F32)
    for g in range(G):
        o_ref[0, :, g * dv:(g + 1) * dv] = o[g * tq:(g + 1) * tq].astype(o_ref.dtype)

    ps = jnp.sum(pc, axis=0)
    hi = ps.astype(CDT)
    lo_part = (ps - hi.astype(F32)).astype(CDT)
    imp = (jnp.dot(hi, m_ref[...], preferred_element_type=F32)
           + jnp.dot(lo_part, m_ref[...], preferred_element_type=F32))
    blk = lax.broadcasted_iota(I32, (tq, LANE), 1)
    cur = (i * tq + lax.broadcasted_iota(I32, (tq, LANE), 0)) >> (NSA_SLC_LEN.bit_length() - 1)
    forced = (blk == 0) | (blk == cur) | (blk == cur - 1)
    imp = jnp.where(forced, FORCE, jnp.where(blk <= cur, imp, NEG))
    v = imp.T
    rowi = lax.broadcasted_iota(I32, (LANE, tq), 0)

    def take(_, carry):
        v, chosen = carry
        mval = jnp.max(v, axis=0, keepdims=True)
        first = jnp.min(jnp.where(v == mval, rowi, LANE), axis=0, keepdims=True)
        hit = rowi == first
        chosen = jnp.where(hit & (mval > NEG_HALF), 1.0, chosen)
        return jnp.where(hit, REMOVED, v), chosen

    _, chosen = lax.fori_loop(0, n_sel, take, (v, jnp.zeros((LANE, tq), F32)))
    sel_ref[0, 0] = chosen.T.astype(sel_ref.dtype)


def _nsa_combine_kernel(oc_ref, os_ref, ow_ref, g_ref, o_ref, *, H, dv):
    gate = _sigmoid(g_ref[0])
    for h in range(H):
        sl = slice(h * dv, (h + 1) * dv)
        o = (oc_ref[0, :, sl] * gate[:, h:h + 1] + os_ref[0, :, sl] * gate[:, H + h:H + h + 1]
             + ow_ref[0, :, sl] * gate[:, 2 * H + h:2 * H + h + 1])
        o_ref[0, :, sl] = o.astype(o_ref.dtype)


def _nsa(p, q_block, kv_block0, gate_block, pos, qk_norm, cmp_pos, cmp_w):
    B, S, _ = p.shape
    H, G, DH = NSA_HEADS, NSA_KV_GROUPS, NSA_HEAD_DIM
    HPG = H // G
    scale = DH ** -0.5 * LOG2E
    ts = _tile(S, 256, 8)
    cosf, sinf = _tables128(pos)
    tab = pl.BlockSpec((1, ts, LANE), lambda b, i: (b, i, 0))
    kvspec = [pl.BlockSpec((1, ts, G * DH), functools.partial(lambda b, i, m: (b, i, kv_block0 + m), m=m))
              for m in range(6)]
    head_out = lambda n: pl.BlockSpec((1, n, ts, DH), lambda b, i: (b, 0, i, 0))
    kv_shape = jax.ShapeDtypeStruct((B, G, S, DH), CDT)
    q, ks, vs, kw, vw, kcr, vcr = pl.pallas_call(
        functools.partial(_nsa_prep_kernel, H=H, G=G, scale=scale),
        out_shape=(jax.ShapeDtypeStruct((B, H, S, DH), CDT),) + (kv_shape,) * 6,
        grid=(B, S // ts),
        in_specs=[pl.BlockSpec((1, ts, H * DH), lambda b, i: (b, i, q_block))] + kvspec
                 + [tab, tab, pl.BlockSpec((4, DH), lambda b, i: (0, 0))],
        out_specs=(head_out(H),) + (head_out(G),) * 6,
        compiler_params=_params(("parallel", "parallel")),
    )(p, p, p, p, p, p, p, cosf, sinf, qk_norm)

    half = NSA_CMP_LEN // 2
    nc = S // NSA_CMP_STRIDE
    n_cmp = (S - NSA_CMP_LEN) // NSA_CMP_STRIDE + 1
    cmp_end = jnp.minimum(jnp.arange(nc) * NSA_CMP_STRIDE + NSA_CMP_LEN - 1, S - 1)
    ccos, csin = _tables128(pos[:, cmp_end])
    wcat = lambda w: jnp.concatenate([w[:half].reshape(half * DH, DH), w[half:].reshape(half * DH, DH)], 1).astype(CDT)
    pecat = lambda pe: jnp.zeros((8, half * DH), F32).at[0].set(pe[:half].reshape(-1)).at[1].set(
        pe[half:].reshape(-1)).astype(CDT)
    xspec = pl.BlockSpec((1, 1, nc, half * DH), lambda b, g: (b, g, 0, 0))
    wspec = pl.BlockSpec((half * DH, 2 * DH), lambda b, g: (0, 0))
    pespec = pl.BlockSpec((8, half * DH), lambda b, g: (0, 0))
    cspec = pl.BlockSpec((1, 1, nc, DH), lambda b, g: (b, g, 0, 0))
    ctab = pl.BlockSpec((1, nc, DH), lambda b, g: (b, 0, 0))
    kc, vc = pl.pallas_call(
        _compress_kernel,
        out_shape=(jax.ShapeDtypeStruct((B, G, nc, DH), CDT),) * 2,
        grid=(B, G),
        in_specs=[xspec, xspec, wspec, wspec, pespec, pespec,
                  pl.BlockSpec((1, DH), lambda b, g: (0, 0)), ctab, ctab],
        out_specs=(cspec, cspec),
        compiler_params=_params(("parallel", "parallel")),
    )(kcr.reshape(B, G, nc, half * DH), vcr.reshape(B, G, nc, half * DH), wcat(cmp_w[0]), wcat(cmp_w[1]),
      pecat(cmp_pos[0]), pecat(cmp_pos[1]), qk_norm[1].reshape(1, DH), ccos, csin)

    n_slc = S // NSA_SLC_LEN
    assert n_slc <= LANE
    r, cl = NSA_SLC_LEN // NSA_CMP_STRIDE, NSA_CMP_LEN // NSA_CMP_STRIDE
    m_np = np.zeros((nc, LANE), np.float32)
    for j in range(n_slc):
        for a in range(r):
            for c in range(cl):
                ci = j * r + a + c - (cl - 1)
                if 0 <= ci < n_cmp:
                    m_np[ci, j] += 1.0
    tq = _tile(S, 512, 8)
    o_c, sel = pl.pallas_call(
        functools.partial(_cmp_attn_kernel, G=HPG, tq=tq, n_cmp=n_cmp, n_sel=min(NSA_N_SEL, n_slc), dv=DH),
        out_shape=(jax.ShapeDtypeStruct((B, S, H * DH), CDT),
                   jax.ShapeDtypeStruct((B, G, S, LANE), CDT)),
        grid=(B, G, S // tq),
        in_specs=[pl.BlockSpec((1, HPG, tq, DH), lambda b, g, i: (b, g, i, 0)),
                  pl.BlockSpec((1, 1, nc, DH), lambda b, g, i: (b, g, 0, 0)),
                  pl.BlockSpec((1, 1, nc, DH), lambda b, g, i: (b, g, 0, 0)),
                  pl.BlockSpec((nc, LANE), lambda b, g, i: (0, 0))],
        out_specs=(pl.BlockSpec((1, tq, HPG * DH), lambda b, g, i: (b, i, g)),
                   pl.BlockSpec((1, 1, tq, LANE), lambda b, g, i: (b, g, i, 0))),
        compiler_params=_params(("parallel", "parallel", "parallel")),
    )(q, kc, vc, jnp.asarray(m_np, CDT))

    expand = jnp.asarray((np.arange(S)[None, :] // NSA_SLC_LEN) == np.arange(LANE)[:, None], CDT)
    o_s = _flash(q, ks, vs, mode="sel", sel=sel, expand=expand, tq=1024, tk=1024)
    o_w = _flash(q, kw, vw, mode="window", window=NSA_WINDOW, tq=512, tk=512)
    ospec = pl.BlockSpec((1, ts, H * DH), lambda b, i: (b, i, 0))
    return pl.pallas_call(
        functools.partial(_nsa_combine_kernel, H=H, dv=DH),
        out_shape=jax.ShapeDtypeStruct((B, S, H * DH), CDT),
        grid=(B, S // ts),
        in_specs=[ospec, ospec, ospec, pl.BlockSpec((1, ts, LANE), lambda b, i: (b, i, gate_block))],
        out_specs=ospec,
        compiler_params=_params(("parallel", "parallel")),
    )(o_c, o_s, o_w, p)


def _dsa_prep_kernel(q_ref, k_ref, v_ref, qi_ref, ki_ref, wi_ref, cos_ref, sin_ref, cos4_ref, sin4_ref,
                     g_ref, gi_ref, qo, ko, vo, qio, kilo, kihi, wio, *, H, HKV, scale, wscale):
    ts = q_ref.shape[1]
    lane = lax.broadcasted_iota(I32, (ts, LANE), 1)
    cosf, sinf, cos4, sin4 = cos_ref[0], sin_ref[0], cos4_ref[0], sin4_ref[0]
    for h in range(H):
        y = _rms128(q_ref[0, :, h * LANE:(h + 1) * LANE], g_ref[0:1, :])
        qo[0, h] = (_rope128(y, cosf, sinf) * scale).astype(qo.dtype)
    for h in range(HKV):
        sl = slice(h * LANE, (h + 1) * LANE)
        ko[0, h] = _rope128(_rms128(k_ref[0, :, sl], g_ref[1:2, :]), cosf, sinf).astype(ko.dtype)
        vo[0, h] = v_ref[0, :, sl].astype(vo.dtype)
    for j in range(qi_ref.shape[2] // LANE):
        qio[0, j] = _rope64pair(qi_ref[0, :, j * LANE:(j + 1) * LANE], cos4, sin4, lane).astype(qio.dtype)
    ki = ki_ref[0]
    inv = lax.rsqrt(_sumsq128(ki) * (1.0 / 64.0) + EPS)
    r = _rope64pair(ki * inv * gi_ref[...], cos4, sin4, lane)
    kilo[0] = r.astype(kilo.dtype)
    kihi[0] = pltpu.roll(r, 64, 1).astype(kihi.dtype)
    wio[0] = wi_ref[0] * wscale


def _indexer_kernel(qi_ref, kj_ref, fl_ref, q_ref, klo_ref, khi_ref, w_ref, o_ref, wb_sc, sc_sc, acc_sc,
                    *, tq, tk, topk, n_pairs, rg):
    p = pl.program_id(1)
    qi, kj, fl = qi_ref[p], kj_ref[p], fl_ref[p]
    n_tiles = o_ref.shape[1]

    @pl.when((fl & 1) != 0)
    def _():
        w = w_ref[0]
        for h in range(2 * n_pairs):
            wb_sc[h] = jnp.broadcast_to(w[:, h:h + 1], (tq, LANE))

    acc_sc[...] = jnp.zeros((tq, tk), F32)
    cw = min(tk, 2 * LANE)

    def pair(j, _):
        q = q_ref[0, j]
        wa = jnp.tile(wb_sc[2 * j], (1, cw // LANE))
        wb = jnp.tile(wb_sc[2 * j + 1], (1, cw // LANE))
        for c in range(tk // cw):
            cols = slice(c * cw, (c + 1) * cw)
            sa = lax.dot_general(q, klo_ref[0, cols, :], (((1,), (1,)), ((), ())), preferred_element_type=F32)
            sb = lax.dot_general(q, khi_ref[0, cols, :], (((1,), (1,)), ((), ())), preferred_element_type=F32)
            acc_sc[:, cols] += wa * jnp.maximum(sa, 0.0) + wb * jnp.maximum(sb, 0.0)
        return 0

    lax.fori_loop(0, n_pairs, pair, 0, unroll=8)
    score = acc_sc[...]
    row = qi * tq + lax.broadcasted_iota(I32, (tq, tk), 0)
    col = kj * tk + lax.broadcasted_iota(I32, (tq, tk), 1)
    score = jnp.where(col <= row, score, NEG)
    bits = pltpu.bitcast(score, I32)
    key = bits ^ ((bits >> 31) & 0x7FFFFFFF)
    sc_sc[kj] = key

    @pl.when((fl & 2) != 0)
    def _():
        n_chunks = kj + 1
        nh_bits = int(np.float32(NEG_HALF).view(np.int32))
        key_neg_half = nh_bits ^ 0x7FFFFFFF if nh_bits < 0 else nh_bits
        for g in range(tq // rg):
            rows = pl.ds(g * rg, rg)

            def bit_step(state):
                b, thr, n_ge, _ = state
                cand = thr + jnp.left_shift(jnp.int32(1), 31 - b)

                def count(c, cnt):
                    blk = sc_sc[c, rows, :]
                    for u in range(tk // LANE):
                        cnt = cnt + (blk[:, u * LANE:(u + 1) * LANE] >= cand).astype(I32)
                    return cnt

                cnt = lax.fori_loop(0, n_chunks, count, jnp.zeros((rg, LANE), I32))
                tot = jnp.sum(cnt, axis=1, keepdims=True)
                take = tot >= topk
                n_ge = jnp.where(take, tot, n_ge)
                return b + 1, jnp.where(take, cand, thr), n_ge, jnp.max(n_ge)

            start = (jnp.int32(0), jnp.full((rg, LANE), -2**31, I32),
                     jnp.full((rg, LANE), 2**30, I32), jnp.int32(2**30))
            _, thr, _, _ = lax.while_loop(lambda st: (st[0] < 32) & (st[3] > topk), bit_step, start)
            thr = jnp.maximum(thr, key_neg_half + 1)
            thr_t = jnp.tile(thr, (1, tk // LANE))

            def emit(c, _):
                o_ref[0, c, rows, :] = jnp.where(sc_sc[c, rows, :] >= thr_t, 0.0, NEG).astype(o_ref.dtype)
                return 0

            def emit_masked(c, _):
                o_ref[0, c, rows, :] = jnp.full((rg, tk), NEG, o_ref.dtype)
                return 0

            lax.fori_loop(0, n_chunks, emit, 0)
            lax.fori_loop(n_chunks, n_tiles, emit_masked, 0)


def _dsa(p, pos, qk_norm, idx_k_norm):
    B, S, _ = p.shape
    H, HKV, DH = DSA_HEADS, DSA_KV_HEADS, DSA_HEAD_DIM
    NP = IDX_HEADS // 2
    ts = _tile(S, 256, 8)
    cosf, sinf = _tables128(pos)
    cos4, sin4 = _tables64pair(pos)
    tab = pl.BlockSpec((1, ts, LANE), lambda b, i: (b, i, 0))
    kw = HKV * DH
    gi = jnp.concatenate([idx_k_norm, jnp.zeros((LANE - IDX_DIM,), F32)]).reshape(1, LANE)
    head_out = lambda n: pl.BlockSpec((1, n, ts, DH), lambda b, i: (b, 0, i, 0))
    q, k, v, qidx, kilo, kihi, wi = pl.pallas_call(
        functools.partial(_dsa_prep_kernel, H=H, HKV=HKV, scale=DH ** -0.5 * LOG2E,
                          wscale=IDX_HEADS ** -0.5 * IDX_DIM ** -0.5),
        out_shape=(jax.ShapeDtypeStruct((B, H, S, DH), CDT), jax.ShapeDtypeStruct((B, HKV, S, DH), CDT),
                   jax.ShapeDtypeStruct((B, HKV, S, DH), CDT), jax.ShapeDtypeStruct((B, NP, S, LANE), CDT),
                   jax.ShapeDtypeStruct((B, S, LANE), CDT), jax.ShapeDtypeStruct((B, S, LANE), CDT),
                   jax.ShapeDtypeStruct((B, S, LANE), F32)),
        grid=(B, S // ts),
        in_specs=[pl.BlockSpec((1, ts, H * DH), lambda b, i: (b, i, 0)),
                  pl.BlockSpec((1, ts, kw), lambda b, i: (b, i, H * DH // kw)),
                  pl.BlockSpec((1, ts, kw), lambda b, i: (b, i, H * DH // kw + 1)),
                  pl.BlockSpec((1, ts, NP * LANE), lambda b, i: (b, i, (H * DH + 2 * kw) // (NP * LANE))),
                  pl.BlockSpec((1, ts, LANE), lambda b, i: (b, i, (H * DH + 2 * kw + NP * LANE) // LANE)),
                  pl.BlockSpec((1, ts, LANE), lambda b, i: (b, i, (H * DH + 2 * kw + NP * LANE) // LANE + 1)),
                  tab, tab, tab, tab,
                  pl.BlockSpec((2, DH), lambda b, i: (0, 0)), pl.BlockSpec((1, LANE), lambda b, i: (0, 0))],
        out_specs=(head_out(H), head_out(HKV), head_out(HKV), head_out(NP), tab, tab, tab),
        compiler_params=_params(("parallel", "parallel")),
    )(p, p, p, p, p, p, cosf, sinf, cos4, sin4, qk_norm, gi)

    topk = min(DSA_TOPK_MAX, S // 4)
    tq, tk = _tile(S, 256, 8), _tile(S, 1024)
    qi_t, kj_t, fl_t = _pairs(S, tq, tk)
    bias = pl.pallas_call(
        functools.partial(_indexer_kernel, tq=tq, tk=tk, topk=topk, n_pairs=NP, rg=min(128, tq)),
        out_shape=jax.ShapeDtypeStruct((B, S // tk, S, tk), CDT),
        grid_spec=pltpu.PrefetchScalarGridSpec(
            num_scalar_prefetch=3,
            grid=(B, int(qi_t.shape[0])),
            in_specs=[pl.BlockSpec((1, NP, tq, LANE), lambda b, p, qi, kj, fl: (b, 0, qi[p], 0)),
                      pl.BlockSpec((1, tk, LANE), lambda b, p, qi, kj, fl: (b, kj[p], 0)),
                      pl.BlockSpec((1, tk, LANE), lambda b, p, qi, kj, fl: (b, kj[p], 0)),
                      pl.BlockSpec((1, tq, LANE), lambda b, p, qi, kj, fl: (b, qi[p], 0))],
            out_specs=pl.BlockSpec((1, S // tk, tq, tk), lambda b, p, qi, kj, fl: (b, 0, qi[p], 0)),
            scratch_shapes=[pltpu.VMEM((2 * NP, tq, LANE), F32), pltpu.VMEM((S // tk, tq, tk), I32),
                            pltpu.VMEM((tq, tk), F32)]),
        compiler_params=_params(("parallel", "arbitrary")),
    )(qi_t, kj_t, fl_t, qidx, kilo, kihi, wi)
    return _flash(q, k, v, mode="bias", bias=bias, tq=4 * tq, tk=tk)


def _router_kernel(l_ref, i_ref, p_ref, *, n_experts):
    x = l_ref[...]
    lane = lax.broadcasted_iota(I32, x.shape, 1)
    x = jnp.where(lane < n_experts, x, -jnp.inf)
    m1 = jnp.max(x, axis=1, keepdims=True)
    i1 = jnp.min(jnp.where(x == m1, lane, LANE), axis=1, keepdims=True)
    x2 = jnp.where(lane == i1, -jnp.inf, x)
    m2 = jnp.max(x2, axis=1, keepdims=True)
    i2 = jnp.min(jnp.where(x2 == m2, lane, LANE), axis=1, keepdims=True)
    e2 = jnp.exp(m2 - m1)
    p1 = 1.0 / (1.0 + e2)
    p2 = e2 / (1.0 + e2)
    i_ref[...] = jnp.where(lane == 0, i1, jnp.where(lane == 1, i2, 0))
    p_ref[...] = jnp.where(lane == 0, p1, jnp.where(lane == 1, p2, 0.0))


def _row_copy(src_hbm, row, dst_vmem, r, sem):
    return pltpu.make_async_copy(src_hbm.at[pl.ds(row, 1), :], dst_vmem.at[pl.ds(r, 1), :], sem)


def _gather_pipeline(n_steps, tm, copies):
    i = pl.program_id(0)
    slot = i % 2

    def start_tile(tile, slot):
        def body(r2, _):
            for u in range(2):
                r = 2 * r2 + u
                for src, idx_ref, dst, sem in copies(slot):
                    _row_copy(src, idx_ref[tile * tm + r], dst, r, sem).start(priority=u)
            return 0
        lax.fori_loop(0, tm // 2, body, 0)

    def wait_tile(slot):
        def body(r, _):
            for src, _, dst, sem in copies(slot):
                _row_copy(src, 0, dst, r, sem).wait()
            return 0
        lax.fori_loop(0, tm, body, 0)

    @pl.when(i == 0)
    def _():
        start_tile(0, 0)

    @pl.when(i + 1 < n_steps)
    def _():
        start_tile(i + 1, 1 - slot)

    wait_tile(slot)
    return slot


def _dispatch_kernel(tok_ref, h_hbm, o_ref, buf, sem, *, tm, n_steps):
    slot = _gather_pipeline(n_steps, tm, lambda s: [(h_hbm, tok_ref, buf.at[s], sem.at[s])])
    o_ref[...] = buf[slot].astype(o_ref.dtype)


def _combine_kernel(s1_ref, s2_ref, y_hbm, x_ref, g_ref, p_ref, o_ref, buf1, buf2, sem, *, tm, n_steps):
    slot = _gather_pipeline(n_steps, tm, lambda s: [(y_hbm, s1_ref, buf1.at[s], sem.at[s]),
                                                    (y_hbm, s2_ref, buf2.at[s], sem.at[s])])
    p = p_ref[...]
    o_ref[...] = x_ref[...] + g_ref[0] * (p[:, 0:1] * buf1[slot] + p[:, 1:2] * buf2[slot])


def _moe(h, h32, x, g_f, w_router, w_gate, w_up, w_down, S):
    N, D = h.shape
    E, _, DE = w_gate.shape
    wr = jnp.zeros((D, LANE), F32).at[:, :E].set(w_router).astype(CDT)
    logits = _mm(h, wr, tn=LANE, tk=D)
    tr = _tile(N, 1024, 8)
    spec = pl.BlockSpec((tr, LANE), lambda i: (i, 0))
    idx, prob = pl.pallas_call(
        functools.partial(_router_kernel, n_experts=E),
        out_shape=(jax.ShapeDtypeStruct((N, LANE), I32), jax.ShapeDtypeStruct((N, LANE), F32)),
        grid=(N // tr,),
        in_specs=[spec],
        out_specs=(spec, spec),
        compiler_params=_params(("parallel",)),
    )(logits)

    tm = min(512, N)
    n_rows = 2 * N + E * tm
    e_flat = jnp.concatenate([idx[:, 0], idx[:, 1]])
    onehot = (e_flat[:, None] == jnp.arange(E)[None, :]).astype(I32)
    csum = jnp.cumsum(onehot, axis=0)
    rank = jnp.take_along_axis(csum, e_flat[:, None], axis=1)[:, 0] - 1
    padded = (csum[-1] + tm - 1) // tm * tm
    ends = jnp.cumsum(padded)
    pos = (ends - padded)[e_flat] + rank
    tile_expert = jnp.minimum(jnp.sum(jnp.arange(n_rows // tm)[:, None] * tm >= ends[None, :], axis=1), E - 1)
    tok = jnp.tile(jnp.arange(N, dtype=I32), 2)
    row_token = jnp.zeros((n_rows,), I32).at[pos].set(tok)

    tg = min(256, N)
    xs = pl.pallas_call(
        functools.partial(_dispatch_kernel, tm=tg, n_steps=n_rows // tg),
        out_shape=jax.ShapeDtypeStruct((n_rows, D), CDT),
        grid_spec=pltpu.PrefetchScalarGridSpec(
            num_scalar_prefetch=1,
            grid=(n_rows // tg,),
            in_specs=[pl.BlockSpec(memory_space=pl.ANY)],
            out_specs=pl.BlockSpec((tg, D), lambda i, tok: (i, 0)),
            scratch_shapes=[pltpu.VMEM((2, tg, D), F32), pltpu.SemaphoreType.DMA((2,))]),
        compiler_params=_params(("arbitrary",)),
    )(row_token, h32)
    te = tile_expert.astype(I32)
    hid = _mm(xs, w_gate.astype(CDT), mode="swiglu", b2=w_up.astype(CDT), group=te, out_dtype=CDT,
              tm=tm, tn=DE, tk=2048)
    ys = _mm(hid, w_down.astype(CDT), group=te, tm=tm, tn=2048, tk=DE)
    return pl.pallas_call(
        functools.partial(_combine_kernel, tm=tg, n_steps=N // tg),
        out_shape=jax.ShapeDtypeStruct((N, D), F32),
        grid_spec=pltpu.PrefetchScalarGridSpec(
            num_scalar_prefetch=2,
            grid=(N // tg,),
            in_specs=[pl.BlockSpec(memory_space=pl.ANY),
                      pl.BlockSpec((tg, D), lambda i, s1, s2: (i, 0)),
                      pl.BlockSpec((1, 1, D), lambda i, s1, s2: ((i * tg) // S, 0, 0)),
                      pl.BlockSpec((tg, LANE), lambda i, s1, s2: (i, 0))],
            out_specs=pl.BlockSpec((tg, D), lambda i, s1, s2: (i, 0)),
            scratch_shapes=[pltpu.VMEM((2, tg, D), F32), pltpu.VMEM((2, tg, D), F32),
                            pltpu.SemaphoreType.DMA((2,))]),
        compiler_params=_params(("arbitrary",)),
    )(pos[:N].astype(I32), pos[N:].astype(I32), ys, x, g_f, prob)


def _pad_cols(blocks, total):
    cols = []
    for w, width in blocks:
        cols.append(w)
        if width > w.shape[1]:
            cols.append(jnp.zeros((w.shape[0], width - w.shape[1]), w.dtype))
    out = jnp.concatenate(cols, axis=1)
    if total > out.shape[1]:
        out = jnp.concatenate([out, jnp.zeros((out.shape[0], total - out.shape[1]), out.dtype)], axis=1)
    return out.astype(CDT)


def _round_up(n, m):
    return (n + m - 1) // m * m


def kernel(x, c, positions, ada_w, ada_b, ada_table, norm_g, ev_w_in, ev_w_out, mla_q_a_norm, mla_kv_a_norm, mla_w_uq, mla_w_ukv, mla_q_norm, mla_k_norm, nsa_qk_norm, nsa_cmp_pos, nsa_cmp_w, ffn_w_gate, ffn_w_up, ffn_w_down, od_w_in, od_w_out, dsa_qk_norm, idx_k_norm, moe_router, moe_w_gate, moe_w_up, moe_w_down):
    B, S, D = x.shape
    N = B * S
    depth = ada_table.shape[0]
    cond = _cond(c, ada_w, ada_b).reshape(B, 6, D)

    mla_in = MLA_Q_RANK + MLA_KV_RANK + MLA_ROPE
    nq = NSA_HEADS * NSA_HEAD_DIM
    nkv = 6 * NSA_KV_GROUPS * NSA_HEAD_DIM
    hn = np.arange(NSA_HEADS)
    gate_perm = np.concatenate([hn * 3 + r for r in range(3)])

    x2 = x.reshape(N, D)
    for l in range(depth):
        i = l // 2
        mod = cond + ada_table[l]
        sh_a, sc_a, g_a, sh_f, sc_f, g_f = [mod[:, j, None, :] for j in range(6)]
        h = _norm(x2.reshape(B, S, D), norm_g[l, 0], sc_a, sh_a).reshape(N, D)
        if l % 2 == 0:
            w = ev_w_in[i]
            nsa = w[:, mla_in:]
            blocks = [(w[:, :MLA_Q_RANK + MLA_KV_RANK], MLA_Q_RANK + MLA_KV_RANK),
                      (nsa[:, :nq + nkv], nq + nkv),
                      (w[:, MLA_Q_RANK + MLA_KV_RANK:mla_in], LANE),
                      (nsa[:, nq + nkv:][:, gate_perm], LANE)]
            width = MLA_Q_RANK + MLA_KV_RANK + nq + nkv + 2 * LANE
            w_in = _pad_cols(blocks, _round_up(width, 512))
            p = _mm(h, w_in).reshape(B, S, -1)
            off = MLA_Q_RANK + MLA_KV_RANK
            a_out = _mla(p, (off + nq + nkv) // LANE, positions, mla_q_a_norm[i], mla_kv_a_norm[i],
                         mla_w_uq[i], mla_w_ukv[i], mla_q_norm[i], mla_k_norm[i])
            b_out = _nsa(p, off // nq, (off + nq) // (NSA_KV_GROUPS * NSA_HEAD_DIM),
                         (off + nq + nkv) // LANE + 1, positions, nsa_qk_norm[i], nsa_cmp_pos[i], nsa_cmp_w[i])
            mix = (a_out.reshape(N, -1), b_out.reshape(N, -1))
            w_out = ev_w_out[i]
        else:
            w = od_w_in[i]
            main = DSA_HEADS * DSA_HEAD_DIM + 2 * DSA_KV_HEADS * DSA_HEAD_DIM + IDX_HEADS * IDX_DIM
            blocks = [(w[:, :main], main), (w[:, main:main + IDX_DIM], LANE), (w[:, main + IDX_DIM:], LANE)]
            w_in = _pad_cols(blocks, _round_up(main + 2 * LANE, 512))
            p = _mm(h, w_in).reshape(B, S, -1)
            mix = _dsa(p, positions, dsa_qk_norm[i], idx_k_norm[i]).reshape(N, -1)
            w_out = od_w_out[i]
        x2 = _mm(mix, w_out.astype(CDT), mode="res", x=x2, g=g_a, rows_per_batch=S)
        if l % 2 == 0:
            h = _norm(x2.reshape(B, S, D), norm_g[l, 1], sc_f, sh_f).reshape(N, D)
            hid = _mm(h, ffn_w_gate[i].astype(CDT), mode="swiglu", b2=ffn_w_up[i].astype(CDT), out_dtype=CDT)
            x2 = _mm(hid, ffn_w_down[i].astype(CDT), mode="res", x=x2, g=g_f, tn=1024, tk=2048,
                     rows_per_batch=S)
        else:
            h, h32 = _norm(x2.reshape(B, S, D), norm_g[l, 1], sc_f, sh_f, also_f32=True)
            x2 = _moe(h.reshape(N, D), h32.reshape(N, D), x2, g_f, moe_router[i], moe_w_gate[i], moe_w_up[i],
                      moe_w_down[i], S)
    return x2.reshape(B, S, D)
```
